```python
import jax, jax.numpy as jnp
from jax import lax
import numpy as np

D_MODEL = 2048
BATCH = 32
SEQ = 256
DEPTH = 2
DEC_BATCH = 8
DEC_SEQ = 1024
PAST_LEN = 512

GRID_W = 64
HEAD_DIM = 128
A_Q_HEADS = 8
A_KV_HEADS = 2
B_HEADS = 8
A_WIDTH = A_Q_HEADS * HEAD_DIM
A_KV_WIDTH = A_KV_HEADS * HEAD_DIM
B_WIDTH = B_HEADS * HEAD_DIM
L0_WIDTH = A_WIDTH + B_WIDTH
L0_IN = A_WIDTH + 2 * A_KV_WIDTH + 3 * B_WIDTH + L0_WIDTH
NA_WIN_H = 8
NA_WIN_W = 16
MLA_HEADS = 16
Q_LORA = 512
KV_LORA = 512
QK_NOPE = 128
QK_ROPE = 64
V_DIM = 128
L1_WIDTH = MLA_HEADS * V_DIM
L1_IN = Q_LORA + KV_LORA + QK_ROPE + L1_WIDTH
Q_BLOCK = 128
ROPE_THETA = 10000.0
EPS = 1e-6
NEG_INF = -1e30

kernel_name = 'hybrid_diffusion_prefix_step'


def rmsnorm(x, g):
    xf = x.astype(jnp.float32)
    y = xf * lax.rsqrt(jnp.mean(xf * xf, axis=-1, keepdims=True) + EPS)
    return (y * g.astype(jnp.float32)).astype(x.dtype)


def adaln(c, w, b):
    m = jax.nn.silu(c) @ w + b
    shift, scale, gate = jnp.split(m, 3, axis=-1)
    return shift[:, None], scale[:, None], gate[:, None]


def rope_half(x, ang):
    m = x.shape[-1] // 2
    cos = jnp.cos(ang)[:, None, :].astype(x.dtype)
    sin = jnp.sin(ang)[:, None, :].astype(x.dtype)
    x1, x2 = x[..., :m], x[..., m:]
    return jnp.concatenate([x1 * cos - x2 * sin, x2 * cos + x1 * sin], axis=-1)


def axial_rope(x):
    S, d = x.shape[1], x.shape[-1]
    a = d // 2
    t = jnp.arange(S)
    row = (t // GRID_W).astype(jnp.float32)
    col = (t % GRID_W).astype(jnp.float32)
    inv = 1.0 / (ROPE_THETA ** (jnp.arange(a // 2, dtype=jnp.float32) * 2.0 / a))
    return jnp.concatenate([rope_half(x[..., :a], row[:, None] * inv),
                            rope_half(x[..., a:], col[:, None] * inv)], axis=-1)


def blocked_attention(q, k, v, scale):
    B, S, Hkv, G, Dq = q.shape
    nb = S // Q_BLOCK
    qb = jnp.moveaxis(q.reshape(B, nb, Q_BLOCK, Hkv, G, Dq), 1, 0)

    def one_block(qblk):
        s = jnp.einsum('bqhgd,bkhd->bhgqk', qblk, k).astype(jnp.float32) * scale
        p = jax.nn.softmax(s, axis=-1).astype(v.dtype)
        return jnp.einsum('bhgqk,bkhd->bqhgd', p, v)

    out = lax.map(one_block, qb)
    return jnp.moveaxis(out, 0, 1).reshape(B, S, Hkv * G, v.shape[-1])


def neighbourhood_attention(q, k, v, k_ctx, v_ctx, rpb, scale):
    B, S, H, D = q.shape
    rows = S // GRID_W
    wh = min(NA_WIN_H, rows)
    ncb = GRID_W // NA_WIN_W
    ksw = 2 * NA_WIN_W
    row_start = jnp.clip(jnp.arange(rows) - wh // 2, 0, rows - wh)
    key_cols = jnp.clip(jnp.arange(ncb) * NA_WIN_W - NA_WIN_W // 2, 0, GRID_W - ksw)[:, None] + jnp.arange(ksw)
    q_cols = jnp.arange(GRID_W).reshape(ncb, NA_WIN_W)
    col_start = jnp.clip(q_cols - NA_WIN_W // 2, 0, GRID_W - NA_WIN_W)
    kc = key_cols[:, None, :]
    col_ok = (kc >= col_start[..., None]) & (kc < col_start[..., None] + NA_WIN_W)
    dx_idx = jnp.clip(kc - q_cols[..., None] + NA_WIN_W - 1, 0, 2 * NA_WIN_W - 2)
    q_g = q.reshape(B, rows, ncb, NA_WIN_W, H, D)
    k_g = k.reshape(B, rows, GRID_W, H, D)
    v_g = v.reshape(B, rows, GRID_W, H, D)
    n_loc = wh * ksw

    def row_block(r):
        q_r = lax.dynamic_index_in_dim(q_g, r, axis=1, keepdims=False)
        rs = row_start[r]
        k_blk = jnp.take(lax.dynamic_slice_in_dim(k_g, rs, wh, axis=1), key_cols, axis=2)
        v_blk = jnp.take(lax.dynamic_slice_in_dim(v_g, rs, wh, axis=1), key_cols, axis=2)
        dy_idx = rs + jnp.arange(wh) - r + NA_WIN_H - 1
        bias = rpb[:, dy_idx[:, None, None, None], dx_idx[None]]
        bias = jnp.where(col_ok[None, None], bias.astype(jnp.float32), NEG_INF)
        bias = jnp.transpose(bias, (0, 2, 3, 1, 4))[None]
        s_loc = jnp.einsum('bjqhd,bajkhd->bhjqak', q_r, k_blk).astype(jnp.float32) * scale + bias
        s_loc = s_loc.reshape(B, H, ncb, NA_WIN_W, n_loc)
        s_ctx = jnp.einsum('bjqhd,blhd->bhjql', q_r, k_ctx).astype(jnp.float32) * scale
        p = jax.nn.softmax(jnp.concatenate([s_loc, s_ctx], axis=-1), axis=-1).astype(v.dtype)
        p_loc = p[..., :n_loc].reshape(B, H, ncb, NA_WIN_W, wh, ksw)
        o = (jnp.einsum('bhjqak,bajkhd->bjqhd', p_loc, v_blk)
             + jnp.einsum('bhjql,blhd->bjqhd', p[..., n_loc:], v_ctx))
        return o.reshape(B, GRID_W, H, D)

    out = lax.map(row_block, jnp.arange(rows))
    return jnp.moveaxis(out, 0, 1).reshape(B, S, H, D)


def l0_split(h, w_in):
    B, S, _ = h.shape
    idx = np.cumsum([A_WIDTH, A_KV_WIDTH, A_KV_WIDTH, B_WIDTH, B_WIDTH, B_WIDTH]).tolist()
    q_a, k_a, v_a, q_b, k_b, v_b, gate = jnp.split(h @ w_in, idx, axis=-1)
    return (q_a.reshape(B, S, A_Q_HEADS, HEAD_DIM), k_a.reshape(B, S, A_KV_HEADS, HEAD_DIM),
            v_a.reshape(B, S, A_KV_HEADS, HEAD_DIM), q_b.reshape(B, S, B_HEADS, HEAD_DIM),
            k_b.reshape(B, S, B_HEADS, HEAD_DIM), v_b.reshape(B, S, B_HEADS, HEAD_DIM), gate)


def l0_merge(o_a, o_b, gate, w_out):
    B, S = gate.shape[:2]
    o = jnp.concatenate([o_a.reshape(B, S, A_WIDTH), o_b.reshape(B, S, B_WIDTH)], axis=-1)
    return (o * jax.nn.silu(gate)) @ w_out


def l0_context(h, w_in, q_norm, k_norm, w_out):
    B, L, _ = h.shape
    q_a, k_a, v_a, q_b, k_b, v_b, gate = l0_split(h, w_in)
    q_a = rmsnorm(q_a, q_norm)
    k_a = rmsnorm(k_a, k_norm)
    sc = HEAD_DIM ** -0.5
    o_a = blocked_attention(q_a.reshape(B, L, A_KV_HEADS, A_Q_HEADS // A_KV_HEADS, HEAD_DIM), k_a, v_a, sc)
    o_b = blocked_attention(q_b[:, :, :, None], k_b, v_b, sc)
    return l0_merge(o_a, o_b, gate, w_out), (k_a, v_a, k_b, v_b)


def l0_latent(h, ck_a, cv_a, ck_b, cv_b, w_in, q_norm, k_norm, rpb, w_out):
    B, S, _ = h.shape
    q_a, k_a, v_a, q_b, k_b, v_b, gate = l0_split(h, w_in)
    q_a = axial_rope(rmsnorm(q_a, q_norm))
    k_a = axial_rope(rmsnorm(k_a, k_norm))
    sc = HEAD_DIM ** -0.5
    o_a = blocked_attention(q_a.reshape(B, S, A_KV_HEADS, A_Q_HEADS // A_KV_HEADS, HEAD_DIM),
                            jnp.concatenate([ck_a, k_a], axis=1), jnp.concatenate([cv_a, v_a], axis=1), sc)
    o_b = neighbourhood_attention(q_b, k_b, v_b, ck_b, cv_b, rpb, sc)
    return l0_merge(o_a, o_b, gate, w_out)


def l1_split(h, w_in, q_a_norm, w_q_b, kv_a_norm):
    B, S, _ = h.shape
    q_lat, kv_lat, k_rope, gate = jnp.split(h @ w_in, np.cumsum([Q_LORA, KV_LORA, QK_ROPE]).tolist(), axis=-1)
    q = (rmsnorm(q_lat, q_a_norm) @ w_q_b).reshape(B, S, MLA_HEADS, QK_NOPE + QK_ROPE)
    ckv = rmsnorm(kv_lat, kv_a_norm)
    return q[..., :QK_NOPE], q[..., QK_NOPE:], ckv, k_rope, gate


def mla_attend(q_nope, q_rope, ckv, k_rope, w_kv_b):
    B, T, _ = ckv.shape
    kv = (ckv @ w_kv_b).reshape(B, T, MLA_HEADS, QK_NOPE + V_DIM)
    k = jnp.concatenate([kv[..., :QK_NOPE],
                         jnp.broadcast_to(k_rope[:, :, None, :], (B, T, MLA_HEADS, QK_ROPE))], axis=-1)
    q = jnp.concatenate([q_nope, q_rope], axis=-1)[:, :, :, None]
    return blocked_attention(q, k, kv[..., QK_NOPE:], (QK_NOPE + QK_ROPE) ** -0.5)


def l1_context(h, w_in, q_a_norm, w_q_b, kv_a_norm, w_kv_b, w_out):
    B, L, _ = h.shape
    q_nope, q_rope, ckv, k_rope, gate = l1_split(h, w_in, q_a_norm, w_q_b, kv_a_norm)
    o = mla_attend(q_nope, q_rope, ckv, k_rope, w_kv_b).reshape(B, L, L1_WIDTH)
    return (o * jax.nn.silu(gate)) @ w_out, (ckv, k_rope)


def l1_latent(h, c_ckv, c_krope, w_in, q_a_norm, w_q_b, kv_a_norm, w_kv_b, w_out):
    B, S, _ = h.shape
    q_nope, q_rope, ckv, k_rope, gate = l1_split(h, w_in, q_a_norm, w_q_b, kv_a_norm)
    q_rope = axial_rope(q_rope)
    k_rope = axial_rope(k_rope[:, :, None, :])[:, :, 0]
    o = mla_attend(q_nope, q_rope, jnp.concatenate([c_ckv, ckv], axis=1),
                   jnp.concatenate([c_krope, k_rope], axis=1), w_kv_b).reshape(B, S, L1_WIDTH)
    return (o * jax.nn.silu(gate)) @ w_out


def setup_inputs(seed: int = 0) -> dict:
    key = jax.random.key(seed)
    ks = jax.random.split(key, 28)
    f32 = jnp.float32

    def nrm(k, shape, scale=1.0):
        return jax.random.normal(k, shape, f32) * scale

    def gain(k, n):
        return 1.0 + 0.05 * jax.random.normal(k, (n,), f32)

    D = D_MODEL
    return {
        'x_prompt': nrm(ks[0], (BATCH, SEQ, D)),
        'x_sample': nrm(ks[1], (DEC_BATCH, DEC_SEQ, D)),
        'cache_l0_a_k': nrm(ks[2], (DEC_BATCH, PAST_LEN, A_KV_HEADS, HEAD_DIM)),
        'cache_l0_a_v': nrm(ks[3], (DEC_BATCH, PAST_LEN, A_KV_HEADS, HEAD_DIM)),
        'cache_l0_b_k': nrm(ks[4], (DEC_BATCH, PAST_LEN, B_HEADS, HEAD_DIM)),
        'cache_l0_b_v': nrm(ks[5], (DEC_BATCH, PAST_LEN, B_HEADS, HEAD_DIM)),
        'cache_l1_ckv': nrm(ks[6], (DEC_BATCH, PAST_LEN, KV_LORA)),
        'cache_l1_krope': nrm(ks[7], (DEC_BATCH, PAST_LEN, QK_ROPE)),
        'c': nrm(ks[8], (DEC_BATCH, D)),
        'c_ctx': nrm(ks[9], (D,)),
        'l0_norm': gain(ks[10], D),
        'l0_ada_w': nrm(ks[11], (D, 3 * D), 0.5 * D ** -0.5),
        'l0_ada_b': nrm(ks[12], (3 * D,), 0.02),
        'l0_w_in': nrm(ks[13], (D, L0_IN), D ** -0.5),
        'l0_q_norm': gain(ks[14], HEAD_DIM),
        'l0_k_norm': gain(ks[15], HEAD_DIM),
        'l0_rpb': nrm(ks[16], (B_HEADS, 2 * NA_WIN_H - 1, 2 * NA_WIN_W - 1), 0.2),
        'l0_w_out': nrm(ks[17], (L0_WIDTH, D), L0_WIDTH ** -0.5),
        'l1_norm': gain(ks[18], D),
        'l1_ada_w': nrm(ks[19], (D, 3 * D), 0.5 * D ** -0.5),
        'l1_ada_b': nrm(ks[20], (3 * D,), 0.02),
        'l1_w_in': nrm(ks[21], (D, L1_IN), D ** -0.5),
        'l1_q_a_norm': gain(ks[22], Q_LORA),
        'l1_w_q_b': nrm(ks[23], (Q_LORA, MLA_HEADS * (QK_NOPE + QK_ROPE)), Q_LORA ** -0.5),
        'l1_kv_a_norm': gain(ks[24], KV_LORA),
        'l1_w_kv_b': nrm(ks[25], (KV_LORA, MLA_HEADS * (QK_NOPE + V_DIM)), KV_LORA ** -0.5),
        'l1_w_out': nrm(ks[26], (L1_WIDTH, D), L1_WIDTH ** -0.5),
        'final_norm': gain(ks[27], D),
    }


def reference(x_prompt, x_sample, cache_l0_a_k, cache_l0_a_v, cache_l0_b_k, cache_l0_b_v,
              cache_l1_ckv, cache_l1_krope, c, c_ctx,
              l0_norm, l0_ada_w, l0_ada_b, l0_w_in, l0_q_norm, l0_k_norm, l0_rpb, l0_w_out,
              l1_norm, l1_ada_w, l1_ada_b, l1_w_in, l1_q_a_norm, l1_w_q_b, l1_kv_a_norm,
              l1_w_kv_b, l1_w_out, final_norm):
    norms = [l0_norm, l1_norm]
    ada_ws = [l0_ada_w, l1_ada_w]
    ada_bs = [l0_ada_b, l1_ada_b]
    caches = [(cache_l0_a_k, cache_l0_a_v, cache_l0_b_k, cache_l0_b_v), (cache_l1_ckv, cache_l1_krope)]

    x = x_prompt
    ctx_states = []
    for i in range(DEPTH):
        shift, scale, gate = adaln(c_ctx[None], ada_ws[i], ada_bs[i])
        h = rmsnorm(x, norms[i]) * (1.0 + scale) + shift
        if i % 2 == 0:
            out, st = l0_context(h, l0_w_in, l0_q_norm, l0_k_norm, l0_w_out)
        else:
            out, st = l1_context(h, l1_w_in, l1_q_a_norm, l1_w_q_b, l1_kv_a_norm, l1_w_kv_b, l1_w_out)
        x = x + gate * out
        ctx_states.append(st)
    y_prompt = rmsnorm(x, final_norm)
    new_l0_a_k, new_l0_a_v, new_l0_b_k, new_l0_b_v = ctx_states[0]
    new_l1_ckv, new_l1_krope = ctx_states[1]

    x = x_sample
    for i in range(DEPTH):
        shift, scale, gate = adaln(c, ada_ws[i], ada_bs[i])
        h = rmsnorm(x, norms[i]) * (1.0 + scale) + shift
        if i % 2 == 0:
            ck_a, cv_a, ck_b, cv_b = caches[i]
            out = l0_latent(h, ck_a, cv_a, ck_b, cv_b, l0_w_in, l0_q_norm, l0_k_norm, l0_rpb, l0_w_out)
        else:
            c_ckv, c_krope = caches[i]
            out = l1_latent(h, c_ckv, c_krope, l1_w_in, l1_q_a_norm, l1_w_q_b, l1_kv_a_norm, l1_w_kv_b, l1_w_out)
        x = x + gate * out
    y_sample = rmsnorm(x, final_norm)

    return (y_prompt, y_sample, new_l0_a_k, new_l0_a_v, new_l0_b_k, new_l0_b_v, new_l1_ckv, new_l1_krope)
```

```python
import functools

import jax
import jax.numpy as jnp
import numpy as np
from jax import lax
from jax.experimental import pallas as pl
from jax.experimental.pallas import tpu as pltpu

D_MODEL = 2048
GRID_W = 64
HEAD_DIM = 128
A_Q_HEADS = 8
A_KV_HEADS = 2
B_HEADS = 8
A_WIDTH = A_Q_HEADS * HEAD_DIM
A_KV_WIDTH = A_KV_HEADS * HEAD_DIM
B_WIDTH = B_HEADS * HEAD_DIM
L0_WIDTH = A_WIDTH + B_WIDTH
L0_QKV = A_WIDTH + 2 * A_KV_WIDTH + 3 * B_WIDTH
NA_WIN_H = 8
NA_WIN_W = 16
MLA_HEADS = 16
Q_LORA = 512
KV_LORA = 512
QK_NOPE = 128
QK_ROPE = 64
V_DIM = 128
L1_WIDTH = MLA_HEADS * V_DIM
L1_LAT = Q_LORA + KV_LORA + QK_ROPE
ROPE_THETA = 10000.0
EPS = 1e-6
NEG_INF = -1e30

LANES = 128
VMEM_LIMIT = 56 * 1024 * 1024
ADA_ROWS = 16
F32 = jnp.float32
BF16 = jnp.bfloat16


def _params(vmem=VMEM_LIMIT):
    return pltpu.CompilerParams(vmem_limit_bytes=vmem)


def _resident(shape):
    nd = len(shape)
    return pl.BlockSpec(shape, lambda *_: (0,) * nd, pipeline_mode=pl.Buffered(1))


def _dot(a, b):
    return jnp.dot(a, b, preferred_element_type=F32)


def _dot_nt(a, b):
    return lax.dot_general(a, b, (((1,), (1,)), ((), ())), preferred_element_type=F32)


def _rms(x, g):
    return x * lax.rsqrt(jnp.mean(x * x, axis=-1, keepdims=True) + EPS) * g


def _silu(x):
    return x * jax.nn.sigmoid(x)


def _rope(x, cos, sin_lo, sin_hi, half):
    return (x * cos + pltpu.roll(x, half, 1) * sin_hi
            + pltpu.roll(x, LANES - half, 1) * sin_lo)


def _mod_norm(x_ref, g_ref, shift_ref, scale_ref, row):
    x = x_ref[...]
    h = _rms(x, g_ref[...]) * (1.0 + scale_ref[pl.ds(row, 1), :]) + shift_ref[pl.ds(row, 1), :]
    return h.astype(BF16)


def _adaln_kernel(c_ref, w_ref, b_ref, o_ref):
    c = c_ref[...]
    o_ref[...] = _dot(_silu(c).astype(BF16), w_ref[...].astype(BF16)) + b_ref[...]


def _adaln(cs, w, b):
    tn = 1024
    n = w.shape[1]
    return pl.pallas_call(
        _adaln_kernel,
        grid=(n // tn,),
        in_specs=[pl.BlockSpec((ADA_ROWS, D_MODEL), lambda j: (0, 0)),
                  pl.BlockSpec((D_MODEL, tn), lambda j: (0, j)),
                  pl.BlockSpec((1, tn), lambda j: (0, j))],
        out_specs=pl.BlockSpec((ADA_ROWS, tn), lambda j: (0, j)),
        out_shape=jax.ShapeDtypeStruct((ADA_ROWS, n), F32),
        compiler_params=_params(),
        name="adaln",
    )(cs, w, b.reshape(1, n))


def _mod_specs(latent, cols):
    rb = 0 if latent else 1
    return [pl.BlockSpec((8, D_MODEL), lambda i, c=c: (rb, c)) for c in cols]


def _in0_kernel(*refs, latent, tm, seq):
    if latent:
        (x_ref, shift_ref, scale_ref, g_ref, w_ref, qn_ref, kn_ref, cos_ref, slo_ref, shi_ref,
         qa_ref, ka_ref, va_ref, qb_ref, kb_ref, vb_ref) = refs
    else:
        (x_ref, shift_ref, scale_ref, g_ref, w_ref, qn_ref, kn_ref,
         qa_ref, ka_ref, va_ref, qb_ref, kb_ref, vb_ref,
         kaf_ref, vaf_ref, kbf_ref, vbf_ref) = refs
    row = (pl.program_id(0) * tm) // seq if latent else 0
    h = _mod_norm(x_ref, g_ref, shift_ref, scale_ref, row)

    def head_norm(y, g_ref_):
        y = _rms(y, g_ref_[...])
        if latent:
            y = _rope(y, cos_ref[...], slo_ref[...], shi_ref[...], HEAD_DIM // 4)
        return y

    c0 = 0
    y = _dot(h, w_ref[:, c0:c0 + A_WIDTH])
    for hd in range(A_Q_HEADS):
        sl = slice(hd * HEAD_DIM, (hd + 1) * HEAD_DIM)
        qa_ref[:, sl] = head_norm(y[:, sl], qn_ref).astype(BF16)
    c0 += A_WIDTH
    y = _dot(h, w_ref[:, c0:c0 + A_KV_WIDTH])
    for hd in range(A_KV_HEADS):
        sl = slice(hd * HEAD_DIM, (hd + 1) * HEAD_DIM)
        k = head_norm(y[:, sl], kn_ref)
        ka_ref[:, sl] = k.astype(BF16)
        if not latent:
            kaf_ref[:, sl] = k
    c0 += A_KV_WIDTH
    y = _dot(h, w_ref[:, c0:c0 + A_KV_WIDTH])
    va_ref[...] = y.astype(BF16)
    if not latent:
        vaf_ref[...] = y
    c0 += A_KV_WIDTH
    y = _dot(h, w_ref[:, c0:c0 + B_WIDTH])
    qb_ref[...] = y.astype(BF16)
    c0 += B_WIDTH
    y = _dot(h, w_ref[:, c0:c0 + B_WIDTH])
    kb_ref[...] = y.astype(BF16)
    if not latent:
        kbf_ref[...] = y
    c0 += B_WIDTH
    y = _dot(h, w_ref[:, c0:c0 + B_WIDTH])
    vb_ref[...] = y.astype(BF16)
    if not latent:
        vbf_ref[...] = y


def _in0(x, mod, g, w, qn, kn, rope_tabs, *, latent, seq, tm=512):
    m = x.shape[0]
    row_spec = lambda n: pl.BlockSpec((tm, n), lambda i: (i, 0))
    in_specs = [row_spec(D_MODEL)] + _mod_specs(latent, (0, 1)) + [
        _resident((1, D_MODEL)), _resident((D_MODEL, L0_QKV)),
        _resident((1, HEAD_DIM)), _resident((1, HEAD_DIM))]
    args = [x, mod, mod, g, w, qn, kn]
    if latent:
        nblk = seq // tm
        in_specs += [pl.BlockSpec((tm, LANES), lambda i: (i % nblk, 0))] * 3
        args += list(rope_tabs)
    widths = [A_WIDTH, A_KV_WIDTH, A_KV_WIDTH, B_WIDTH, B_WIDTH, B_WIDTH]
    out_specs = [row_spec(n) for n in widths]
    out_shape = [jax.ShapeDtypeStruct((m, n), BF16) for n in widths]
    if not latent:
        fw = [A_KV_WIDTH, A_KV_WIDTH, B_WIDTH, B_WIDTH]
        out_specs += [row_spec(n) for n in fw]
        out_shape += [jax.ShapeDtypeStruct((m, n), F32) for n in fw]
    return pl.pallas_call(
        functools.partial(_in0_kernel, latent=latent, tm=tm, seq=seq),
        grid=(m // tm,), in_specs=in_specs, out_specs=out_specs, out_shape=out_shape,
        compiler_params=_params(), name="in0_lat" if latent else "in0_ctx",
    )(*args)


def _attn_kernel(*refs, n_src, heads, group, scale):
    q_ref = refs[0]
    kv_refs = refs[1:1 + 2 * n_src]
    o_ref = refs[1 + 2 * n_src]
    for j in range(heads // group):
        sl_kv = slice(j * HEAD_DIM, (j + 1) * HEAD_DIM)
        ks = [kv_refs[2 * s][:, sl_kv].astype(BF16) for s in range(n_src)]
        vs = [kv_refs[2 * s + 1][:, sl_kv].astype(BF16) for s in range(n_src)]
        for gi in range(group):
            hd = j * group + gi
            sl = slice(hd * HEAD_DIM, (hd + 1) * HEAD_DIM)
            q = q_ref[:, sl]
            ss = [_dot_nt(q, k) * scale for k in ks]
            mx = ss[0].max(axis=-1, keepdims=True)
            for s in ss[1:]:
                mx = jnp.maximum(mx, s.max(axis=-1, keepdims=True))
            ps = [jnp.exp(s - mx) for s in ss]
            den = ps[0].sum(axis=-1, keepdims=True)
            for p in ps[1:]:
                den = den + p.sum(axis=-1, keepdims=True)
            acc = _dot(ps[0].astype(BF16), vs[0])
            for p, v in zip(ps[1:], vs[1:]):
                acc = acc + _dot(p.astype(BF16), v)
            o_ref[:, sl] = (acc / den).astype(BF16)


def _attn(q, srcs, *, batch, seq, heads, group, tq, name):
    kvw = (heads // group) * HEAD_DIM
    nq = seq // tq
    in_specs = [pl.BlockSpec((tq, heads * HEAD_DIM), lambda b, i: (b * nq + i, 0))]
    args = [q]
    for k, v, t in srcs:
        for a in (k, v):
            if a.ndim == 3:
                in_specs.append(pl.BlockSpec((None, t, kvw), lambda b, i: (b, 0, 0)))
            else:
                in_specs.append(pl.BlockSpec((t, kvw), lambda b, i: (b, 0)))
            args.append(a)
    return pl.pallas_call(
        functools.partial(_attn_kernel, n_src=len(srcs), heads=heads, group=group,
                          scale=HEAD_DIM ** -0.5),
        grid=(batch, nq), in_specs=in_specs,
        out_specs=pl.BlockSpec((tq, heads * HEAD_DIM), lambda b, i: (b * nq + i, 0)),
        out_shape=jax.ShapeDtypeStruct((batch * seq, heads * HEAD_DIM), BF16),
        compiler_params=_params(), name=name,
    )(*args)


def _na_geometry(rows):
    wh = min(NA_WIN_H, rows)
    row_start = np.clip(np.arange(rows) - wh // 2, 0, rows - wh)
    return wh, row_start


def _na_bias_kernel(rpb_ref, o_ref, *, rows):
    hd = pl.program_id(0)
    wh, row_start = _na_geometry(rows)
    nx = 2 * NA_WIN_W - 1
    qc = lax.broadcasted_iota(jnp.int32, (GRID_W, GRID_W), 0)
    kc = lax.broadcasted_iota(jnp.int32, (GRID_W, GRID_W), 1)
    col_start = jnp.clip(qc - NA_WIN_W // 2, 0, GRID_W - NA_WIN_W)
    col_ok = (kc >= col_start) & (kc < col_start + NA_WIN_W)
    dx = kc - qc + NA_WIN_W - 1
    base = hd * ((2 * NA_WIN_H - 1) * nx)
    tiles = {}
    for dy in range(-(wh - 1), wh):
        t = jnp.zeros((GRID_W, GRID_W), F32)
        for j in range(nx):
            t = jnp.where(dx == j, rpb_ref[base + (dy + NA_WIN_H - 1) * nx + j], t)
        tiles[dy] = jnp.where(col_ok, t, NEG_INF)
    o_ref[...] = jnp.full(o_ref.shape, NEG_INF, F32)
    for qr in range(rows):
        for kr in range(int(row_start[qr]), int(row_start[qr]) + wh):
            o_ref[0, qr * GRID_W:(qr + 1) * GRID_W, kr * GRID_W:(kr + 1) * GRID_W] = tiles[kr - qr]


def _na_bias(rpb, seq):
    rows = seq // GRID_W
    return pl.pallas_call(
        functools.partial(_na_bias_kernel, rows=rows),
        grid=(B_HEADS,),
        in_specs=[pl.BlockSpec(memory_space=pltpu.SMEM)],
        out_specs=pl.BlockSpec((1, seq, seq), lambda h: (h, 0, 0)),
        out_shape=jax.ShapeDtypeStruct((B_HEADS, seq, seq), F32),
        compiler_params=_params(), name="na_bias",
    )(rpb.reshape(-1))


def _na_windows(rows, tq):
    wh, row_start = _na_geometry(rows)
    rq = tq // GRID_W
    wins = []
    for qb in range(rows // rq):
        lo = int(row_start[qb * rq]) * GRID_W
        hi = (int(row_start[qb * rq + rq - 1]) + wh) * GRID_W
        wins.append((lo // LANES * LANES, -(-hi // LANES) * LANES))
    return wins


def _na_kernel(q_ref, k_ref, v_ref, kc_ref, vc_ref, bias_ref, o_ref, *, tq, wins, scale):
    kc = kc_ref[...].astype(BF16)
    vc = vc_ref[...].astype(BF16)
    for qb, (lo, hi) in enumerate(wins):
        rs = slice(qb * tq, (qb + 1) * tq)
        q = q_ref[rs, :]
        s_loc = _dot_nt(q, k_ref[lo:hi, :]) * scale + bias_ref[0, rs, lo:hi]
        s_ctx = _dot_nt(q, kc) * scale
        mx = jnp.maximum(s_loc.max(axis=-1, keepdims=True), s_ctx.max(axis=-1, keepdims=True))
        p_loc = jnp.exp(s_loc - mx)
        p_ctx = jnp.exp(s_ctx - mx)
        den = p_loc.sum(axis=-1, keepdims=True) + p_ctx.sum(axis=-1, keepdims=True)
        acc = _dot(p_loc.astype(BF16), v_ref[lo:hi, :]) + _dot(p_ctx.astype(BF16), vc)
        o_ref[rs, :] = (acc / den).astype(BF16)


def _na(q, k, v, kc, vc, bias, *, batch, seq, tq=256):
    past = kc.shape[1]
    wins = _na_windows(seq // GRID_W, tq)
    tok = pl.BlockSpec((seq, HEAD_DIM), lambda h, b: (b, h))
    cache = pl.BlockSpec((None, past, HEAD_DIM), lambda h, b: (b, 0, h))
    return pl.pallas_call(
        functools.partial(_na_kernel, tq=tq, wins=wins, scale=HEAD_DIM ** -0.5),
        grid=(B_HEADS, batch),
        in_specs=[tok, tok, tok, cache, cache,
                  pl.BlockSpec((1, seq, seq), lambda h, b: (h, 0, 0))],
        out_specs=tok,
        out_shape=jax.ShapeDtypeStruct((batch * seq, B_WIDTH), BF16),
        compiler_params=_params(), name="na",
    )(q, k, v, kc, vc, bias)


def _out_kernel(*refs, latent, tm, seq, n_o, final):
    x_ref, shift_ref, scale_ref, gate_ref, g_ref = refs[:5]
    o_refs = refs[5:5 + n_o]
    wg_ref, wo_ref = refs[5 + n_o:7 + n_o]
    fn_ref = refs[7 + n_o] if final else None
    y_ref = refs[-1]
    row = (pl.program_id(0) * tm) // seq if latent else 0
    h = _mod_norm(x_ref, g_ref, shift_ref, scale_ref, row)
    gt = _dot(h, wg_ref[...])
    o = jnp.concatenate([r[...] for r in o_refs], axis=-1).astype(F32)
    t = (o * _silu(gt)).astype(BF16)
    y = x_ref[...] + gate_ref[pl.ds(row, 1), :] * _dot(t, wo_ref[...])
    if final:
        y = _rms(y, fn_ref[...])
    y_ref[...] = y


def _out(x, mod, g, os_, wg, wo, fn, *, latent, seq, name, tm=512):
    m = x.shape[0]
    width = wg.shape[1]
    final = fn is not None
    in_specs = [pl.BlockSpec((tm, D_MODEL), lambda i: (i, 0))] + _mod_specs(latent, (0, 1, 2))
    in_specs += [_resident((1, D_MODEL))]
    in_specs += [pl.BlockSpec((tm, o.shape[1]), lambda i: (i, 0)) for o in os_]
    in_specs += [_resident((D_MODEL, width)), _resident((width, D_MODEL))]
    args = [x, mod, mod, mod, g, *os_, wg, wo]
    if final:
        in_specs.append(_resident((1, D_MODEL)))
        args.append(fn)
    return pl.pallas_call(
        functools.partial(_out_kernel, latent=latent, tm=tm, seq=seq, n_o=len(os_), final=final),
        grid=(m // tm,), in_specs=in_specs,
        out_specs=pl.BlockSpec((tm, D_MODEL), lambda i: (i, 0)),
        out_shape=jax.ShapeDtypeStruct((m, D_MODEL), F32),
        compiler_params=_params(), name=name,
    )(*args)


L1_LAT_PAD = L1_LAT + (LANES - QK_ROPE)


def _in1_kernel(*refs, latent, tm, seq):
    if latent:
        (x_ref, shift_ref, scale_ref, g_ref, w_ref, qan_ref, kvn_ref, wqn_ref, wqr_ref,
         cos_ref, slo_ref, shi_ref, qn_ref, qr_ref, ckv_ref, kr_ref) = refs
    else:
        (x_ref, shift_ref, scale_ref, g_ref, w_ref, qan_ref, kvn_ref, wqn_ref, wqr_ref,
         qn_ref, qr_ref, ckv_ref, kr_ref) = refs
    row = (pl.program_id(0) * tm) // seq if latent else 0
    h = _mod_norm(x_ref, g_ref, shift_ref, scale_ref, row)
    z = _dot(h, w_ref[...])
    q_lat = _rms(z[:, :Q_LORA], qan_ref[...]).astype(BF16)
    qn_ref[...] = _dot(q_lat, wqn_ref[...]).astype(BF16)
    qr = _dot(q_lat, wqr_ref[...])
    kr = z[:, Q_LORA + KV_LORA:]
    if latent:
        cos, slo, shi = cos_ref[...], slo_ref[...], shi_ref[...]
        for j in range(MLA_HEADS * QK_ROPE // LANES):
            sl = slice(j * LANES, (j + 1) * LANES)
            qr_ref[:, sl] = _rope(qr[:, sl], cos, slo, shi, QK_ROPE // 4).astype(BF16)
        kr = _rope(kr, cos, slo, shi, QK_ROPE // 4)
    else:
        qr_ref[...] = qr.astype(BF16)
    ckv_ref[...] = _rms(z[:, Q_LORA:Q_LORA + KV_LORA], kvn_ref[...]).astype(ckv_ref.dtype)
    kr_ref[...] = kr[:, :QK_ROPE].astype(kr_ref.dtype)


def _in1(x, mod, g, w, qan, kvn, wqn, wqr, rope_tabs, *, latent, seq, tm=512):
    m = x.shape[0]
    row_spec = lambda n: pl.BlockSpec((tm, n), lambda i: (i, 0))
    in_specs = [row_spec(D_MODEL)] + _mod_specs(latent, (0, 1)) + [
        _resident((1, D_MODEL)), _resident((D_MODEL, L1_LAT_PAD)),
        _resident((1, Q_LORA)), _resident((1, KV_LORA)),
        _resident((Q_LORA, MLA_HEADS * QK_NOPE)), _resident((Q_LORA, MLA_HEADS * QK_ROPE))]
    args = [x, mod, mod, g, w, qan, kvn, wqn, wqr]
    if latent:
        nblk = seq // tm
        in_specs += [pl.BlockSpec((tm, LANES), lambda i: (i % nblk, 0))] * 3
        args += list(rope_tabs)
    lat_dt = BF16 if latent else F32
    widths = [(MLA_HEADS * QK_NOPE, BF16), (MLA_HEADS * QK_ROPE, BF16), (KV_LORA, lat_dt),
              (QK_ROPE, lat_dt)]
    return pl.pallas_call(
        functools.partial(_in1_kernel, latent=latent, tm=tm, seq=seq),
        grid=(m // tm,), in_specs=in_specs,
        out_specs=[row_spec(n) for n, _ in widths],
        out_shape=[jax.ShapeDtypeStruct((m, n), dt) for n, dt in widths],
        compiler_params=_params(), name="in1_lat" if latent else "in1_ctx",
    )(*args)


def _mla_kernel(*refs, n_src, hb, tq, scale):
    qn_ref, qr_ref = refs[:2]
    src = refs[2:2 + 2 * n_src]
    w_ref = refs[2 + 2 * n_src]
    o_ref = refs[3 + 2 * n_src]
    ckv = [src[2 * s][...].astype(BF16) for s in range(n_src)]
    kr = [src[2 * s + 1][...].astype(BF16) for s in range(n_src)]
    seq = qn_ref.shape[0]
    kvw = QK_NOPE + V_DIM
    for hd in range(hb):
        w = w_ref[:, hd * kvw:(hd + 1) * kvw]
        kv = [_dot(c, w).astype(BF16) for c in ckv]
        sl = slice(hd * QK_NOPE, (hd + 1) * QK_NOPE)
        slr = slice(hd * QK_ROPE, (hd + 1) * QK_ROPE)
        for qb in range(seq // tq):
            rs = slice(qb * tq, (qb + 1) * tq)
            qn = qn_ref[rs, sl]
            qr = qr_ref[rs, slr]
            ss = [(_dot_nt(qn, kv_s[:, :QK_NOPE]) + _dot_nt(qr, kr_s)) * scale
                  for kv_s, kr_s in zip(kv, kr)]
            mx = ss[0].max(axis=-1, keepdims=True)
            for s in ss[1:]:
                mx = jnp.maximum(mx, s.max(axis=-1, keepdims=True))
            ps = [jnp.exp(s - mx) for s in ss]
            den = ps[0].sum(axis=-1, keepdims=True)
            for p in ps[1:]:
                den = den + p.sum(axis=-1, keepdims=True)
            acc = _dot(ps[0].astype(BF16), kv[0][:, QK_NOPE:])
            for p, kv_s in zip(ps[1:], kv[1:]):
                acc = acc + _dot(p.astype(BF16), kv_s[:, QK_NOPE:])
            o_ref[rs, sl] = (acc / den).astype(BF16)


def _mla(qn, qr, srcs, w_kv, *, batch, seq, hb, tq, name):
    kvw = QK_NOPE + V_DIM
    in_specs = [pl.BlockSpec((seq, hb * QK_NOPE), lambda b, j: (b, j)),
                pl.BlockSpec((seq, hb * QK_ROPE), lambda b, j: (b, j))]
    args = [qn, qr]
    for ckv, kr, t in srcs:
        for a in (ckv, kr):
            wd = a.shape[-1]
            if a.ndim == 3:
                in_specs.append(pl.BlockSpec((None, t, wd), lambda b, j: (b, 0, 0)))
            else:
                in_specs.append(pl.BlockSpec((t, wd), lambda b, j: (b, 0)))
            args.append(a)
    in_specs.append(pl.BlockSpec((KV_LORA, hb * kvw), lambda b, j: (0, j)))
    args.append(w_kv)
    return pl.pallas_call(
        functools.partial(_mla_kernel, n_src=len(srcs), hb=hb, tq=tq,
                          scale=(QK_NOPE + QK_ROPE) ** -0.5),
        grid=(batch, MLA_HEADS // hb), in_specs=in_specs,
        out_specs=pl.BlockSpec((seq, hb * V_DIM), lambda b, j: (b, j)),
        out_shape=jax.ShapeDtypeStruct((batch * seq, L1_WIDTH), BF16),
        compiler_params=_params(), name=name,
    )(*args)


def _rope_tables(seq, d):
    a = d // 2
    t = jnp.arange(seq)
    row = (t // GRID_W).astype(F32)
    col = (t % GRID_W).astype(F32)
    inv = 1.0 / (ROPE_THETA ** (jnp.arange(a // 2, dtype=F32) * 2.0 / a))
    ang = jnp.concatenate([row[:, None] * inv] * 2 + [col[:, None] * inv] * 2, axis=-1)
    ang = jnp.tile(ang, (1, LANES // d))
    lower = jnp.tile((jnp.arange(d) % a) < a // 2, LANES // d)[None, :]
    cos, sin = jnp.cos(ang), jnp.sin(ang)
    zero = jnp.zeros_like(sin)
    return cos, jnp.where(lower, -sin, zero), jnp.where(lower, zero, sin)


def kernel(x_prompt, x_sample, cache_l0_a_k, cache_l0_a_v, cache_l0_b_k, cache_l0_b_v, cache_l1_ckv, cache_l1_krope, c, c_ctx, l0_norm, l0_ada_w, l0_ada_b, l0_w_in, l0_q_norm, l0_k_norm, l0_rpb, l0_w_out, l1_norm, l1_ada_w, l1_ada_b, l1_w_in, l1_q_a_norm, l1_w_q_b, l1_kv_a_norm, l1_w_kv_b, l1_w_out, final_norm):
    batch, seq_c, _ = x_prompt.shape
    dec_batch, seq_l, _ = x_sample.shape
    past = cache_l0_a_k.shape[1]

    w0_qkv = l0_w_in[:, :L0_QKV].astype(BF16)
    w0_gate = l0_w_in[:, L0_QKV:].astype(BF16)
    w0_out = l0_w_out.astype(BF16)
    w1_lat = jnp.pad(l1_w_in[:, :L1_LAT], ((0, 0), (0, L1_LAT_PAD - L1_LAT))).astype(BF16)
    w1_gate = l1_w_in[:, L1_LAT:].astype(BF16)
    w1_out = l1_w_out.astype(BF16)
    wq = l1_w_q_b.reshape(Q_LORA, MLA_HEADS, QK_NOPE + QK_ROPE)
    wq_nope = wq[:, :, :QK_NOPE].reshape(Q_LORA, MLA_HEADS * QK_NOPE).astype(BF16)
    wq_rope = wq[:, :, QK_NOPE:].reshape(Q_LORA, MLA_HEADS * QK_ROPE).astype(BF16)
    w_kv = l1_w_kv_b.astype(BF16)

    row2 = lambda v: v.reshape(1, -1)
    cs = jnp.zeros((ADA_ROWS, D_MODEL), F32).at[:dec_batch].set(c).at[8].set(c_ctx)
    mod0 = _adaln(cs, l0_ada_w, l0_ada_b)
    mod1 = _adaln(cs, l1_ada_w, l1_ada_b)

    rope_a = _rope_tables(seq_l, HEAD_DIM)
    rope_c = _rope_tables(seq_l, QK_ROPE)

    xc = x_prompt.reshape(batch * seq_c, D_MODEL)
    qa, ka, va, qb, kb, vb, ka_f, va_f, kb_f, vb_f = _in0(
        xc, mod0, row2(l0_norm), w0_qkv, row2(l0_q_norm), row2(l0_k_norm), None,
        latent=False, seq=seq_c)
    oa = _attn(qa, [(ka, va, seq_c)], batch=batch, seq=seq_c, heads=A_Q_HEADS,
               group=A_Q_HEADS // A_KV_HEADS, tq=seq_c, name="attn_a_ctx")
    ob = _attn(qb, [(kb, vb, seq_c)], batch=batch, seq=seq_c, heads=B_HEADS, group=1,
               tq=seq_c, name="attn_b_ctx")
    xc = _out(xc, mod0, row2(l0_norm), [oa, ob], w0_gate, w0_out, None,
              latent=False, seq=seq_c, name="out0_ctx")
    qn, qr, ckv_f, kr_f = _in1(xc, mod1, row2(l1_norm), w1_lat, row2(l1_q_a_norm),
                               row2(l1_kv_a_norm), wq_nope, wq_rope, None,
                               latent=False, seq=seq_c)
    oc = _mla(qn, qr, [(ckv_f, kr_f, seq_c)], w_kv, batch=batch, seq=seq_c,
              hb=MLA_HEADS, tq=seq_c, name="mla_ctx")
    y_prompt = _out(xc, mod1, row2(l1_norm), [oc], w1_gate, w1_out, row2(final_norm),
                    latent=False, seq=seq_c, name="out1_ctx")

    xl = x_sample.reshape(dec_batch * seq_l, D_MODEL)
    qa, ka, va, qb, kb, vb = _in0(
        xl, mod0, row2(l0_norm), w0_qkv, row2(l0_q_norm), row2(l0_k_norm), rope_a,
        latent=True, seq=seq_l)
    oa = _attn(qa, [(cache_l0_a_k.reshape(dec_batch, past, A_KV_WIDTH),
                     cache_l0_a_v.reshape(dec_batch, past, A_KV_WIDTH), past),
                    (ka, va, seq_l)],
               batch=dec_batch, seq=seq_l, heads=A_Q_HEADS, group=A_Q_HEADS // A_KV_HEADS,
               tq=256, name="attn_a_lat")
    bias = _na_bias(l0_rpb, seq_l)
    ob = _na(qb, kb, vb, cache_l0_b_k.reshape(dec_batch, past, B_WIDTH),
             cache_l0_b_v.reshape(dec_batch, past, B_WIDTH), bias, batch=dec_batch, seq=seq_l)
    xl = _out(xl, mod0, row2(l0_norm), [oa, ob], w0_gate, w0_out, None,
              latent=True, seq=seq_l, name="out0_lat")
    qn, qr, ckv, kr = _in1(xl, mod1, row2(l1_norm), w1_lat, row2(l1_q_a_norm),
                           row2(l1_kv_a_norm), wq_nope, wq_rope, rope_c,
                           latent=True, seq=seq_l)
    oc = _mla(qn, qr, [(cache_l1_ckv, cache_l1_krope, past), (ckv, kr, seq_l)], w_kv,
              batch=dec_batch, seq=seq_l, hb=2, tq=256, name="mla_lat")
    y_sample = _out(xl, mod1, row2(l1_norm), [oc], w1_gate, w1_out, row2(final_norm),
                    latent=True, seq=seq_l, name="out1_lat")

    return (y_prompt.reshape(batch, seq_c, D_MODEL),
            y_sample.reshape(dec_batch, seq_l, D_MODEL),
            ka_f.reshape(batch, seq_c, A_KV_HEADS, HEAD_DIM),
            va_f.reshape(batch, seq_c, A_KV_HEADS, HEAD_DIM),
            kb_f.reshape(batch, seq_c, B_HEADS, HEAD_DIM),
            vb_f.reshape(batch, seq_c, B_HEADS, HEAD_DIM),
            ckv_f.reshape(batch, seq_c, KV_LORA),
            kr_f.reshape(batch, seq_c, QK_ROPE))
```

```python
import functools

import jax
import jax.numpy as jnp
import numpy as np
from jax import lax
from jax.experimental import pallas as pl
from jax.experimental.pallas import tpu as pltpu

LANES = 128
VMEM_LIMIT = 56 * 1024 * 1024
ADA_ROWS = 16
F32 = jnp.float32
BF16 = jnp.bfloat16

D_MODEL = 2048
GRID_W = 64
HEAD_DIM = 128
A_Q_HEADS = 8
A_KV_HEADS = 2
B_HEADS = 8
A_WIDTH = A_Q_HEADS * HEAD_DIM
A_KV_WIDTH = A_KV_HEADS * HEAD_DIM
B_WIDTH = B_HEADS * HEAD_DIM
L0_WIDTH = A_WIDTH + B_WIDTH
L0_QKV = A_WIDTH + 2 * A_KV_WIDTH + 3 * B_WIDTH
NA_WIN_H = 8
NA_WIN_W = 16
MLA_HEADS = 16
Q_LORA = 512
KV_LORA = 512
QK_NOPE = 128
QK_ROPE = 64
V_DIM = 128
L1_WIDTH = MLA_HEADS * V_DIM
L1_LAT = Q_LORA + KV_LORA + QK_ROPE
L1_LAT_PAD = L1_LAT + (LANES - QK_ROPE)
MLA_QK = 2 * LANES
ROPE_THETA = 10000.0
EPS = 1e-6
NEG_INF = -1e30
LOG2E = 1.4426950408889634
Q_SCALE_L0 = HEAD_DIM ** -0.5 * LOG2E
Q_SCALE_L1 = (QK_NOPE + QK_ROPE) ** -0.5 * LOG2E


def _params(vmem=VMEM_LIMIT):
    return pltpu.CompilerParams(vmem_limit_bytes=vmem)


def _resident(shape):
    nd = len(shape)
    return pl.BlockSpec(shape, lambda *_: (0,) * nd, pipeline_mode=pl.Buffered(1))


def _dot(a, b):
    return jnp.dot(a, b, preferred_element_type=F32)


def _dot_nt(a, b):
    return lax.dot_general(a, b, (((1,), (1,)), ((), ())), preferred_element_type=F32)


def _rms(x, g):
    return x * lax.rsqrt(jnp.mean(x * x, axis=-1, keepdims=True) + EPS) * g


def _silu(x):
    return x * jax.nn.sigmoid(x)


def _rope(x, cos, sin_lo, sin_hi, half):
    return (x * cos + pltpu.roll(x, half, 1) * sin_hi
            + pltpu.roll(x, LANES - half, 1) * sin_lo)


def _mod_norm(x_ref, g_ref, shift_ref, scale_ref, row):
    x = x_ref[...]
    h = _rms(x, g_ref[...]) * (1.0 + scale_ref[pl.ds(row, 1), :]) + shift_ref[pl.ds(row, 1), :]
    return h.astype(BF16)


def _softmax_pv(ss, vs):
    mx = ss[0].max(axis=-1, keepdims=True)
    for s in ss[1:]:
        mx = jnp.maximum(mx, s.max(axis=-1, keepdims=True))
    ps = [jnp.exp2(s - mx) for s in ss]
    den = ps[0].sum(axis=-1, keepdims=True)
    for p in ps[1:]:
        den = den + p.sum(axis=-1, keepdims=True)
    acc = _dot(ps[0].astype(BF16), vs[0])
    for p, v in zip(ps[1:], vs[1:]):
        acc = acc + _dot(p.astype(BF16), v)
    return acc / den


def _cast_kernel(w_ref, o_ref):
    o_ref[...] = w_ref[...].astype(BF16)


def _cast_cols(w, col0, ncols, tn=512):
    rows = w.shape[0]
    assert col0 % tn == 0 and ncols % tn == 0
    off = col0 // tn
    return pl.pallas_call(
        _cast_kernel, grid=(ncols // tn,),
        in_specs=[pl.BlockSpec((rows, tn), lambda j: (0, j + off))],
        out_specs=pl.BlockSpec((rows, tn), lambda j: (0, j)),
        out_shape=jax.ShapeDtypeStruct((rows, ncols), BF16),
        compiler_params=_params(), name="cast",
    )(w)


def _adaln_kernel(c_ref, w_ref, b_ref, o_ref):
    c = c_ref[...]
    o_ref[...] = _dot(_silu(c).astype(BF16), w_ref[...].astype(BF16)) + b_ref[...]


def _adaln(cs, w, b):
    tn = 1024
    n = w.shape[1]
    return pl.pallas_call(
        _adaln_kernel,
        grid=(n // tn,),
        in_specs=[pl.BlockSpec((ADA_ROWS, D_MODEL), lambda j: (0, 0)),
                  pl.BlockSpec((D_MODEL, tn), lambda j: (0, j)),
                  pl.BlockSpec((1, tn), lambda j: (0, j))],
        out_specs=pl.BlockSpec((ADA_ROWS, tn), lambda j: (0, j)),
        out_shape=jax.ShapeDtypeStruct((ADA_ROWS, n), F32),
        compiler_params=_params(),
        name="adaln",
    )(cs, w, b.reshape(1, n))


def _mod_specs(latent, cols):
    rb = 0 if latent else 1
    return [pl.BlockSpec((8, D_MODEL), lambda i, c=c: (rb, c)) for c in cols]


def _in0_kernel(*refs, latent, tm, seq):
    if latent:
        (x_ref, shift_ref, scale_ref, g_ref, w_ref, qn_ref, kn_ref, cos_ref, slo_ref, shi_ref,
         qa_ref, ka_ref, va_ref, qb_ref, kb_ref, vb_ref) = refs
    else:
        (x_ref, shift_ref, scale_ref, g_ref, w_ref, qn_ref, kn_ref,
         qa_ref, ka_ref, va_ref, qb_ref, kb_ref, vb_ref,
         kaf_ref, vaf_ref, kbf_ref, vbf_ref) = refs
    row = (pl.program_id(0) * tm) // seq if latent else 0
    h = _mod_norm(x_ref, g_ref, shift_ref, scale_ref, row)

    def head_norm(y, g_ref_):
        y = _rms(y, g_ref_[...])
        if latent:
            y = _rope(y, cos_ref[...], slo_ref[...], shi_ref[...], HEAD_DIM // 4)
        return y

    c0 = 0
    y = _dot(h, w_ref[:, c0:c0 + A_WIDTH])
    for hd in range(A_Q_HEADS):
        sl = slice(hd * HEAD_DIM, (hd + 1) * HEAD_DIM)
        qa_ref[:, sl] = (head_norm(y[:, sl], qn_ref) * Q_SCALE_L0).astype(BF16)
    c0 += A_WIDTH
    y = _dot(h, w_ref[:, c0:c0 + A_KV_WIDTH])
    for hd in range(A_KV_HEADS):
        sl = slice(hd * HEAD_DIM, (hd + 1) * HEAD_DIM)
        k = head_norm(y[:, sl], kn_ref)
        ka_ref[:, sl] = k.astype(BF16)
        if not latent:
            kaf_ref[:, sl] = k
    c0 += A_KV_WIDTH
    y = _dot(h, w_ref[:, c0:c0 + A_KV_WIDTH])
    va_ref[...] = y.astype(BF16)
    if not latent:
        vaf_ref[...] = y
    c0 += A_KV_WIDTH
    y = _dot(h, w_ref[:, c0:c0 + B_WIDTH])
    qb_ref[...] = (y * Q_SCALE_L0).astype(BF16)
    c0 += B_WIDTH
    y = _dot(h, w_ref[:, c0:c0 + B_WIDTH])
    kb_ref[...] = y.astype(BF16)
    if not latent:
        kbf_ref[...] = y
    c0 += B_WIDTH
    y = _dot(h, w_ref[:, c0:c0 + B_WIDTH])
    vb_ref[...] = y.astype(BF16)
    if not latent:
        vbf_ref[...] = y


def _in0(x, mod, g, w, qn, kn, rope_tabs, *, latent, seq, tm=512):
    m = x.shape[0]
    row_spec = lambda n: pl.BlockSpec((tm, n), lambda i: (i, 0))
    in_specs = [row_spec(D_MODEL)] + _mod_specs(latent, (0, 1)) + [
        _resident((1, D_MODEL)), _resident((D_MODEL, L0_QKV)),
        _resident((1, HEAD_DIM)), _resident((1, HEAD_DIM))]
    args = [x, mod, mod, g, w, qn, kn]
    if latent:
        nblk = seq // tm
        in_specs += [pl.BlockSpec((tm, LANES), lambda i: (i % nblk, 0))] * 3
        args += list(rope_tabs)
    widths = [A_WIDTH, A_KV_WIDTH, A_KV_WIDTH, B_WIDTH, B_WIDTH, B_WIDTH]
    out_specs = [row_spec(n) for n in widths]
    out_shape = [jax.ShapeDtypeStruct((m, n), BF16) for n in widths]
    if not latent:
        fw = [A_KV_WIDTH, A_KV_WIDTH, B_WIDTH, B_WIDTH]
        out_specs += [row_spec(n) for n in fw]
        out_shape += [jax.ShapeDtypeStruct((m, n), F32) for n in fw]
    return pl.pallas_call(
        functools.partial(_in0_kernel, latent=latent, tm=tm, seq=seq),
        grid=(m // tm,), in_specs=in_specs, out_specs=out_specs, out_shape=out_shape,
        compiler_params=_params(), name="in0_lat" if latent else "in0_ctx",
    )(*args)


def _attn_kernel(*refs, n_src, heads, group):
    q_ref = refs[0]
    kv_refs = refs[1:1 + 2 * n_src]
    o_ref = refs[1 + 2 * n_src]
    tq = q_ref.shape[0]
    for j in range(heads // group):
        sl_kv = slice(j * HEAD_DIM, (j + 1) * HEAD_DIM)
        ks = [kv_refs[2 * s][:, sl_kv].astype(BF16) for s in range(n_src)]
        vs = [kv_refs[2 * s + 1][:, sl_kv].astype(BF16) for s in range(n_src)]
        sls = [slice((j * group + gi) * HEAD_DIM, (j * group + gi + 1) * HEAD_DIM)
               for gi in range(group)]
        q = jnp.concatenate([q_ref[:, sl] for sl in sls], axis=0)
        o = _softmax_pv([_dot_nt(q, k) for k in ks], vs).astype(BF16)
        for gi, sl in enumerate(sls):
            o_ref[:, sl] = o[gi * tq:(gi + 1) * tq]


def _attn(q, srcs, *, batch, seq, heads, group, tq, name):
    kvw = (heads // group) * HEAD_DIM
    nq = seq // tq
    in_specs = [pl.BlockSpec((tq, heads * HEAD_DIM), lambda b, i: (b * nq + i, 0))]
    args = [q]
    for k, v, t in srcs:
        for a in (k, v):
            if a.ndim == 3:
                in_specs.append(pl.BlockSpec((None, t, kvw), lambda b, i: (b, 0, 0)))
            else:
                in_specs.append(pl.BlockSpec((t, kvw), lambda b, i: (b, 0)))
            args.append(a)
    return pl.pallas_call(
        functools.partial(_attn_kernel, n_src=len(srcs), heads=heads, group=group),
        grid=(batch, nq), in_specs=in_specs,
        out_specs=pl.BlockSpec((tq, heads * HEAD_DIM), lambda b, i: (b * nq + i, 0)),
        out_shape=jax.ShapeDtypeStruct((batch * seq, heads * HEAD_DIM), BF16),
        compiler_params=_params(), name=name,
    )(*args)


def _na_geometry(rows):
    wh = min(NA_WIN_H, rows)
    row_start = np.clip(np.arange(rows) - wh // 2, 0, rows - wh)
    return wh, row_start


def _na_bias_kernel(rpb_ref, o_ref, *, rows):
    hd = pl.program_id(0)
    wh, row_start = _na_geometry(rows)
    nx = 2 * NA_WIN_W - 1
    qc = lax.broadcasted_iota(jnp.int32, (GRID_W, GRID_W), 0)
    kc = lax.broadcasted_iota(jnp.int32, (GRID_W, GRID_W), 1)
    col_start = jnp.clip(qc - NA_WIN_W // 2, 0, GRID_W - NA_WIN_W)
    col_ok = (kc >= col_start) & (kc < col_start + NA_WIN_W)
    dx = kc - qc + NA_WIN_W - 1
    base = hd * ((2 * NA_WIN_H - 1) * nx)
    tiles = {}
    for dy in range(-(wh - 1), wh):
        t = jnp.zeros((GRID_W, GRID_W), F32)
        for j in range(nx):
            t = jnp.where(dx == j, rpb_ref[base + (dy + NA_WIN_H - 1) * nx + j], t)
        tiles[dy] = jnp.where(col_ok, t * LOG2E, NEG_INF)
    o_ref[...] = jnp.full(o_ref.shape, NEG_INF, F32)
    for qr in range(rows):
        for kr in range(int(row_start[qr]), int(row_start[qr]) + wh):
            o_ref[0, qr * GRID_W:(qr + 1) * GRID_W, kr * GRID_W:(kr + 1) * GRID_W] = tiles[kr - qr]


def _na_bias(rpb, seq):
    rows = seq // GRID_W
    return pl.pallas_call(
        functools.partial(_na_bias_kernel, rows=rows),
        grid=(B_HEADS,),
        in_specs=[pl.BlockSpec(memory_space=pltpu.SMEM)],
        out_specs=pl.BlockSpec((1, seq, seq), lambda h: (h, 0, 0)),
        out_shape=jax.ShapeDtypeStruct((B_HEADS, seq, seq), F32),
        compiler_params=_params(), name="na_bias",
    )(rpb.reshape(-1))


def _na_windows(rows, tq):
    wh, row_start = _na_geometry(rows)
    rq = tq // GRID_W
    wins = []
    for qb in range(rows // rq):
        lo = int(row_start[qb * rq]) * GRID_W
        hi = (int(row_start[qb * rq + rq - 1]) + wh) * GRID_W
        wins.append((lo // LANES * LANES, -(-hi // LANES) * LANES))
    return wins


def _na_kernel(q_ref, k_ref, v_ref, kc_ref, vc_ref, bias_ref, o_ref, *, tq, wins):
    kc = kc_ref[...].astype(BF16)
    vc = vc_ref[...].astype(BF16)
    for qb, (lo, hi) in enumerate(wins):
        rs = slice(qb * tq, (qb + 1) * tq)
        q = q_ref[rs, :]
        s_loc = _dot_nt(q, k_ref[lo:hi, :]) + bias_ref[0, rs, lo:hi]
        o_ref[rs, :] = _softmax_pv([s_loc, _dot_nt(q, kc)], [v_ref[lo:hi, :], vc]).astype(BF16)


def _na(q, k, v, kc, vc, bias, *, batch, seq, tq=512):
    past = kc.shape[1]
    wins = _na_windows(seq // GRID_W, tq)
    tok = pl.BlockSpec((seq, HEAD_DIM), lambda h, b: (b, h))
    cache = pl.BlockSpec((None, past, HEAD_DIM), lambda h, b: (b, 0, h))
    return pl.pallas_call(
        functools.partial(_na_kernel, tq=tq, wins=wins),
        grid=(B_HEADS, batch),
        in_specs=[tok, tok, tok, cache, cache,
                  pl.BlockSpec((1, seq, seq), lambda h, b: (h, 0, 0))],
        out_specs=tok,
        out_shape=jax.ShapeDtypeStruct((batch * seq, B_WIDTH), BF16),
        compiler_params=_params(), name="na",
    )(q, k, v, kc, vc, bias)


def _out_kernel(*refs, latent, tm, seq, n_o, final):
    x_ref, shift_ref, scale_ref, gate_ref, g_ref = refs[:5]
    o_refs = refs[5:5 + n_o]
    wg_ref, wo_ref = refs[5 + n_o:7 + n_o]
    fn_ref = refs[7 + n_o] if final else None
    y_ref = refs[-1]
    row = (pl.program_id(0) * tm) // seq if latent else 0
    h = _mod_norm(x_ref, g_ref, shift_ref, scale_ref, row)
    gt = _dot(h, wg_ref[...])
    o = jnp.concatenate([r[...] for r in o_refs], axis=-1).astype(F32)
    t = (o * _silu(gt)).astype(BF16)
    y = x_ref[...] + gate_ref[pl.ds(row, 1), :] * _dot(t, wo_ref[...])
    if final:
        y = _rms(y, fn_ref[...])
    y_ref[...] = y


def _out(x, mod, g, os_, wg, wo, fn, *, latent, seq, name, tm=512):
    m = x.shape[0]
    width = wg.shape[1]
    final = fn is not None
    in_specs = [pl.BlockSpec((tm, D_MODEL), lambda i: (i, 0))] + _mod_specs(latent, (0, 1, 2))
    in_specs += [_resident((1, D_MODEL))]
    in_specs += [pl.BlockSpec((tm, o.shape[1]), lambda i: (i, 0)) for o in os_]
    in_specs += [_resident((D_MODEL, width)), _resident((width, D_MODEL))]
    args = [x, mod, mod, mod, g, *os_, wg, wo]
    if final:
        in_specs.append(_resident((1, D_MODEL)))
        args.append(fn)
    return pl.pallas_call(
        functools.partial(_out_kernel, latent=latent, tm=tm, seq=seq, n_o=len(os_), final=final),
        grid=(m // tm,), in_specs=in_specs,
        out_specs=pl.BlockSpec((tm, D_MODEL), lambda i: (i, 0)),
        out_shape=jax.ShapeDtypeStruct((m, D_MODEL), F32),
        compiler_params=_params(), name=name,
    )(*args)


def _in1_kernel(*refs, latent, tm, seq):
    if latent:
        (x_ref, shift_ref, scale_ref, g_ref, w_ref, qan_ref, kvn_ref, wq_ref,
         cos_ref, slo_ref, shi_ref, q_ref, ckv_ref, kr_ref) = refs
    else:
        (x_ref, shift_ref, scale_ref, g_ref, w_ref, qan_ref, kvn_ref, wq_ref,
         q_ref, ckv_ref, kr_ref) = refs
    row = (pl.program_id(0) * tm) // seq if latent else 0
    h = _mod_norm(x_ref, g_ref, shift_ref, scale_ref, row)
    z = _dot(h, w_ref[:, :L1_LAT_PAD])
    q_lat = _rms(z[:, :Q_LORA], qan_ref[...]).astype(BF16)
    q = _dot(q_lat, wq_ref[...])
    kr = z[:, Q_LORA + KV_LORA:]
    if latent:
        cos, slo, shi = cos_ref[...], slo_ref[...], shi_ref[...]
        kr = _rope(kr, cos, slo, shi, QK_ROPE // 4)
    for j in range(MLA_HEADS * MLA_QK // LANES):
        sl = slice(j * LANES, (j + 1) * LANES)
        qj = q[:, sl]
        if latent and j % 2 == 1:
            qj = _rope(qj, cos, slo, shi, QK_ROPE // 4)
        q_ref[:, sl] = (qj * Q_SCALE_L1).astype(BF16)
    ckv_ref[...] = _rms(z[:, Q_LORA:Q_LORA + KV_LORA], kvn_ref[...]).astype(ckv_ref.dtype)
    kr_ref[...] = kr[:, :QK_ROPE].astype(kr_ref.dtype)


def _in1(x, mod, g, w, qan, kvn, wq, rope_tabs, *, latent, seq, tm=512):
    m = x.shape[0]
    row_spec = lambda n: pl.BlockSpec((tm, n), lambda i: (i, 0))
    in_specs = [row_spec(D_MODEL)] + _mod_specs(latent, (0, 1)) + [
        _resident((1, D_MODEL)), _resident(w.shape),
        _resident((1, Q_LORA)), _resident((1, KV_LORA)),
        _resident((Q_LORA, MLA_HEADS * MLA_QK))]
    args = [x, mod, mod, g, w, qan, kvn, wq]
    if latent:
        nblk = seq // tm
        in_specs += [pl.BlockSpec((tm, LANES), lambda i: (i % nblk, 0))] * 3
        args += list(rope_tabs)
    lat_dt = BF16 if latent else F32
    widths = [(MLA_HEADS * MLA_QK, BF16), (KV_LORA, lat_dt), (QK_ROPE, lat_dt)]
    return pl.pallas_call(
        functools.partial(_in1_kernel, latent=latent, tm=tm, seq=seq),
        grid=(m // tm,), in_specs=in_specs,
        out_specs=[row_spec(n) for n, _ in widths],
        out_shape=[jax.ShapeDtypeStruct((m, n), dt) for n, dt in widths],
        compiler_params=_params(), name="in1_lat" if latent else "in1_ctx",
    )(*args)


def _mla_kernel(*refs, n_src, hb):
    q_ref = refs[0]
    src = refs[1:1 + 2 * n_src]
    w_ref = refs[1 + 2 * n_src]
    o_ref = refs[2 + 2 * n_src]
    ckv = [src[2 * s][...].astype(BF16) for s in range(n_src)]
    krp = []
    for s in range(n_src):
        kr = src[2 * s + 1][...].astype(F32)
        pad = jnp.zeros((kr.shape[0], LANES - QK_ROPE), F32)
        krp.append(jnp.concatenate([kr, pad], axis=1).astype(BF16))
    kvw = QK_NOPE + V_DIM
    for hd in range(hb):
        w = w_ref[:, hd * kvw:(hd + 1) * kvw]
        kv = [_dot(c, w).astype(BF16) for c in ckv]
        kcat = [jnp.concatenate([kv_s[:, :QK_NOPE], kr_s], axis=1) for kv_s, kr_s in zip(kv, krp)]
        q = q_ref[:, hd * MLA_QK:(hd + 1) * MLA_QK]
        o = _softmax_pv([_dot_nt(q, kc) for kc in kcat], [kv_s[:, QK_NOPE:] for kv_s in kv])
        o_ref[:, hd * V_DIM:(hd + 1) * V_DIM] = o.astype(BF16)


def _mla(q, srcs, w_kv, *, batch, seq, hb, name):
    kvw = QK_NOPE + V_DIM
    in_specs = [pl.BlockSpec((seq, hb * MLA_QK), lambda b, j: (b, j))]
    args = [q]
    for ckv, kr, t in srcs:
        for a in (ckv, kr):
            wd = a.shape[-1]
            if a.ndim == 3:
                in_specs.append(pl.BlockSpec((None, t, wd), lambda b, j: (b, 0, 0)))
            else:
                in_specs.append(pl.BlockSpec((t, wd), lambda b, j: (b, 0)))
            args.append(a)
    in_specs.append(pl.BlockSpec((KV_LORA, hb * kvw), lambda b, j: (0, j)))
    args.append(w_kv)
    return pl.pallas_call(
        functools.partial(_mla_kernel, n_src=len(srcs), hb=hb),
        grid=(batch, MLA_HEADS // hb), in_specs=in_specs,
        out_specs=pl.BlockSpec((seq, hb * V_DIM), lambda b, j: (b, j)),
        out_shape=jax.ShapeDtypeStruct((batch * seq, L1_WIDTH), BF16),
        compiler_params=_params(), name=name,
    )(*args)


def _rope_tables(seq, d):
    a = d // 2
    t = jnp.arange(seq)
    row = (t // GRID_W).astype(F32)
    col = (t % GRID_W).astype(F32)
    inv = 1.0 / (ROPE_THETA ** (jnp.arange(a // 2, dtype=F32) * 2.0 / a))
    ang = jnp.concatenate([row[:, None] * inv] * 2 + [col[:, None] * inv] * 2, axis=-1)
    ang = jnp.tile(ang, (1, LANES // d))
    lower = jnp.tile((jnp.arange(d) % a) < a // 2, LANES // d)[None, :]
    cos, sin = jnp.cos(ang), jnp.sin(ang)
    zero = jnp.zeros_like(sin)
    return cos, jnp.where(lower, -sin, zero), jnp.where(lower, zero, sin)


def kernel(x_prompt, x_sample, cache_l0_a_k, cache_l0_a_v, cache_l0_b_k, cache_l0_b_v, cache_l1_ckv, cache_l1_krope, c, c_ctx, l0_norm, l0_ada_w, l0_ada_b, l0_w_in, l0_q_norm, l0_k_norm, l0_rpb, l0_w_out, l1_norm, l1_ada_w, l1_ada_b, l1_w_in, l1_q_a_norm, l1_w_q_b, l1_kv_a_norm, l1_w_kv_b, l1_w_out, final_norm):
    batch, seq_c, _ = x_prompt.shape
    dec_batch, seq_l, _ = x_sample.shape
    past = cache_l0_a_k.shape[1]

    w0_qkv = _cast_cols(l0_w_in, 0, L0_QKV)
    w0_gate = _cast_cols(l0_w_in, L0_QKV, L0_WIDTH)
    w0_out = l0_w_out.astype(BF16)
    w1_all = l1_w_in.astype(BF16)
    w1_gate = w1_all[:, L1_LAT:]
    w1_out = l1_w_out.astype(BF16)
    wq = l1_w_q_b.reshape(Q_LORA, MLA_HEADS, QK_NOPE + QK_ROPE)
    wq = jnp.pad(wq, ((0, 0), (0, 0), (0, MLA_QK - QK_NOPE - QK_ROPE)))
    wq = wq.reshape(Q_LORA, MLA_HEADS * MLA_QK).astype(BF16)
    w_kv = l1_w_kv_b.astype(BF16)

    row2 = lambda v: v.reshape(1, -1)
    cs = jnp.zeros((ADA_ROWS, D_MODEL), F32).at[:dec_batch].set(c).at[8].set(c_ctx)
    mod0 = _adaln(cs, l0_ada_w, l0_ada_b)
    mod1 = _adaln(cs, l1_ada_w, l1_ada_b)

    rope_a = _rope_tables(seq_l, HEAD_DIM)
    rope_c = _rope_tables(seq_l, QK_ROPE)

    xc = x_prompt.reshape(batch * seq_c, D_MODEL)
    qa, ka, va, qb, kb, vb, ka_f, va_f, kb_f, vb_f = _in0(
        xc, mod0, row2(l0_norm), w0_qkv, row2(l0_q_norm), row2(l0_k_norm), None,
        latent=False, seq=seq_c)
    oa = _attn(qa, [(ka, va, seq_c)], batch=batch, seq=seq_c, heads=A_Q_HEADS,
               group=A_Q_HEADS // A_KV_HEADS, tq=seq_c, name="attn_a_ctx")
    ob = _attn(qb, [(kb, vb, seq_c)], batch=batch, seq=seq_c, heads=B_HEADS, group=1,
               tq=seq_c, name="attn_b_ctx")
    xc = _out(xc, mod0, row2(l0_norm), [oa, ob], w0_gate, w0_out, None,
              latent=False, seq=seq_c, name="out0_ctx")
    q1, ckv_f, kr_f = _in1(xc, mod1, row2(l1_norm), w1_all, row2(l1_q_a_norm),
                           row2(l1_kv_a_norm), wq, None, latent=False, seq=seq_c)
    oc = _mla(q1, [(ckv_f, kr_f, seq_c)], w_kv, batch=batch, seq=seq_c,
              hb=MLA_HEADS, name="mla_ctx")
    y_prompt = _out(xc, mod1, row2(l1_norm), [oc], w1_gate, w1_out, row2(final_norm),
                    latent=False, seq=seq_c, name="out1_ctx")

    xl = x_sample.reshape(dec_batch * seq_l, D_MODEL)
    qa, ka, va, qb, kb, vb = _in0(
        xl, mod0, row2(l0_norm), w0_qkv, row2(l0_q_norm), row2(l0_k_norm), rope_a,
        latent=True, seq=seq_l)
    oa = _attn(qa, [(cache_l0_a_k.reshape(dec_batch, past, A_KV_WIDTH),
                     cache_l0_a_v.reshape(dec_batch, past, A_KV_WIDTH), past),
                    (ka, va, seq_l)],
               batch=dec_batch, seq=seq_l, heads=A_Q_HEADS, group=A_Q_HEADS // A_KV_HEADS,
               tq=256, name="attn_a_lat")
    bias = _na_bias(l0_rpb, seq_l)
    ob = _na(qb, kb, vb, cache_l0_b_k.reshape(dec_batch, past, B_WIDTH),
             cache_l0_b_v.reshape(dec_batch, past, B_WIDTH), bias, batch=dec_batch, seq=seq_l)
    xl = _out(xl, mod0, row2(l0_norm), [oa, ob], w0_gate, w0_out, None,
              latent=True, seq=seq_l, name="out0_lat")
    q1, ckv, kr = _in1(xl, mod1, row2(l1_norm), w1_all, row2(l1_q_a_norm),
                       row2(l1_kv_a_norm), wq, rope_c, latent=True, seq=seq_l)
    oc = _mla(q1, [(cache_l1_ckv, cache_l1_krope, past), (ckv, kr, seq_l)], w_kv,
              batch=dec_batch, seq=seq_l, hb=2, name="mla_lat")
    y_sample = _out(xl, mod1, row2(l1_norm), [oc], w1_gate, w1_out, row2(final_norm),
                    latent=True, seq=seq_l, name="out1_lat")

    return (y_prompt.reshape(batch, seq_c, D_MODEL),
            y_sample.reshape(dec_batch, seq_l, D_MODEL),
            ka_f.reshape(batch, seq_c, A_KV_HEADS, HEAD_DIM),
            va_f.reshape(batch, seq_c, A_KV_HEADS, HEAD_DIM),
            kb_f.reshape(batch, seq_c, B_HEADS, HEAD_DIM),
            vb_f.reshape(batch, seq_c, B_HEADS, HEAD_DIM),
            ckv_f.reshape(batch, seq_c, KV_LORA),
            kr_f.reshape(batch, seq_c, QK_ROPE))
```

```python
import functools

import jax
import jax.numpy as jnp
import numpy as np
from jax import lax
from jax.experimental import pallas as pl
from jax.experimental.pallas import tpu as pltpu

LANES = 128
VMEM_LIMIT = 56 * 1024 * 1024
ADA_ROWS = 16
F32 = jnp.float32
BF16 = jnp.bfloat16

D_MODEL = 2048
GRID_W = 64
HEAD_DIM = 128
A_Q_HEADS = 8
A_KV_HEADS = 2
B_HEADS = 8
A_WIDTH = A_Q_HEADS * HEAD_DIM
A_KV_WIDTH = A_KV_HEADS * HEAD_DIM
B_WIDTH = B_HEADS * HEAD_DIM
L0_WIDTH = A_WIDTH + B_WIDTH
L0_QKV = A_WIDTH + 2 * A_KV_WIDTH + 3 * B_WIDTH
NA_WIN_H = 8
NA_WIN_W = 16
MLA_HEADS = 16
Q_LORA = 512
KV_LORA = 512
QK_NOPE = 128
QK_ROPE = 64
V_DIM = 128
L1_WIDTH = MLA_HEADS * V_DIM
L1_LAT = Q_LORA + KV_LORA + QK_ROPE
L1_LAT_PAD = L1_LAT + (LANES - QK_ROPE)
MLA_QK = 2 * LANES
ROPE_THETA = 10000.0
EPS = 1e-6
NEG_INF = -1e30
LOG2E = 1.4426950408889634
Q_SCALE_L0 = HEAD_DIM ** -0.5 * LOG2E
Q_SCALE_L1 = (QK_NOPE + QK_ROPE) ** -0.5 * LOG2E


def _params(vmem=VMEM_LIMIT):
    return pltpu.CompilerParams(vmem_limit_bytes=vmem)


def _resident(shape):
    nd = len(shape)
    return pl.BlockSpec(shape, lambda *_: (0,) * nd, pipeline_mode=pl.Buffered(1))


def _dot(a, b):
    return jnp.dot(a, b, preferred_element_type=F32)


def _dot_nt(a, b):
    return lax.dot_general(a, b, (((1,), (1,)), ((), ())), preferred_element_type=F32)


def _rms(x, g):
    return x * lax.rsqrt(jnp.mean(x * x, axis=-1, keepdims=True) + EPS) * g


def _silu(x):
    return x * jax.nn.sigmoid(x)


def _rope(x, cos, sin_lo, sin_hi, half):
    return (x * cos + pltpu.roll(x, half, 1) * sin_hi
            + pltpu.roll(x, LANES - half, 1) * sin_lo)


def _mod_norm(x_ref, g_ref, shift_ref, scale_ref, row):
    x = x_ref[...]
    h = _rms(x, g_ref[...]) * (1.0 + scale_ref[pl.ds(row, 1), :]) + shift_ref[pl.ds(row, 1), :]
    return h.astype(BF16)


def _with_ones(v):
    v = v.astype(BF16)
    return jnp.concatenate([v, jnp.ones_like(v)], axis=1)


def _softmax_pv(ss, vs):
    mx = ss[0].max(axis=-1, keepdims=True)
    for s in ss[1:]:
        mx = jnp.maximum(mx, s.max(axis=-1, keepdims=True))
    acc = _dot(jnp.exp2(ss[0] - mx).astype(BF16), vs[0])
    for s, v in zip(ss[1:], vs[1:]):
        acc = acc + _dot(jnp.exp2(s - mx).astype(BF16), v)
    return acc[:, :LANES] / acc[:, LANES:]


def _cast_kernel(w_ref, o_ref):
    o_ref[...] = w_ref[...].astype(BF16)


def _cast_cols(w, col0, ncols, tn=512):
    rows = w.shape[0]
    assert col0 % tn == 0 and ncols % tn == 0
    off = col0 // tn
    return pl.pallas_call(
        _cast_kernel, grid=(ncols // tn,),
        in_specs=[pl.BlockSpec((rows, tn), lambda j: (0, j + off))],
        out_specs=pl.BlockSpec((rows, tn), lambda j: (0, j)),
        out_shape=jax.ShapeDtypeStruct((rows, ncols), BF16),
        compiler_params=_params(), name="cast",
    )(w)


def _adaln_kernel(c_ref, w_ref, b_ref, o_ref):
    c = c_ref[...]
    o_ref[...] = _dot(_silu(c).astype(BF16), w_ref[...].astype(BF16)) + b_ref[...]


def _adaln(cs, w, b):
    tn = 1024
    n = w.shape[1]
    return pl.pallas_call(
        _adaln_kernel,
        grid=(n // tn,),
        in_specs=[pl.BlockSpec((ADA_ROWS, D_MODEL), lambda j: (0, 0)),
                  pl.BlockSpec((D_MODEL, tn), lambda j: (0, j)),
                  pl.BlockSpec((1, tn), lambda j: (0, j))],
        out_specs=pl.BlockSpec((ADA_ROWS, tn), lambda j: (0, j)),
        out_shape=jax.ShapeDtypeStruct((ADA_ROWS, n), F32),
        compiler_params=_params(),
        name="adaln",
    )(cs, w, b.reshape(1, n))


def _mod_specs(latent, cols):
    rb = 0 if latent else 1
    return [pl.BlockSpec((8, D_MODEL), lambda i, c=c: (rb, c)) for c in cols]


def _in0_kernel(*refs, latent, tm, seq):
    if latent:
        (x_ref, shift_ref, scale_ref, g_ref, w_ref, qn_ref, kn_ref, cos_ref, slo_ref, shi_ref,
         qa_ref, ka_ref, va_ref, qb_ref, kb_ref, vb_ref) = refs
    else:
        (x_ref, shift_ref, scale_ref, g_ref, w_ref, qn_ref, kn_ref,
         qa_ref, ka_ref, va_ref, qb_ref, kb_ref, vb_ref,
         kaf_ref, vaf_ref, kbf_ref, vbf_ref) = refs
    row = (pl.program_id(0) * tm) // seq if latent else 0
    h = _mod_norm(x_ref, g_ref, shift_ref, scale_ref, row)

    def head_norm(y, g_ref_):
        y = _rms(y, g_ref_[...])
        if latent:
            y = _rope(y, cos_ref[...], slo_ref[...], shi_ref[...], HEAD_DIM // 4)
        return y

    c0 = 0
    y = _dot(h, w_ref[:, c0:c0 + A_WIDTH])
    for hd in range(A_Q_HEADS):
        sl = slice(hd * HEAD_DIM, (hd + 1) * HEAD_DIM)
        qa_ref[:, sl] = (head_norm(y[:, sl], qn_ref) * Q_SCALE_L0).astype(BF16)
    c0 += A_WIDTH
    y = _dot(h, w_ref[:, c0:c0 + A_KV_WIDTH])
    for hd in range(A_KV_HEADS):
        sl = slice(hd * HEAD_DIM, (hd + 1) * HEAD_DIM)
        k = head_norm(y[:, sl], kn_ref)
        ka_ref[:, sl] = k.astype(BF16)
        if not latent:
            kaf_ref[:, sl] = k
    c0 += A_KV_WIDTH
    y = _dot(h, w_ref[:, c0:c0 + A_KV_WIDTH])
    va_ref[...] = y.astype(BF16)
    if not latent:
        vaf_ref[...] = y
    c0 += A_KV_WIDTH
    y = _dot(h, w_ref[:, c0:c0 + B_WIDTH])
    qb_ref[...] = (y * Q_SCALE_L0).astype(BF16)
    c0 += B_WIDTH
    y = _dot(h, w_ref[:, c0:c0 + B_WIDTH])
    kb_ref[...] = y.astype(BF16)
    if not latent:
        kbf_ref[...] = y
    c0 += B_WIDTH
    y = _dot(h, w_ref[:, c0:c0 + B_WIDTH])
    vb_ref[...] = y.astype(BF16)
    if not latent:
        vbf_ref[...] = y


def _in0(x, mod, g, w, qn, kn, rope_tabs, *, latent, seq, tm=512):
    m = x.shape[0]
    row_spec = lambda n: pl.BlockSpec((tm, n), lambda i: (i, 0))
    in_specs = [row_spec(D_MODEL)] + _mod_specs(latent, (0, 1)) + [
        _resident((1, D_MODEL)), _resident((D_MODEL, L0_QKV)),
        _resident((1, HEAD_DIM)), _resident((1, HEAD_DIM))]
    args = [x, mod, mod, g, w, qn, kn]
    if latent:
        nblk = seq // tm
        in_specs += [pl.BlockSpec((tm, LANES), lambda i: (i % nblk, 0))] * 3
        args += list(rope_tabs)
    widths = [A_WIDTH, A_KV_WIDTH, A_KV_WIDTH, B_WIDTH, B_WIDTH, B_WIDTH]
    out_specs = [row_spec(n) for n in widths]
    out_shape = [jax.ShapeDtypeStruct((m, n), BF16) for n in widths]
    if not latent:
        fw = [A_KV_WIDTH, A_KV_WIDTH, B_WIDTH, B_WIDTH]
        out_specs += [row_spec(n) for n in fw]
        out_shape += [jax.ShapeDtypeStruct((m, n), F32) for n in fw]
    return pl.pallas_call(
        functools.partial(_in0_kernel, latent=latent, tm=tm, seq=seq),
        grid=(m // tm,), in_specs=in_specs, out_specs=out_specs, out_shape=out_shape,
        compiler_params=_params(), name="in0_lat" if latent else "in0_ctx",
    )(*args)


def _attn_kernel(*refs, n_src, heads, group, sub):
    q_ref = refs[0]
    kv_refs = refs[1:1 + 2 * n_src]
    o_ref = refs[1 + 2 * n_src]
    for j in range(heads // group):
        sl_kv = slice(j * HEAD_DIM, (j + 1) * HEAD_DIM)
        ks = [kv_refs[2 * s][:, sl_kv].astype(BF16) for s in range(n_src)]
        vs = [_with_ones(kv_refs[2 * s + 1][:, sl_kv]) for s in range(n_src)]
        sls = [slice((j * group + gi) * HEAD_DIM, (j * group + gi + 1) * HEAD_DIM)
               for gi in range(group)]
        for r0 in range(0, q_ref.shape[0], sub):
            q = jnp.concatenate([q_ref[r0:r0 + sub, sl] for sl in sls], axis=0)
            o = _softmax_pv([_dot_nt(q, k) for k in ks], vs).astype(BF16)
            for gi, sl in enumerate(sls):
                o_ref[r0:r0 + sub, sl] = o[gi * sub:(gi + 1) * sub]


def _attn(q, srcs, *, batch, seq, heads, group, tq, name):
    kvw = (heads // group) * HEAD_DIM
    nq = seq // tq
    in_specs = [pl.BlockSpec((tq, heads * HEAD_DIM), lambda b, i: (b * nq + i, 0))]
    args = [q]
    for k, v, t in srcs:
        for a in (k, v):
            if a.ndim == 3:
                in_specs.append(pl.BlockSpec((None, t, kvw), lambda b, i: (b, 0, 0)))
            else:
                in_specs.append(pl.BlockSpec((t, kvw), lambda b, i: (b, 0)))
            args.append(a)
    return pl.pallas_call(
        functools.partial(_attn_kernel, n_src=len(srcs), heads=heads, group=group,
                          sub=min(tq, 256)),
        grid=(batch, nq), in_specs=in_specs,
        out_specs=pl.BlockSpec((tq, heads * HEAD_DIM), lambda b, i: (b * nq + i, 0)),
        out_shape=jax.ShapeDtypeStruct((batch * seq, heads * HEAD_DIM), BF16),
        compiler_params=_params(), name=name,
    )(*args)


def _na_geometry(rows):
    wh = min(NA_WIN_H, rows)
    row_start = np.clip(np.arange(rows) - wh // 2, 0, rows - wh)
    return wh, row_start


def _na_bias_kernel(rpb_ref, o_ref, *, rows):
    hd = pl.program_id(0)
    wh, row_start = _na_geometry(rows)
    nx = 2 * NA_WIN_W - 1
    qc = lax.broadcasted_iota(jnp.int32, (GRID_W, GRID_W), 0)
    kc = lax.broadcasted_iota(jnp.int32, (GRID_W, GRID_W), 1)
    col_start = jnp.clip(qc - NA_WIN_W // 2, 0, GRID_W - NA_WIN_W)
    col_ok = (kc >= col_start) & (kc < col_start + NA_WIN_W)
    dx = kc - qc + NA_WIN_W - 1
    base = hd * ((2 * NA_WIN_H - 1) * nx)
    tiles = {}
    for dy in range(-(wh - 1), wh):
        t = jnp.zeros((GRID_W, GRID_W), F32)
        for j in range(nx):
            t = jnp.where(dx == j, rpb_ref[base + (dy + NA_WIN_H - 1) * nx + j], t)
        tiles[dy] = jnp.where(col_ok, t * LOG2E, NEG_INF)
    o_ref[...] = jnp.full(o_ref.shape, NEG_INF, F32)
    for qr in range(rows):
        for kr in range(int(row_start[qr]), int(row_start[qr]) + wh):
            o_ref[0, qr * GRID_W:(qr + 1) * GRID_W, kr * GRID_W:(kr + 1) * GRID_W] = tiles[kr - qr]


def _na_bias(rpb, seq):
    rows = seq // GRID_W
    return pl.pallas_call(
        functools.partial(_na_bias_kernel, rows=rows),
        grid=(B_HEADS,),
        in_specs=[pl.BlockSpec(memory_space=pltpu.SMEM)],
        out_specs=pl.BlockSpec((1, seq, seq), lambda h: (h, 0, 0)),
        out_shape=jax.ShapeDtypeStruct((B_HEADS, seq, seq), F32),
        compiler_params=_params(), name="na_bias",
    )(rpb.reshape(-1))


def _na_windows(rows, tq):
    wh, row_start = _na_geometry(rows)
    rq = tq // GRID_W
    wins = []
    for qb in range(rows // rq):
        lo = int(row_start[qb * rq]) * GRID_W
        hi = (int(row_start[qb * rq + rq - 1]) + wh) * GRID_W
        wins.append((lo // LANES * LANES, -(-hi // LANES) * LANES))
    return wins


def _na_kernel(q_ref, k_ref, v_ref, kc_ref, vc_ref, bias_ref, o_ref, *, seq, tq, wins):
    for b in range(q_ref.shape[0] // seq):
        kc = kc_ref[b].astype(BF16)
        vc = _with_ones(vc_ref[b])
        v = _with_ones(v_ref[b * seq:(b + 1) * seq, :])
        for qb, (lo, hi) in enumerate(wins):
            rs = slice(qb * tq, (qb + 1) * tq)
            rows = slice(b * seq + qb * tq, b * seq + (qb + 1) * tq)
            q = q_ref[rows, :]
            s_loc = _dot_nt(q, k_ref[b * seq + lo:b * seq + hi, :]) + bias_ref[0, rs, lo:hi]
            o_ref[rows, :] = _softmax_pv([s_loc, _dot_nt(q, kc)], [v[lo:hi], vc]).astype(BF16)


def _na(q, k, v, kc, vc, bias, *, batch, seq, tq=512, nb=4):
    past = kc.shape[1]
    wins = _na_windows(seq // GRID_W, tq)
    tok = pl.BlockSpec((nb * seq, HEAD_DIM), lambda h, b: (b, h))
    cache = pl.BlockSpec((nb, past, HEAD_DIM), lambda h, b: (b, 0, h))
    return pl.pallas_call(
        functools.partial(_na_kernel, seq=seq, tq=tq, wins=wins),
        grid=(B_HEADS, batch // nb),
        in_specs=[tok, tok, tok, cache, cache,
                  pl.BlockSpec((1, seq, seq), lambda h, b: (h, 0, 0))],
        out_specs=tok,
        out_shape=jax.ShapeDtypeStruct((batch * seq, B_WIDTH), BF16),
        compiler_params=_params(), name="na",
    )(q, k, v, kc, vc, bias)


def _out_kernel(*refs, latent, tm, seq, n_o, final):
    x_ref, shift_ref, scale_ref, gate_ref, g_ref = refs[:5]
    o_refs = refs[5:5 + n_o]
    wg_ref, wo_ref = refs[5 + n_o:7 + n_o]
    fn_ref = refs[7 + n_o] if final else None
    y_ref = refs[-1]
    row = (pl.program_id(0) * tm) // seq if latent else 0
    h = _mod_norm(x_ref, g_ref, shift_ref, scale_ref, row)
    gt = _dot(h, wg_ref[...])
    o = jnp.concatenate([r[...] for r in o_refs], axis=-1).astype(F32)
    t = (o * _silu(gt)).astype(BF16)
    y = x_ref[...] + gate_ref[pl.ds(row, 1), :] * _dot(t, wo_ref[...])
    if final:
        y = _rms(y, fn_ref[...])
    y_ref[...] = y


def _out(x, mod, g, os_, wg, wo, fn, *, latent, seq, name, tm=512):
    m = x.shape[0]
    width = wg.shape[1]
    final = fn is not None
    in_specs = [pl.BlockSpec((tm, D_MODEL), lambda i: (i, 0))] + _mod_specs(latent, (0, 1, 2))
    in_specs += [_resident((1, D_MODEL))]
    in_specs += [pl.BlockSpec((tm, o.shape[1]), lambda i: (i, 0)) for o in os_]
    in_specs += [_resident((D_MODEL, width)), _resident((width, D_MODEL))]
    args = [x, mod, mod, mod, g, *os_, wg, wo]
    if final:
        in_specs.append(_resident((1, D_MODEL)))
        args.append(fn)
    return pl.pallas_call(
        functools.partial(_out_kernel, latent=latent, tm=tm, seq=seq, n_o=len(os_), final=final),
        grid=(m // tm,), in_specs=in_specs,
        out_specs=pl.BlockSpec((tm, D_MODEL), lambda i: (i, 0)),
        out_shape=jax.ShapeDtypeStruct((m, D_MODEL), F32),
        compiler_params=_params(), name=name,
    )(*args)


def _in1_kernel(*refs, latent, tm, seq):
    if latent:
        (x_ref, shift_ref, scale_ref, g_ref, w_ref, qan_ref, kvn_ref, wq_ref,
         cos_ref, slo_ref, shi_ref, q_ref, ckv_ref, kr_ref) = refs
    else:
        (x_ref, shift_ref, scale_ref, g_ref, w_ref, qan_ref, kvn_ref, wq_ref,
         q_ref, ckv_ref, kr_ref) = refs
    row = (pl.program_id(0) * tm) // seq if latent else 0
    h = _mod_norm(x_ref, g_ref, shift_ref, scale_ref, row)
    z = _dot(h, w_ref[:, :L1_LAT_PAD])
    q_lat = _rms(z[:, :Q_LORA], qan_ref[...]).astype(BF16)
    q = _dot(q_lat, wq_ref[...])
    kr = z[:, Q_LORA + KV_LORA:]
    if latent:
        cos, slo, shi = cos_ref[...], slo_ref[...], shi_ref[...]
        kr = _rope(kr, cos, slo, shi, QK_ROPE // 4)
    for j in range(MLA_HEADS * MLA_QK // LANES):
        sl = slice(j * LANES, (j + 1) * LANES)
        qj = q[:, sl]
        if latent and j % 2 == 1:
            qj = _rope(qj, cos, slo, shi, QK_ROPE // 4)
        q_ref[:, sl] = (qj * Q_SCALE_L1).astype(BF16)
    ckv_ref[...] = _rms(z[:, Q_LORA:Q_LORA + KV_LORA], kvn_ref[...]).astype(ckv_ref.dtype)
    kr_ref[...] = kr[:, :QK_ROPE].astype(kr_ref.dtype)


def _in1(x, mod, g, w, qan, kvn, wq, rope_tabs, *, latent, seq, tm=512):
    m = x.shape[0]
    row_spec = lambda n: pl.BlockSpec((tm, n), lambda i: (i, 0))
    in_specs = [row_spec(D_MODEL)] + _mod_specs(latent, (0, 1)) + [
        _resident((1, D_MODEL)), _resident(w.shape),
        _resident((1, Q_LORA)), _resident((1, KV_LORA)),
        _resident((Q_LORA, MLA_HEADS * MLA_QK))]
    args = [x, mod, mod, g, w, qan, kvn, wq]
    if latent:
        nblk = seq // tm
        in_specs += [pl.BlockSpec((tm, LANES), lambda i: (i % nblk, 0))] * 3
        args += list(rope_tabs)
    lat_dt = BF16 if latent else F32
    widths = [(MLA_HEADS * MLA_QK, BF16), (KV_LORA, lat_dt), (QK_ROPE, lat_dt)]
    return pl.pallas_call(
        functools.partial(_in1_kernel, latent=latent, tm=tm, seq=seq),
        grid=(m // tm,), in_specs=in_specs,
        out_specs=[row_spec(n) for n, _ in widths],
        out_shape=[jax.ShapeDtypeStruct((m, n), dt) for n, dt in widths],
        compiler_params=_params(), name="in1_lat" if latent else "in1_ctx",
    )(*args)


def _mla_kernel(*refs, n_src, hb):
    q_ref = refs[0]
    src = refs[1:1 + 2 * n_src]
    w_ref = refs[1 + 2 * n_src]
    o_ref = refs[2 + 2 * n_src]
    ckv = [src[2 * s][...].astype(BF16) for s in range(n_src)]
    krp = []
    for s in range(n_src):
        kr = src[2 * s + 1][...].astype(F32)
        pad = jnp.zeros((kr.shape[0], LANES - QK_ROPE), F32)
        krp.append(jnp.concatenate([kr, pad], axis=1).astype(BF16))
    kvw = QK_NOPE + V_DIM
    for hd in range(hb):
        w = w_ref[:, hd * kvw:(hd + 1) * kvw]
        kv = [_dot(c, w).astype(BF16) for c in ckv]
        kcat = [jnp.concatenate([kv_s[:, :QK_NOPE], kr_s], axis=1) for kv_s, kr_s in zip(kv, krp)]
        q = q_ref[:, hd * MLA_QK:(hd + 1) * MLA_QK]
        o = _softmax_pv([_dot_nt(q, kc) for kc in kcat],
                        [_with_ones(kv_s[:, QK_NOPE:]) for kv_s in kv])
        o_ref[:, hd * V_DIM:(hd + 1) * V_DIM] = o.astype(BF16)


def _mla(q, srcs, w_kv, *, batch, seq, hb, name):
    kvw = QK_NOPE + V_DIM
    in_specs = [pl.BlockSpec((seq, hb * MLA_QK), lambda b, j: (b, j))]
    args = [q]
    for ckv, kr, t in srcs:
        for a in (ckv, kr):
            wd = a.shape[-1]
            if a.ndim == 3:
                in_specs.append(pl.BlockSpec((None, t, wd), lambda b, j: (b, 0, 0)))
            else:
                in_specs.append(pl.BlockSpec((t, wd), lambda b, j: (b, 0)))
            args.append(a)
    in_specs.append(pl.BlockSpec((KV_LORA, hb * kvw), lambda b, j: (0, j)))
    args.append(w_kv)
    return pl.pallas_call(
        functools.partial(_mla_kernel, n_src=len(srcs), hb=hb),
        grid=(batch, MLA_HEADS // hb), in_specs=in_specs,
        out_specs=pl.BlockSpec((seq, hb * V_DIM), lambda b, j: (b, j)),
        out_shape=jax.ShapeDtypeStruct((batch * seq, L1_WIDTH), BF16),
        compiler_params=_params(), name=name,
    )(*args)


def _rope_tables(seq, d):
    a = d // 2
    t = jnp.arange(seq)
    row = (t // GRID_W).astype(F32)
    col = (t % GRID_W).astype(F32)
    inv = 1.0 / (ROPE_THETA ** (jnp.arange(a // 2, dtype=F32) * 2.0 / a))
    ang = jnp.concatenate([row[:, None] * inv] * 2 + [col[:, None] * inv] * 2, axis=-1)
    ang = jnp.tile(ang, (1, LANES // d))
    lower = jnp.tile((jnp.arange(d) % a) < a // 2, LANES // d)[None, :]
    cos, sin = jnp.cos(ang), jnp.sin(ang)
    zero = jnp.zeros_like(sin)
    return cos, jnp.where(lower, -sin, zero), jnp.where(lower, zero, sin)


def kernel(x_prompt, x_sample, cache_l0_a_k, cache_l0_a_v, cache_l0_b_k, cache_l0_b_v, cache_l1_ckv, cache_l1_krope, c, c_ctx, l0_norm, l0_ada_w, l0_ada_b, l0_w_in, l0_q_norm, l0_k_norm, l0_rpb, l0_w_out, l1_norm, l1_ada_w, l1_ada_b, l1_w_in, l1_q_a_norm, l1_w_q_b, l1_kv_a_norm, l1_w_kv_b, l1_w_out, final_norm):
    batch, seq_c, _ = x_prompt.shape
    dec_batch, seq_l, _ = x_sample.shape
    past = cache_l0_a_k.shape[1]

    w0_qkv = _cast_cols(l0_w_in, 0, L0_QKV)
    w0_gate = _cast_cols(l0_w_in, L0_QKV, L0_WIDTH)
    w0_out = l0_w_out.astype(BF16)
    w1_all = l1_w_in.astype(BF16)
    w1_gate = w1_all[:, L1_LAT:]
    w1_out = l1_w_out.astype(BF16)
    wq = l1_w_q_b.reshape(Q_LORA, MLA_HEADS, QK_NOPE + QK_ROPE)
    wq = jnp.pad(wq, ((0, 0), (0, 0), (0, MLA_QK - QK_NOPE - QK_ROPE)))
    wq = wq.reshape(Q_LORA, MLA_HEADS * MLA_QK).astype(BF16)
    w_kv = l1_w_kv_b.astype(BF16)

    row2 = lambda v: v.reshape(1, -1)
    cs = jnp.zeros((ADA_ROWS, D_MODEL), F32).at[:dec_batch].set(c).at[8].set(c_ctx)
    mod0 = _adaln(cs, l0_ada_w, l0_ada_b)
    mod1 = _adaln(cs, l1_ada_w, l1_ada_b)

    rope_a = _rope_tables(seq_l, HEAD_DIM)
    rope_c = _rope_tables(seq_l, QK_ROPE)

    xc = x_prompt.reshape(batch * seq_c, D_MODEL)
    qa, ka, va, qb, kb, vb, ka_f, va_f, kb_f, vb_f = _in0(
        xc, mod0, row2(l0_norm), w0_qkv, row2(l0_q_norm), row2(l0_k_norm), None,
        latent=False, seq=seq_c)
    oa = _attn(qa, [(ka, va, seq_c)], batch=batch, seq=seq_c, heads=A_Q_HEADS,
               group=A_Q_HEADS // A_KV_HEADS, tq=seq_c, name="attn_a_ctx")
    ob = _attn(qb, [(kb, vb, seq_c)], batch=batch, seq=seq_c, heads=B_HEADS, group=1,
               tq=seq_c, name="attn_b_ctx")
    xc = _out(xc, mod0, row2(l0_norm), [oa, ob], w0_gate, w0_out, None,
              latent=False, seq=seq_c, name="out0_ctx")
    q1, ckv_f, kr_f = _in1(xc, mod1, row2(l1_norm), w1_all, row2(l1_q_a_norm),
                           row2(l1_kv_a_norm), wq, None, latent=False, seq=seq_c)
    oc = _mla(q1, [(ckv_f, kr_f, seq_c)], w_kv, batch=batch, seq=seq_c,
              hb=MLA_HEADS, name="mla_ctx")
    y_prompt = _out(xc, mod1, row2(l1_norm), [oc], w1_gate, w1_out, row2(final_norm),
                    latent=False, seq=seq_c, name="out1_ctx")

    xl = x_sample.reshape(dec_batch * seq_l, D_MODEL)
    qa, ka, va, qb, kb, vb = _in0(
        xl, mod0, row2(l0_norm), w0_qkv, row2(l0_q_norm), row2(l0_k_norm), rope_a,
        latent=True, seq=seq_l)
    oa = _attn(qa, [(cache_l0_a_k.reshape(dec_batch, past, A_KV_WIDTH),
                     cache_l0_a_v.reshape(dec_batch, past, A_KV_WIDTH), past),
                    (ka, va, seq_l)],
               batch=dec_batch, seq=seq_l, heads=A_Q_HEADS, group=A_Q_HEADS // A_KV_HEADS,
               tq=seq_l, name="attn_a_lat")
    bias = _na_bias(l0_rpb, seq_l)
    ob = _na(qb, kb, vb, cache_l0_b_k.reshape(dec_batch, past, B_WIDTH),
             cache_l0_b_v.reshape(dec_batch, past, B_WIDTH), bias, batch=dec_batch, seq=seq_l)
    xl = _out(xl, mod0, row2(l0_norm), [oa, ob], w0_gate, w0_out, None,
              latent=True, seq=seq_l, name="out0_lat")
    q1, ckv, kr = _in1(xl, mod1, row2(l1_norm), w1_all, row2(l1_q_a_norm),
                       row2(l1_kv_a_norm), wq, rope_c, latent=True, seq=seq_l)
    oc = _mla(q1, [(cache_l1_ckv, cache_l1_krope, past), (ckv, kr, seq_l)], w_kv,
              batch=dec_batch, seq=seq_l, hb=8, name="mla_lat")
    y_sample = _out(xl, mod1, row2(l1_norm), [oc], w1_gate, w1_out, row2(final_norm),
                    latent=True, seq=seq_l, name="out1_lat")

    return (y_prompt.reshape(batch, seq_c, D_MODEL),
            y_sample.reshape(dec_batch, seq_l, D_MODEL),
            ka_f.reshape(batch, seq_c, A_KV_HEADS, HEAD_DIM),
            va_f.reshape(batch, seq_c, A_KV_HEADS, HEAD_DIM),
            kb_f.reshape(batch, seq_c, B_HEADS, HEAD_DIM),
            vb_f.reshape(batch, seq_c, B_HEADS, HEAD_DIM),
            ckv_f.reshape(batch, seq_c, KV_LORA),
            kr_f.reshape(batch, seq_c, QK_ROPE))
```

```python
import functools

import jax
import jax.numpy as jnp
import numpy as np
from jax import lax
from jax.experimental import pallas as pl
from jax.experimental.pallas import tpu as pltpu

LANES = 128
VMEM_LIMIT = 56 * 1024 * 1024
ADA_ROWS = 16
F32 = jnp.float32
BF16 = jnp.bfloat16

D_MODEL = 2048
GRID_W = 64
HEAD_DIM = 128
A_Q_HEADS = 8
A_KV_HEADS = 2
B_HEADS = 8
A_WIDTH = A_Q_HEADS * HEAD_DIM
A_KV_WIDTH = A_KV_HEADS * HEAD_DIM
B_WIDTH = B_HEADS * HEAD_DIM
L0_WIDTH = A_WIDTH + B_WIDTH
L0_QKV = A_WIDTH + 2 * A_KV_WIDTH + 3 * B_WIDTH
NA_WIN_H = 8
NA_WIN_W = 16
MLA_HEADS = 16
Q_LORA = 512
KV_LORA = 512
QK_NOPE = 128
QK_ROPE = 64
V_DIM = 128
L1_WIDTH = MLA_HEADS * V_DIM
L1_LAT = Q_LORA + KV_LORA + QK_ROPE
L1_LAT_PAD = L1_LAT + (LANES - QK_ROPE)
MLA_QK = 2 * LANES
ROPE_THETA = 10000.0
EPS = 1e-6
NEG_INF = -1e30
LOG2E = 1.4426950408889634
Q_SCALE_L0 = HEAD_DIM ** -0.5 * LOG2E
Q_SCALE_L1 = (QK_NOPE + QK_ROPE) ** -0.5 * LOG2E


def _params(vmem=VMEM_LIMIT):
    return pltpu.CompilerParams(vmem_limit_bytes=vmem)


def _resident(shape):
    nd = len(shape)
    return pl.BlockSpec(shape, lambda *_: (0,) * nd, pipeline_mode=pl.Buffered(1))


def _dot(a, b):
    return jnp.dot(a, b, preferred_element_type=F32)


def _dot_nt(a, b):
    return lax.dot_general(a, b, (((1,), (1,)), ((), ())), preferred_element_type=F32)


def _rms(x, g):
    return x * lax.rsqrt(jnp.mean(x * x, axis=-1, keepdims=True) + EPS) * g


def _silu(x):
    return x * jax.nn.sigmoid(x)


def _rope(x, cos, sin_lo, sin_hi, half):
    return (x * cos + pltpu.roll(x, half, 1) * sin_hi
            + pltpu.roll(x, LANES - half, 1) * sin_lo)


def _mod_norm(x_ref, g_ref, shift_ref, scale_ref, row):
    x = x_ref[...]
    h = _rms(x, g_ref[...]) * (1.0 + scale_ref[pl.ds(row, 1), :]) + shift_ref[pl.ds(row, 1), :]
    return h.astype(BF16)


def _with_ones(v):
    v = v.astype(BF16)
    return jnp.concatenate([v, jnp.ones_like(v)], axis=1)


def _softmax_pv(ss, vs):
    mx = ss[0].max(axis=-1, keepdims=True)
    for s in ss[1:]:
        mx = jnp.maximum(mx, s.max(axis=-1, keepdims=True))
    acc = _dot(jnp.exp2(ss[0] - mx).astype(BF16), vs[0])
    for s, v in zip(ss[1:], vs[1:]):
        acc = acc + _dot(jnp.exp2(s - mx).astype(BF16), v)
    return acc[:, :LANES] / acc[:, LANES:]


WEIGHT_ROW_BLOCKS = 16


def _prep_kernel(w0_ref, w0o_ref, w1_ref, wq_ref, wkv_ref, w1o_ref,
                 qkv_ref, g0_ref, o0_ref, lat_ref, g1_ref, q_ref, kv_ref, o1_ref):
    qkv_ref[...] = w0_ref[:, :L0_QKV].astype(BF16)
    g0_ref[...] = w0_ref[:, L0_QKV:].astype(BF16)
    o0_ref[...] = w0o_ref[...].astype(BF16)
    lat_ref[...] = w1_ref[:, :L1_LAT_PAD].astype(BF16)
    g1_ref[...] = w1_ref[:, L1_LAT:].astype(BF16)
    qw = QK_NOPE + QK_ROPE
    pad = jnp.zeros((wq_ref.shape[0], MLA_QK - qw), F32)
    parts = []
    for hd in range(MLA_HEADS):
        parts += [wq_ref[:, hd * qw:(hd + 1) * qw], pad]
    q_ref[...] = jnp.concatenate(parts, axis=1).astype(BF16)
    kv_ref[...] = wkv_ref[...].astype(BF16)
    o1_ref[...] = w1o_ref[...].astype(BF16)


def _prep_weights(w0, w0o, w1, wq, wkv, w1o):
    nb = WEIGHT_ROW_BLOCKS
    ins = [w0, w0o, w1, wq, wkv, w1o]
    out_cols = [(D_MODEL, L0_QKV), (D_MODEL, L0_WIDTH), (L0_WIDTH, D_MODEL),
                (D_MODEL, L1_LAT_PAD), (D_MODEL, L1_WIDTH), (Q_LORA, MLA_HEADS * MLA_QK),
                (KV_LORA, MLA_HEADS * (QK_NOPE + V_DIM)), (L1_WIDTH, D_MODEL)]
    spec = lambda r, c: pl.BlockSpec((r // nb, c), lambda i: (i, 0))
    return pl.pallas_call(
        _prep_kernel, grid=(nb,),
        in_specs=[spec(*a.shape) for a in ins],
        out_specs=[spec(r, c) for r, c in out_cols],
        out_shape=[jax.ShapeDtypeStruct(rc, BF16) for rc in out_cols],
        compiler_params=_params(), name="prep_weights",
    )(*ins)


def _adaln_kernel(c_ref, w_ref, b_ref, o_ref):
    c = c_ref[...]
    o_ref[...] = _dot(_silu(c).astype(BF16), w_ref[...].astype(BF16)) + b_ref[...]


def _adaln(cs, w, b):
    tn = 1024
    n = w.shape[1]
    return pl.pallas_call(
        _adaln_kernel,
        grid=(n // tn,),
        in_specs=[pl.BlockSpec((ADA_ROWS, D_MODEL), lambda j: (0, 0)),
                  pl.BlockSpec((D_MODEL, tn), lambda j: (0, j)),
                  pl.BlockSpec((1, tn), lambda j: (0, j))],
        out_specs=pl.BlockSpec((ADA_ROWS, tn), lambda j: (0, j)),
        out_shape=jax.ShapeDtypeStruct((ADA_ROWS, n), F32),
        compiler_params=_params(),
        name="adaln",
    )(cs, w, b.reshape(1, n))


def _mod_specs(latent, cols):
    rb = 0 if latent else 1
    return [pl.BlockSpec((8, D_MODEL), lambda i, c=c: (rb, c)) for c in cols]


def _store_heads(o_ref, y):
    for hd in range(o_ref.shape[1]):
        o_ref[:, hd, :] = y[:, hd * HEAD_DIM:(hd + 1) * HEAD_DIM]


def _in0_kernel(*refs, latent, tm, seq):
    if latent:
        (x_ref, shift_ref, scale_ref, g_ref, w_ref, qn_ref, kn_ref, cos_ref, slo_ref, shi_ref,
         qa_ref, ka_ref, va_ref, qb_ref, kb_ref, vb_ref) = refs
    else:
        (x_ref, shift_ref, scale_ref, g_ref, w_ref, qn_ref, kn_ref,
         qa_ref, ka_ref, va_ref, qb_ref, kb_ref, vb_ref,
         kaf_ref, vaf_ref, kbf_ref, vbf_ref) = refs
    row = (pl.program_id(0) * tm) // seq if latent else 0
    h = _mod_norm(x_ref, g_ref, shift_ref, scale_ref, row)

    def head_norm(y, g_ref_):
        y = _rms(y, g_ref_[...])
        if latent:
            y = _rope(y, cos_ref[...], slo_ref[...], shi_ref[...], HEAD_DIM // 4)
        return y

    c_ka, c_va, c_qb = A_WIDTH, A_WIDTH + A_KV_WIDTH, A_WIDTH + 2 * A_KV_WIDTH
    c_kb, c_vb = c_qb + B_WIDTH, c_qb + 2 * B_WIDTH
    y = _dot(h, w_ref[:, c_kb:c_kb + B_WIDTH])
    kb_ref[...] = y.astype(BF16)
    if not latent:
        _store_heads(kbf_ref, y)
    y = _dot(h, w_ref[:, c_vb:c_vb + B_WIDTH])
    vb_ref[...] = y.astype(BF16)
    if not latent:
        _store_heads(vbf_ref, y)
    y = _dot(h, w_ref[:, c_ka:c_ka + A_KV_WIDTH])
    for hd in range(A_KV_HEADS):
        sl = slice(hd * HEAD_DIM, (hd + 1) * HEAD_DIM)
        k = head_norm(y[:, sl], kn_ref)
        ka_ref[:, sl] = k.astype(BF16)
        if not latent:
            kaf_ref[:, hd, :] = k
    y = _dot(h, w_ref[:, c_va:c_va + A_KV_WIDTH])
    va_ref[...] = y.astype(BF16)
    if not latent:
        _store_heads(vaf_ref, y)
    y = _dot(h, w_ref[:, :A_WIDTH])
    for hd in range(A_Q_HEADS):
        sl = slice(hd * HEAD_DIM, (hd + 1) * HEAD_DIM)
        qa_ref[:, sl] = (head_norm(y[:, sl], qn_ref) * Q_SCALE_L0).astype(BF16)
    y = _dot(h, w_ref[:, c_qb:c_qb + B_WIDTH])
    qb_ref[...] = (y * Q_SCALE_L0).astype(BF16)


def _in0(x, mod, g, w, qn, kn, rope_tabs, *, latent, seq, tm=512):
    m = x.shape[0]
    row_spec = lambda n: pl.BlockSpec((tm, n), lambda i: (i, 0))
    in_specs = [row_spec(D_MODEL)] + _mod_specs(latent, (0, 1)) + [
        _resident((1, D_MODEL)), _resident((D_MODEL, L0_QKV)),
        _resident((1, HEAD_DIM)), _resident((1, HEAD_DIM))]
    args = [x, mod, mod, g, w, qn, kn]
    if latent:
        nblk = seq // tm
        in_specs += [pl.BlockSpec((tm, LANES), lambda i: (i % nblk, 0))] * 3
        args += list(rope_tabs)
    widths = [A_WIDTH, A_KV_WIDTH, A_KV_WIDTH, B_WIDTH, B_WIDTH, B_WIDTH]
    out_specs = [row_spec(n) for n in widths]
    out_shape = [jax.ShapeDtypeStruct((m, n), BF16) for n in widths]
    if not latent:
        for nh in (A_KV_HEADS, A_KV_HEADS, B_HEADS, B_HEADS):
            out_specs.append(pl.BlockSpec((tm, nh, HEAD_DIM), lambda i: (i, 0, 0)))
            out_shape.append(jax.ShapeDtypeStruct((m, nh, HEAD_DIM), F32))
    return pl.pallas_call(
        functools.partial(_in0_kernel, latent=latent, tm=tm, seq=seq),
        grid=(m // tm,), in_specs=in_specs, out_specs=out_specs, out_shape=out_shape,
        compiler_params=_params(), name="in0_lat" if latent else "in0_ctx",
    )(*args)


def _attn_kernel(*refs, n_src, heads, group, sub):
    q_ref = refs[0]
    kv_refs = refs[1:1 + 2 * n_src]
    o_ref = refs[1 + 2 * n_src]
    for j in range(heads // group):
        sl_kv = slice(j * HEAD_DIM, (j + 1) * HEAD_DIM)
        ks = [kv_refs[2 * s][:, sl_kv].astype(BF16) for s in range(n_src)]
        vs = [_with_ones(kv_refs[2 * s + 1][:, sl_kv]) for s in range(n_src)]
        sls = [slice((j * group + gi) * HEAD_DIM, (j * group + gi + 1) * HEAD_DIM)
               for gi in range(group)]
        for r0 in range(0, q_ref.shape[0], sub):
            q = jnp.concatenate([q_ref[r0:r0 + sub, sl] for sl in sls], axis=0)
            o = _softmax_pv([_dot_nt(q, k) for k in ks], vs).astype(BF16)
            for gi, sl in enumerate(sls):
                o_ref[r0:r0 + sub, sl] = o[gi * sub:(gi + 1) * sub]


def _attn(q, srcs, *, batch, seq, heads, group, tq, name):
    kvw = (heads // group) * HEAD_DIM
    nq = seq // tq
    in_specs = [pl.BlockSpec((tq, heads * HEAD_DIM), lambda b, i: (b * nq + i, 0))]
    args = [q]
    for k, v, t in srcs:
        for a in (k, v):
            if a.ndim == 3:
                in_specs.append(pl.BlockSpec((None, t, kvw), lambda b, i: (b, 0, 0)))
            else:
                in_specs.append(pl.BlockSpec((t, kvw), lambda b, i: (b, 0)))
            args.append(a)
    return pl.pallas_call(
        functools.partial(_attn_kernel, n_src=len(srcs), heads=heads, group=group,
                          sub=min(tq, 256)),
        grid=(batch, nq), in_specs=in_specs,
        out_specs=pl.BlockSpec((tq, heads * HEAD_DIM), lambda b, i: (b * nq + i, 0)),
        out_shape=jax.ShapeDtypeStruct((batch * seq, heads * HEAD_DIM), BF16),
        compiler_params=_params(), name=name,
    )(*args)


def _na_geometry(rows):
    wh = min(NA_WIN_H, rows)
    row_start = np.clip(np.arange(rows) - wh // 2, 0, rows - wh)
    return wh, row_start


def _na_bias_kernel(rpb_ref, o_ref, *, rows):
    hd = pl.program_id(0)
    wh, row_start = _na_geometry(rows)
    nx = 2 * NA_WIN_W - 1
    qc = lax.broadcasted_iota(jnp.int32, (GRID_W, GRID_W), 0)
    kc = lax.broadcasted_iota(jnp.int32, (GRID_W, GRID_W), 1)
    col_start = jnp.clip(qc - NA_WIN_W // 2, 0, GRID_W - NA_WIN_W)
    col_ok = (kc >= col_start) & (kc < col_start + NA_WIN_W)
    dx = kc - qc + NA_WIN_W - 1
    base = hd * ((2 * NA_WIN_H - 1) * nx)
    tiles = {}
    for dy in range(-(wh - 1), wh):
        t = jnp.zeros((GRID_W, GRID_W), F32)
        for j in range(nx):
            t = jnp.where(dx == j, rpb_ref[base + (dy + NA_WIN_H - 1) * nx + j], t)
        tiles[dy] = jnp.where(col_ok, t * LOG2E, NEG_INF)
    o_ref[...] = jnp.full(o_ref.shape, NEG_INF, F32)
    for qr in range(rows):
        for kr in range(int(row_start[qr]), int(row_start[qr]) + wh):
            o_ref[0, qr * GRID_W:(qr + 1) * GRID_W, kr * GRID_W:(kr + 1) * GRID_W] = tiles[kr - qr]


def _na_bias(rpb, seq):
    rows = seq // GRID_W
    return pl.pallas_call(
        functools.partial(_na_bias_kernel, rows=rows),
        grid=(B_HEADS,),
        in_specs=[pl.BlockSpec(memory_space=pltpu.SMEM)],
        out_specs=pl.BlockSpec((1, seq, seq), lambda h: (h, 0, 0)),
        out_shape=jax.ShapeDtypeStruct((B_HEADS, seq, seq), F32),
        compiler_params=_params(), name="na_bias",
    )(rpb.reshape(-1))


def _na_windows(rows, tq):
    wh, row_start = _na_geometry(rows)
    rq = tq // GRID_W
    wins = []
    for qb in range(rows // rq):
        lo = int(row_start[qb * rq]) * GRID_W
        hi = (int(row_start[qb * rq + rq - 1]) + wh) * GRID_W
        wins.append((lo // LANES * LANES, -(-hi // LANES) * LANES))
    return wins


def _na_kernel(q_ref, k_ref, v_ref, kc_ref, vc_ref, bias_ref, o_ref, *, seq, tq, wins):
    for b in range(q_ref.shape[0] // seq):
        kc = kc_ref[b].astype(BF16)
        vc = _with_ones(vc_ref[b])
        v = _with_ones(v_ref[b * seq:(b + 1) * seq, :])
        for qb, (lo, hi) in enumerate(wins):
            rs = slice(qb * tq, (qb + 1) * tq)
            rows = slice(b * seq + qb * tq, b * seq + (qb + 1) * tq)
            q = q_ref[rows, :]
            s_loc = _dot_nt(q, k_ref[b * seq + lo:b * seq + hi, :]) + bias_ref[0, rs, lo:hi]
            o_ref[rows, :] = _softmax_pv([s_loc, _dot_nt(q, kc)], [v[lo:hi], vc]).astype(BF16)


def _na(q, k, v, kc, vc, bias, *, batch, seq, tq=512, nb=4):
    past = kc.shape[1]
    wins = _na_windows(seq // GRID_W, tq)
    tok = pl.BlockSpec((nb * seq, HEAD_DIM), lambda h, b: (b, h))
    cache = pl.BlockSpec((nb, past, HEAD_DIM), lambda h, b: (b, 0, h))
    return pl.pallas_call(
        functools.partial(_na_kernel, seq=seq, tq=tq, wins=wins),
        grid=(B_HEADS, batch // nb),
        in_specs=[tok, tok, tok, cache, cache,
                  pl.BlockSpec((1, seq, seq), lambda h, b: (h, 0, 0))],
        out_specs=tok,
        out_shape=jax.ShapeDtypeStruct((batch * seq, B_WIDTH), BF16),
        compiler_params=_params(), name="na",
    )(q, k, v, kc, vc, bias)


def _out_kernel(*refs, latent, tm, seq, n_o, final):
    x_ref, shift_ref, scale_ref, gate_ref, g_ref = refs[:5]
    o_refs = refs[5:5 + n_o]
    wg_ref, wo_ref = refs[5 + n_o:7 + n_o]
    fn_ref = refs[7 + n_o] if final else None
    y_ref = refs[-1]
    row = (pl.program_id(0) * tm) // seq if latent else 0
    h = _mod_norm(x_ref, g_ref, shift_ref, scale_ref, row)
    gt = _dot(h, wg_ref[...])
    o = jnp.concatenate([r[...] for r in o_refs], axis=-1).astype(F32)
    t = (o * _silu(gt)).astype(BF16)
    y = x_ref[...] + gate_ref[pl.ds(row, 1), :] * _dot(t, wo_ref[...])
    if final:
        y = _rms(y, fn_ref[...])
    y_ref[...] = y


def _out(x, mod, g, os_, wg, wo, fn, *, latent, seq, name, tm=512):
    m = x.shape[0]
    width = wg.shape[1]
    final = fn is not None
    in_specs = [pl.BlockSpec((tm, D_MODEL), lambda i: (i, 0))] + _mod_specs(latent, (0, 1, 2))
    in_specs += [_resident((1, D_MODEL))]
    in_specs += [pl.BlockSpec((tm, o.shape[1]), lambda i: (i, 0)) for o in os_]
    in_specs += [_resident((D_MODEL, width)), _resident((width, D_MODEL))]
    args = [x, mod, mod, mod, g, *os_, wg, wo]
    if final:
        in_specs.append(_resident((1, D_MODEL)))
        args.append(fn)
    return pl.pallas_call(
        functools.partial(_out_kernel, latent=latent, tm=tm, seq=seq, n_o=len(os_), final=final),
        grid=(m // tm,), in_specs=in_specs,
        out_specs=pl.BlockSpec((tm, D_MODEL), lambda i: (i, 0)),
        out_shape=jax.ShapeDtypeStruct((m, D_MODEL), F32),
        compiler_params=_params(), name=name,
    )(*args)


def _in1_kernel(*refs, latent, tm, seq):
    if latent:
        (x_ref, shift_ref, scale_ref, g_ref, w_ref, qan_ref, kvn_ref, wq_ref,
         cos_ref, slo_ref, shi_ref, q_ref, ckv_ref, kr_ref) = refs
    else:
        (x_ref, shift_ref, scale_ref, g_ref, w_ref, qan_ref, kvn_ref, wq_ref,
         q_ref, ckv_ref, kr_ref) = refs
    row = (pl.program_id(0) * tm) // seq if latent else 0
    h = _mod_norm(x_ref, g_ref, shift_ref, scale_ref, row)
    z = _dot(h, w_ref[:, :L1_LAT_PAD])
    q_lat = _rms(z[:, :Q_LORA], qan_ref[...]).astype(BF16)
    q = _dot(q_lat, wq_ref[...])
    kr = z[:, Q_LORA + KV_LORA:]
    if latent:
        cos, slo, shi = cos_ref[...], slo_ref[...], shi_ref[...]
        kr = _rope(kr, cos, slo, shi, QK_ROPE // 4)
    for j in range(MLA_HEADS * MLA_QK // LANES):
        sl = slice(j * LANES, (j + 1) * LANES)
        qj = q[:, sl]
        if latent and j % 2 == 1:
            qj = _rope(qj, cos, slo, shi, QK_ROPE // 4)
        q_ref[:, sl] = (qj * Q_SCALE_L1).astype(BF16)
    ckv_ref[...] = _rms(z[:, Q_LORA:Q_LORA + KV_LORA], kvn_ref[...]).astype(ckv_ref.dtype)
    kr_ref[...] = kr[:, :QK_ROPE].astype(kr_ref.dtype)


def _in1(x, mod, g, w, qan, kvn, wq, rope_tabs, *, latent, seq, tm=512):
    m = x.shape[0]
    row_spec = lambda n: pl.BlockSpec((tm, n), lambda i: (i, 0))
    in_specs = [row_spec(D_MODEL)] + _mod_specs(latent, (0, 1)) + [
        _resident((1, D_MODEL)), _resident(w.shape),
        _resident((1, Q_LORA)), _resident((1, KV_LORA)),
        _resident((Q_LORA, MLA_HEADS * MLA_QK))]
    args = [x, mod, mod, g, w, qan, kvn, wq]
    if latent:
        nblk = seq // tm
        in_specs += [pl.BlockSpec((tm, LANES), lambda i: (i % nblk, 0))] * 3
        args += list(rope_tabs)
    lat_dt = BF16 if latent else F32
    widths = [(MLA_HEADS * MLA_QK, BF16), (KV_LORA, lat_dt), (QK_ROPE, lat_dt)]
    return pl.pallas_call(
        functools.partial(_in1_kernel, latent=latent, tm=tm, seq=seq),
        grid=(m // tm,), in_specs=in_specs,
        out_specs=[row_spec(n) for n, _ in widths],
        out_shape=[jax.ShapeDtypeStruct((m, n), dt) for n, dt in widths],
        compiler_params=_params(), name="in1_lat" if latent else "in1_ctx",
    )(*args)


def _mla_kernel(*refs, n_src, hb, nbs, seq):
    q_ref = refs[0]
    src = refs[1:1 + 2 * n_src]
    w_ref = refs[1 + 2 * n_src]
    o_ref = refs[2 + 2 * n_src]
    ckv = [src[2 * s][...].astype(BF16) for s in range(n_src)]
    krp = []
    for s in range(n_src):
        kr = src[2 * s + 1][...].astype(F32)
        pad = jnp.zeros((kr.shape[0], LANES - QK_ROPE), F32)
        krp.append(jnp.concatenate([kr, pad], axis=1).astype(BF16))
    kvw = QK_NOPE + V_DIM
    for hd in range(hb):
        w = w_ref[:, hd * kvw:(hd + 1) * kvw]
        kv = [_dot(c, w).astype(BF16) for c in ckv]
        for b in range(nbs):
            rq = slice(b * seq, (b + 1) * seq)
            kvb = [kv_s[b * (kv_s.shape[0] // nbs):(b + 1) * (kv_s.shape[0] // nbs)] for kv_s in kv]
            krb = [kr_s[b * (kr_s.shape[0] // nbs):(b + 1) * (kr_s.shape[0] // nbs)] for kr_s in krp]
            kcat = [jnp.concatenate([kv_s[:, :QK_NOPE], kr_s], axis=1)
                    for kv_s, kr_s in zip(kvb, krb)]
            q = q_ref[rq, hd * MLA_QK:(hd + 1) * MLA_QK]
            o = _softmax_pv([_dot_nt(q, kc) for kc in kcat],
                            [_with_ones(kv_s[:, QK_NOPE:]) for kv_s in kvb])
            o_ref[rq, hd * V_DIM:(hd + 1) * V_DIM] = o.astype(BF16)


def _mla(q, srcs, w_kv, *, batch, seq, hb, name, nbs=1):
    kvw = QK_NOPE + V_DIM
    in_specs = [pl.BlockSpec((nbs * seq, hb * MLA_QK), lambda b, j: (b, j))]
    args = [q]
    for ckv, kr, t in srcs:
        for a in (ckv, kr):
            wd = a.shape[-1]
            if a.ndim == 3:
                assert nbs == 1
                in_specs.append(pl.BlockSpec((None, t, wd), lambda b, j: (b, 0, 0)))
            else:
                in_specs.append(pl.BlockSpec((nbs * t, wd), lambda b, j: (b, 0)))
            args.append(a)
    in_specs.append(pl.BlockSpec((KV_LORA, hb * kvw), lambda b, j: (0, j)))
    args.append(w_kv)
    return pl.pallas_call(
        functools.partial(_mla_kernel, n_src=len(srcs), hb=hb, nbs=nbs, seq=seq),
        grid=(batch // nbs, MLA_HEADS // hb), in_specs=in_specs,
        out_specs=pl.BlockSpec((nbs * seq, hb * V_DIM), lambda b, j: (b, j)),
        out_shape=jax.ShapeDtypeStruct((batch * seq, L1_WIDTH), BF16),
        compiler_params=_params(), name=name,
    )(*args)


def _rope_tables(seq, d):
    a = d // 2
    t = jnp.arange(seq)
    row = (t // GRID_W).astype(F32)
    col = (t % GRID_W).astype(F32)
    inv = 1.0 / (ROPE_THETA ** (jnp.arange(a // 2, dtype=F32) * 2.0 / a))
    ang = jnp.concatenate([row[:, None] * inv] * 2 + [col[:, None] * inv] * 2, axis=-1)
    ang = jnp.tile(ang, (1, LANES // d))
    lower = jnp.tile((jnp.arange(d) % a) < a // 2, LANES // d)[None, :]
    cos, sin = jnp.cos(ang), jnp.sin(ang)
    zero = jnp.zeros_like(sin)
    return cos, jnp.where(lower, -sin, zero), jnp.where(lower, zero, sin)


def kernel(x_prompt, x_sample, cache_l0_a_k, cache_l0_a_v, cache_l0_b_k, cache_l0_b_v, cache_l1_ckv, cache_l1_krope, c, c_ctx, l0_norm, l0_ada_w, l0_ada_b, l0_w_in, l0_q_norm, l0_k_norm, l0_rpb, l0_w_out, l1_norm, l1_ada_w, l1_ada_b, l1_w_in, l1_q_a_norm, l1_w_q_b, l1_kv_a_norm, l1_w_kv_b, l1_w_out, final_norm):
    batch, seq_c, _ = x_prompt.shape
    dec_batch, seq_l, _ = x_sample.shape
    past = cache_l0_a_k.shape[1]

    w0_qkv, w0_gate, w0_out, w1_all, w1_gate, wq, w_kv, w1_out = _prep_weights(
        l0_w_in, l0_w_out, l1_w_in, l1_w_q_b, l1_w_kv_b, l1_w_out)

    row2 = lambda v: v.reshape(1, -1)
    cs = jnp.zeros((ADA_ROWS, D_MODEL), F32).at[:dec_batch].set(c).at[8].set(c_ctx)
    mod0 = _adaln(cs, l0_ada_w, l0_ada_b)
    mod1 = _adaln(cs, l1_ada_w, l1_ada_b)

    rope_a = _rope_tables(seq_l, HEAD_DIM)
    rope_c = _rope_tables(seq_l, QK_ROPE)

    xc = x_prompt.reshape(batch * seq_c, D_MODEL)
    qa, ka, va, qb, kb, vb, ka_f, va_f, kb_f, vb_f = _in0(
        xc, mod0, row2(l0_norm), w0_qkv, row2(l0_q_norm), row2(l0_k_norm), None,
        latent=False, seq=seq_c)
    oa = _attn(qa, [(ka, va, seq_c)], batch=batch, seq=seq_c, heads=A_Q_HEADS,
               group=A_Q_HEADS // A_KV_HEADS, tq=seq_c, name="attn_a_ctx")
    ob = _attn(qb, [(kb, vb, seq_c)], batch=batch, seq=seq_c, heads=B_HEADS, group=1,
               tq=seq_c, name="attn_b_ctx")
    xc = _out(xc, mod0, row2(l0_norm), [oa, ob], w0_gate, w0_out, None,
              latent=False, seq=seq_c, name="out0_ctx")
    q1, ckv_f, kr_f = _in1(xc, mod1, row2(l1_norm), w1_all, row2(l1_q_a_norm),
                           row2(l1_kv_a_norm), wq, None, latent=False, seq=seq_c)
    oc = _mla(q1, [(ckv_f, kr_f, seq_c)], w_kv, batch=batch, seq=seq_c,
              hb=MLA_HEADS, nbs=2, name="mla_ctx")
    y_prompt = _out(xc, mod1, row2(l1_norm), [oc], w1_gate, w1_out, row2(final_norm),
                    latent=False, seq=seq_c, name="out1_ctx")

    xl = x_sample.reshape(dec_batch * seq_l, D_MODEL)
    qa, ka, va, qb, kb, vb = _in0(
        xl, mod0, row2(l0_norm), w0_qkv, row2(l0_q_norm), row2(l0_k_norm), rope_a,
        latent=True, seq=seq_l)
    oa = _attn(qa, [(cache_l0_a_k.reshape(dec_batch, past, A_KV_WIDTH),
                     cache_l0_a_v.reshape(dec_batch, past, A_KV_WIDTH), past),
                    (ka, va, seq_l)],
               batch=dec_batch, seq=seq_l, heads=A_Q_HEADS, group=A_Q_HEADS // A_KV_HEADS,
               tq=seq_l, name="attn_a_lat")
    bias = _na_bias(l0_rpb, seq_l)
    ob = _na(qb, kb, vb, cache_l0_b_k.reshape(dec_batch, past, B_WIDTH),
             cache_l0_b_v.reshape(dec_batch, past, B_WIDTH), bias, batch=dec_batch, seq=seq_l)
    xl = _out(xl, mod0, row2(l0_norm), [oa, ob], w0_gate, w0_out, None,
              latent=True, seq=seq_l, name="out0_lat")
    q1, ckv, kr = _in1(xl, mod1, row2(l1_norm), w1_all, row2(l1_q_a_norm),
                       row2(l1_kv_a_norm), wq, rope_c, latent=True, seq=seq_l)
    oc = _mla(q1, [(cache_l1_ckv, cache_l1_krope, past), (ckv, kr, seq_l)], w_kv,
              batch=dec_batch, seq=seq_l, hb=8, name="mla_lat")
    y_sample = _out(xl, mod1, row2(l1_norm), [oc], w1_gate, w1_out, row2(final_norm),
                    latent=True, seq=seq_l, name="out1_lat")

    return (y_prompt.reshape(batch, seq_c, D_MODEL),
            y_sample.reshape(dec_batch, seq_l, D_MODEL),
            ka_f.reshape(batch, seq_c, A_KV_HEADS, HEAD_DIM),
            va_f.reshape(batch, seq_c, A_KV_HEADS, HEAD_DIM),
            kb_f.reshape(batch, seq_c, B_HEADS, HEAD_DIM),
            vb_f.reshape(batch, seq_c, B_HEADS, HEAD_DIM),
            ckv_f.reshape(batch, seq_c, KV_LORA),
            kr_f.reshape(batch, seq_c, QK_ROPE))
```

```python
import functools

import jax
import jax.numpy as jnp
import numpy as np
from jax import lax
from jax.experimental import pallas as pl
from jax.experimental.pallas import tpu as pltpu

LANES = 128
VMEM_LIMIT = 56 * 1024 * 1024
ADA_ROWS = 16
F32 = jnp.float32
BF16 = jnp.bfloat16

D_MODEL = 2048
GRID_W = 64
HEAD_DIM = 128
A_Q_HEADS = 8
A_KV_HEADS = 2
B_HEADS = 8
A_WIDTH = A_Q_HEADS * HEAD_DIM
A_KV_WIDTH = A_KV_HEADS * HEAD_DIM
B_WIDTH = B_HEADS * HEAD_DIM
L0_WIDTH = A_WIDTH + B_WIDTH
L0_QKV = A_WIDTH + 2 * A_KV_WIDTH + 3 * B_WIDTH
NA_WIN_H = 8
NA_WIN_W = 16
MLA_HEADS = 16
Q_LORA = 512
KV_LORA = 512
QK_NOPE = 128
QK_ROPE = 64
V_DIM = 128
L1_WIDTH = MLA_HEADS * V_DIM
L1_LAT = Q_LORA + KV_LORA + QK_ROPE
L1_LAT_PAD = L1_LAT + (LANES - QK_ROPE)
MLA_QK = 2 * LANES
ROPE_THETA = 10000.0
EPS = 1e-6
NEG_INF = -1e30
LOG2E = 1.4426950408889634
Q_SCALE_L0 = HEAD_DIM ** -0.5 * LOG2E
Q_SCALE_L1 = (QK_NOPE + QK_ROPE) ** -0.5 * LOG2E


def _params(vmem=VMEM_LIMIT):
    return pltpu.CompilerParams(vmem_limit_bytes=vmem)


def _resident(shape):
    nd = len(shape)
    return pl.BlockSpec(shape, lambda *_: (0,) * nd, pipeline_mode=pl.Buffered(1))


def _dot(a, b):
    return jnp.dot(a, b, preferred_element_type=F32)


def _dot_nt(a, b):
    return lax.dot_general(a, b, (((1,), (1,)), ((), ())), preferred_element_type=F32)


def _rms(x, g):
    return x * lax.rsqrt(jnp.mean(x * x, axis=-1, keepdims=True) + EPS) * g


def _silu(x):
    return x * jax.nn.sigmoid(x)


def _rope(x, cos, sin_lo, sin_hi, half):
    return (x * cos + pltpu.roll(x, half, 1) * sin_hi
            + pltpu.roll(x, LANES - half, 1) * sin_lo)


def _mod_norm(x_ref, g_ref, shift_ref, scale_ref, row):
    x = x_ref[...]
    h = _rms(x, g_ref[...]) * (1.0 + scale_ref[pl.ds(row, 1), :]) + shift_ref[pl.ds(row, 1), :]
    return h.astype(BF16)


def _with_ones(v):
    v = v.astype(BF16)
    return jnp.concatenate([v, jnp.ones_like(v)], axis=1)


def _softmax_pv(ss, vs):
    mx = ss[0].max(axis=-1, keepdims=True)
    for s in ss[1:]:
        mx = jnp.maximum(mx, s.max(axis=-1, keepdims=True))
    acc = _dot(jnp.exp2(ss[0] - mx).astype(BF16), vs[0])
    for s, v in zip(ss[1:], vs[1:]):
        acc = acc + _dot(jnp.exp2(s - mx).astype(BF16), v)
    return acc[:, :LANES] / acc[:, LANES:]


def _run_lagged(units, lag, finish):
    pending = []
    for unit in units:
        pending.append(unit)
        while len(pending) > lag:
            finish(pending.pop(0))
    for unit in pending:
        finish(unit)


WEIGHT_ROW_BLOCKS = 16


def _prep_kernel(w0_ref, w0o_ref, w1_ref, wq_ref, wkv_ref, w1o_ref,
                 qkv_ref, g0_ref, o0_ref, lat_ref, g1_ref, q_ref, kv_ref, o1_ref):
    qkv_ref[...] = w0_ref[:, :L0_QKV].astype(BF16)
    g0_ref[...] = w0_ref[:, L0_QKV:].astype(BF16)
    o0_ref[...] = w0o_ref[...].astype(BF16)
    lat_ref[...] = w1_ref[:, :L1_LAT_PAD].astype(BF16)
    g1_ref[...] = w1_ref[:, L1_LAT:].astype(BF16)
    qw = QK_NOPE + QK_ROPE
    pad = jnp.zeros((wq_ref.shape[0], MLA_QK - qw), F32)
    parts = []
    for hd in range(MLA_HEADS):
        parts += [wq_ref[:, hd * qw:(hd + 1) * qw], pad]
    q_ref[...] = jnp.concatenate(parts, axis=1).astype(BF16)
    kv_ref[...] = wkv_ref[...].astype(BF16)
    o1_ref[...] = w1o_ref[...].astype(BF16)


def _prep_weights(w0, w0o, w1, wq, wkv, w1o):
    nb = WEIGHT_ROW_BLOCKS
    ins = [w0, w0o, w1, wq, wkv, w1o]
    out_cols = [(D_MODEL, L0_QKV), (D_MODEL, L0_WIDTH), (L0_WIDTH, D_MODEL),
                (D_MODEL, L1_LAT_PAD), (D_MODEL, L1_WIDTH), (Q_LORA, MLA_HEADS * MLA_QK),
                (KV_LORA, MLA_HEADS * (QK_NOPE + V_DIM)), (L1_WIDTH, D_MODEL)]
    spec = lambda r, c: pl.BlockSpec((r // nb, c), lambda i: (i, 0))
    return pl.pallas_call(
        _prep_kernel, grid=(nb,),
        in_specs=[spec(*a.shape) for a in ins],
        out_specs=[spec(r, c) for r, c in out_cols],
        out_shape=[jax.ShapeDtypeStruct(rc, BF16) for rc in out_cols],
        compiler_params=_params(), name="prep_weights",
    )(*ins)


def _prep_cache_kernel(*refs):
    n = len(refs) // 2
    for c_ref, o_ref in zip(refs[:n], refs[n:]):
        o_ref[...] = c_ref[...].reshape(o_ref.shape).astype(BF16)


def _prep_caches(caches):
    batch, past = caches[0].shape[:2]
    return pl.pallas_call(
        _prep_cache_kernel, grid=(batch,),
        in_specs=[pl.BlockSpec((None, past) + c.shape[2:], lambda b: (b, 0, 0, 0)) for c in caches],
        out_specs=[pl.BlockSpec((past, c.shape[2] * HEAD_DIM), lambda b: (b, 0)) for c in caches],
        out_shape=[jax.ShapeDtypeStruct((batch * past, c.shape[2] * HEAD_DIM), BF16)
                   for c in caches],
        compiler_params=_params(), name="prep_caches",
    )(*caches)


def _adaln_kernel(c_ref, w_ref, b_ref, o_ref):
    c = c_ref[...]
    o_ref[...] = _dot(_silu(c).astype(BF16), w_ref[...].astype(BF16)) + b_ref[...]


def _adaln(cs, w, b):
    tn = 1024
    n = w.shape[1]
    return pl.pallas_call(
        _adaln_kernel,
        grid=(n // tn,),
        in_specs=[pl.BlockSpec((ADA_ROWS, D_MODEL), lambda j: (0, 0)),
                  pl.BlockSpec((D_MODEL, tn), lambda j: (0, j)),
                  pl.BlockSpec((1, tn), lambda j: (0, j))],
        out_specs=pl.BlockSpec((ADA_ROWS, tn), lambda j: (0, j)),
        out_shape=jax.ShapeDtypeStruct((ADA_ROWS, n), F32),
        compiler_params=_params(),
        name="adaln",
    )(cs, w, b.reshape(1, n))


def _mod_specs(latent, cols):
    rb = 0 if latent else 1
    return [pl.BlockSpec((8, D_MODEL), lambda i, c=c: (rb, c)) for c in cols]


def _store_heads(o_ref, y):
    o_ref[...] = y.reshape(o_ref.shape)


def _in0_kernel(*refs, latent, tm, seq):
    if latent:
        (x_ref, shift_ref, scale_ref, g_ref, w_ref, qn_ref, kn_ref, cos_ref, slo_ref, shi_ref,
         qa_ref, ka_ref, va_ref, qb_ref, kb_ref, vb_ref) = refs
    else:
        (x_ref, shift_ref, scale_ref, g_ref, w_ref, qn_ref, kn_ref,
         qa_ref, ka_ref, va_ref, qb_ref, kb_ref, vb_ref,
         kaf_ref, vaf_ref, kbf_ref, vbf_ref) = refs
    row = (pl.program_id(0) * tm) // seq if latent else 0
    h = _mod_norm(x_ref, g_ref, shift_ref, scale_ref, row)

    def head_norm(y, g_ref_):
        y = _rms(y, g_ref_[...])
        if latent:
            y = _rope(y, cos_ref[...], slo_ref[...], shi_ref[...], HEAD_DIM // 4)
        return y

    c_ka, c_va, c_qb = A_WIDTH, A_WIDTH + A_KV_WIDTH, A_WIDTH + 2 * A_KV_WIDTH
    c_kb, c_vb = c_qb + B_WIDTH, c_qb + 2 * B_WIDTH
    y = _dot(h, w_ref[:, c_kb:c_kb + B_WIDTH])
    kb_ref[...] = y.astype(BF16)
    if not latent:
        _store_heads(kbf_ref, y)
    y = _dot(h, w_ref[:, c_vb:c_vb + B_WIDTH])
    vb_ref[...] = y.astype(BF16)
    if not latent:
        _store_heads(vbf_ref, y)
    y = _dot(h, w_ref[:, c_ka:c_ka + A_KV_WIDTH])
    for hd in range(A_KV_HEADS):
        sl = slice(hd * HEAD_DIM, (hd + 1) * HEAD_DIM)
        k = head_norm(y[:, sl], kn_ref)
        ka_ref[:, sl] = k.astype(BF16)
        if not latent:
            kaf_ref[:, hd, :] = k
    y = _dot(h, w_ref[:, c_va:c_va + A_KV_WIDTH])
    va_ref[...] = y.astype(BF16)
    if not latent:
        _store_heads(vaf_ref, y)
    y = _dot(h, w_ref[:, :A_WIDTH])
    for hd in range(A_Q_HEADS):
        sl = slice(hd * HEAD_DIM, (hd + 1) * HEAD_DIM)
        qa_ref[:, sl] = (head_norm(y[:, sl], qn_ref) * Q_SCALE_L0).astype(BF16)
    y = _dot(h, w_ref[:, c_qb:c_qb + B_WIDTH])
    qb_ref[...] = (y * Q_SCALE_L0).astype(BF16)


def _in0(x, mod, g, w, qn, kn, rope_tabs, *, latent, seq, tm=512):
    m = x.shape[0]
    row_spec = lambda n: pl.BlockSpec((tm, n), lambda i: (i, 0))
    in_specs = [row_spec(D_MODEL)] + _mod_specs(latent, (0, 1)) + [
        _resident((1, D_MODEL)), _resident((D_MODEL, L0_QKV)),
        _resident((1, HEAD_DIM)), _resident((1, HEAD_DIM))]
    args = [x, mod, mod, g, w, qn, kn]
    if latent:
        nblk = seq // tm
        in_specs += [pl.BlockSpec((tm, LANES), lambda i: (i % nblk, 0))] * 3
        args += list(rope_tabs)
    widths = [A_WIDTH, A_KV_WIDTH, A_KV_WIDTH, B_WIDTH, B_WIDTH, B_WIDTH]
    out_specs = [row_spec(n) for n in widths]
    out_shape = [jax.ShapeDtypeStruct((m, n), BF16) for n in widths]
    if not latent:
        for nh in (A_KV_HEADS, A_KV_HEADS, B_HEADS, B_HEADS):
            out_specs.append(pl.BlockSpec((tm, nh, HEAD_DIM), lambda i: (i, 0, 0)))
            out_shape.append(jax.ShapeDtypeStruct((m, nh, HEAD_DIM), F32))
    return pl.pallas_call(
        functools.partial(_in0_kernel, latent=latent, tm=tm, seq=seq),
        grid=(m // tm,), in_specs=in_specs, out_specs=out_specs, out_shape=out_shape,
        compiler_params=_params(), name="in0_lat" if latent else "in0_ctx",
    )(*args)


def _attn_kernel(*refs, n_src, heads, group, sub, lag):
    q_ref = refs[0]
    kv_refs = refs[1:1 + 2 * n_src]
    o_ref = refs[1 + 2 * n_src]

    def scores():
        for j in range(heads // group):
            sl_kv = slice(j * HEAD_DIM, (j + 1) * HEAD_DIM)
            ks = [kv_refs[2 * s][:, sl_kv].astype(BF16) for s in range(n_src)]
            vs = [_with_ones(kv_refs[2 * s + 1][:, sl_kv]) for s in range(n_src)]
            sls = [slice((j * group + gi) * HEAD_DIM, (j * group + gi + 1) * HEAD_DIM)
                   for gi in range(group)]
            for r0 in range(0, q_ref.shape[0], sub):
                q = jnp.concatenate([q_ref[r0:r0 + sub, sl] for sl in sls], axis=0)
                yield r0, sls, [_dot_nt(q, k) for k in ks], vs

    def finish(unit):
        r0, sls, ss, vs = unit
        o = _softmax_pv(ss, vs).astype(BF16)
        for gi, sl in enumerate(sls):
            o_ref[r0:r0 + sub, sl] = o[gi * sub:(gi + 1) * sub]

    _run_lagged(scores(), lag, finish)


def _attn(q, srcs, *, batch, seq, heads, group, tq, name, lag=1):
    kvw = (heads // group) * HEAD_DIM
    nq = seq // tq
    in_specs = [pl.BlockSpec((tq, heads * HEAD_DIM), lambda b, i: (b * nq + i, 0))]
    args = [q]
    for k, v, t in srcs:
        for a in (k, v):
            if a.ndim == 3:
                in_specs.append(pl.BlockSpec((None, t, kvw), lambda b, i: (b, 0, 0)))
            else:
                in_specs.append(pl.BlockSpec((t, kvw), lambda b, i: (b, 0)))
            args.append(a)
    return pl.pallas_call(
        functools.partial(_attn_kernel, n_src=len(srcs), heads=heads, group=group,
                          sub=min(tq, 256), lag=lag),
        grid=(batch, nq), in_specs=in_specs,
        out_specs=pl.BlockSpec((tq, heads * HEAD_DIM), lambda b, i: (b * nq + i, 0)),
        out_shape=jax.ShapeDtypeStruct((batch * seq, heads * HEAD_DIM), BF16),
        compiler_params=_params(), name=name,
    )(*args)


def _na_geometry(rows):
    wh = min(NA_WIN_H, rows)
    row_start = np.clip(np.arange(rows) - wh // 2, 0, rows - wh)
    return wh, row_start


def _na_bias_kernel(rpb_ref, o_ref, *, rows):
    hd = pl.program_id(0)
    wh, row_start = _na_geometry(rows)
    nx = 2 * NA_WIN_W - 1
    qc = lax.broadcasted_iota(jnp.int32, (GRID_W, GRID_W), 0)
    kc = lax.broadcasted_iota(jnp.int32, (GRID_W, GRID_W), 1)
    col_start = jnp.clip(qc - NA_WIN_W // 2, 0, GRID_W - NA_WIN_W)
    col_ok = (kc >= col_start) & (kc < col_start + NA_WIN_W)
    dx = kc - qc + NA_WIN_W - 1
    base = hd * ((2 * NA_WIN_H - 1) * nx)
    tiles = {}
    for dy in range(-(wh - 1), wh):
        t = jnp.zeros((GRID_W, GRID_W), F32)
        for j in range(nx):
            t = jnp.where(dx == j, rpb_ref[base + (dy + NA_WIN_H - 1) * nx + j], t)
        tiles[dy] = jnp.where(col_ok, t * LOG2E, NEG_INF)
    o_ref[...] = jnp.full(o_ref.shape, NEG_INF, F32)
    for qr in range(rows):
        for kr in range(int(row_start[qr]), int(row_start[qr]) + wh):
            o_ref[0, qr * GRID_W:(qr + 1) * GRID_W, kr * GRID_W:(kr + 1) * GRID_W] = tiles[kr - qr]


def _na_bias(rpb, seq):
    rows = seq // GRID_W
    return pl.pallas_call(
        functools.partial(_na_bias_kernel, rows=rows),
        grid=(B_HEADS,),
        in_specs=[pl.BlockSpec(memory_space=pltpu.SMEM)],
        out_specs=pl.BlockSpec((1, seq, seq), lambda h: (h, 0, 0)),
        out_shape=jax.ShapeDtypeStruct((B_HEADS, seq, seq), F32),
        compiler_params=_params(), name="na_bias",
    )(rpb.reshape(-1))


def _na_windows(rows, tq):
    wh, row_start = _na_geometry(rows)
    rq = tq // GRID_W
    wins = []
    for qb in range(rows // rq):
        lo = int(row_start[qb * rq]) * GRID_W
        hi = (int(row_start[qb * rq + rq - 1]) + wh) * GRID_W
        wins.append((lo // LANES * LANES, -(-hi // LANES) * LANES))
    return wins


def _na_kernel(q_ref, k_ref, v_ref, kc_ref, vc_ref, bias_ref, o_ref, *, seq, tq, wins, lag):
    nb = q_ref.shape[0] // seq
    past = kc_ref.shape[0] // nb

    def scores():
        for b in range(nb):
            kc = kc_ref[b * past:(b + 1) * past, :]
            vc = _with_ones(vc_ref[b * past:(b + 1) * past, :])
            v = _with_ones(v_ref[b * seq:(b + 1) * seq, :])
            for qb, (lo, hi) in enumerate(wins):
                rs = slice(qb * tq, (qb + 1) * tq)
                rows = slice(b * seq + qb * tq, b * seq + (qb + 1) * tq)
                q = q_ref[rows, :]
                s_loc = _dot_nt(q, k_ref[b * seq + lo:b * seq + hi, :]) + bias_ref[0, rs, lo:hi]
                yield rows, [s_loc, _dot_nt(q, kc)], [v[lo:hi], vc]

    def finish(unit):
        rows, ss, vs = unit
        o_ref[rows, :] = _softmax_pv(ss, vs).astype(BF16)

    _run_lagged(scores(), lag, finish)


def _na(q, k, v, kc, vc, bias, *, batch, seq, tq=512, nb=4, lag=1):
    past = kc.shape[0] // batch
    wins = _na_windows(seq // GRID_W, tq)
    tok = pl.BlockSpec((nb * seq, HEAD_DIM), lambda h, b: (b, h))
    cache = pl.BlockSpec((nb * past, HEAD_DIM), lambda h, b: (b, h))
    return pl.pallas_call(
        functools.partial(_na_kernel, seq=seq, tq=tq, wins=wins, lag=lag),
        grid=(B_HEADS, batch // nb),
        in_specs=[tok, tok, tok, cache, cache,
                  pl.BlockSpec((1, seq, seq), lambda h, b: (h, 0, 0))],
        out_specs=tok,
        out_shape=jax.ShapeDtypeStruct((batch * seq, B_WIDTH), BF16),
        compiler_params=_params(), name="na",
    )(q, k, v, kc, vc, bias)


def _out_kernel(*refs, latent, tm, seq, n_o, final):
    x_ref, shift_ref, scale_ref, gate_ref, g_ref = refs[:5]
    o_refs = refs[5:5 + n_o]
    wg_ref, wo_ref = refs[5 + n_o:7 + n_o]
    fn_ref = refs[7 + n_o] if final else None
    y_ref = refs[-1]
    row = (pl.program_id(0) * tm) // seq if latent else 0
    h = _mod_norm(x_ref, g_ref, shift_ref, scale_ref, row)
    gt = _dot(h, wg_ref[...])
    o = jnp.concatenate([r[...] for r in o_refs], axis=-1).astype(F32)
    t = (o * _silu(gt)).astype(BF16)
    y = x_ref[...] + gate_ref[pl.ds(row, 1), :] * _dot(t, wo_ref[...])
    if final:
        y = _rms(y, fn_ref[...])
    y_ref[...] = y


def _out(x, mod, g, os_, wg, wo, fn, *, latent, seq, name, tm=512):
    m = x.shape[0]
    width = wg.shape[1]
    final = fn is not None
    in_specs = [pl.BlockSpec((tm, D_MODEL), lambda i: (i, 0))] + _mod_specs(latent, (0, 1, 2))
    in_specs += [_resident((1, D_MODEL))]
    in_specs += [pl.BlockSpec((tm, o.shape[1]), lambda i: (i, 0)) for o in os_]
    in_specs += [_resident((D_MODEL, width)), _resident((width, D_MODEL))]
    args = [x, mod, mod, mod, g, *os_, wg, wo]
    if final:
        in_specs.append(_resident((1, D_MODEL)))
        args.append(fn)
    return pl.pallas_call(
        functools.partial(_out_kernel, latent=latent, tm=tm, seq=seq, n_o=len(os_), final=final),
        grid=(m // tm,), in_specs=in_specs,
        out_specs=pl.BlockSpec((tm, D_MODEL), lambda i: (i, 0)),
        out_shape=jax.ShapeDtypeStruct((m, D_MODEL), F32),
        compiler_params=_params(), name=name,
    )(*args)


def _in1_kernel(*refs, latent, tm, seq):
    if latent:
        (x_ref, shift_ref, scale_ref, g_ref, w_ref, qan_ref, kvn_ref, wq_ref,
         cos_ref, slo_ref, shi_ref, q_ref, ckv_ref, kr_ref) = refs
    else:
        (x_ref, shift_ref, scale_ref, g_ref, w_ref, qan_ref, kvn_ref, wq_ref,
         q_ref, ckv_ref, kr_ref) = refs
    row = (pl.program_id(0) * tm) // seq if latent else 0
    h = _mod_norm(x_ref, g_ref, shift_ref, scale_ref, row)
    z = _dot(h, w_ref[:, :L1_LAT_PAD])
    q_lat = _rms(z[:, :Q_LORA], qan_ref[...]).astype(BF16)
    q = _dot(q_lat, wq_ref[...])
    kr = z[:, Q_LORA + KV_LORA:]
    if latent:
        cos, slo, shi = cos_ref[...], slo_ref[...], shi_ref[...]
        kr = _rope(kr, cos, slo, shi, QK_ROPE // 4)
    for j in range(MLA_HEADS * MLA_QK // LANES):
        sl = slice(j * LANES, (j + 1) * LANES)
        qj = q[:, sl]
        if latent and j % 2 == 1:
            qj = _rope(qj, cos, slo, shi, QK_ROPE // 4)
        q_ref[:, sl] = (qj * Q_SCALE_L1).astype(BF16)
    ckv_ref[...] = _rms(z[:, Q_LORA:Q_LORA + KV_LORA], kvn_ref[...]).astype(ckv_ref.dtype)
    kr_ref[...] = kr[:, :QK_ROPE].astype(kr_ref.dtype)


def _in1(x, mod, g, w, qan, kvn, wq, rope_tabs, *, latent, seq, tm=512):
    m = x.shape[0]
    row_spec = lambda n: pl.BlockSpec((tm, n), lambda i: (i, 0))
    in_specs = [row_spec(D_MODEL)] + _mod_specs(latent, (0, 1)) + [
        _resident((1, D_MODEL)), _resident(w.shape),
        _resident((1, Q_LORA)), _resident((1, KV_LORA)),
        _resident((Q_LORA, MLA_HEADS * MLA_QK))]
    args = [x, mod, mod, g, w, qan, kvn, wq]
    if latent:
        nblk = seq // tm
        in_specs += [pl.BlockSpec((tm, LANES), lambda i: (i % nblk, 0))] * 3
        args += list(rope_tabs)
    lat_dt = BF16 if latent else F32
    widths = [(MLA_HEADS * MLA_QK, BF16), (KV_LORA, lat_dt), (QK_ROPE, lat_dt)]
    return pl.pallas_call(
        functools.partial(_in1_kernel, latent=latent, tm=tm, seq=seq),
        grid=(m // tm,), in_specs=in_specs,
        out_specs=[row_spec(n) for n, _ in widths],
        out_shape=[jax.ShapeDtypeStruct((m, n), dt) for n, dt in widths],
        compiler_params=_params(), name="in1_lat" if latent else "in1_ctx",
    )(*args)


def _mla_kernel(*refs, n_src, hb, nbs, seq, lag):
    q_ref = refs[0]
    src = refs[1:1 + 2 * n_src]
    w_ref = refs[1 + 2 * n_src]
    o_ref = refs[2 + 2 * n_src]
    ckv = [src[2 * s][...].astype(BF16) for s in range(n_src)]
    krp = []
    for s in range(n_src):
        kr = src[2 * s + 1][...].astype(F32)
        pad = jnp.zeros((kr.shape[0], LANES - QK_ROPE), F32)
        krp.append(jnp.concatenate([kr, pad], axis=1).astype(BF16))
    kvw = QK_NOPE + V_DIM

    def scores():
        for hd in range(hb):
            kv = [_dot(c, w_ref[:, hd * kvw:(hd + 1) * kvw]).astype(BF16) for c in ckv]
            for b in range(nbs):
                rq = slice(b * seq, (b + 1) * seq)
                part = lambda a: a[b * (a.shape[0] // nbs):(b + 1) * (a.shape[0] // nbs)]
                kvb = [part(kv_s) for kv_s in kv]
                kcat = [jnp.concatenate([kv_s[:, :QK_NOPE], part(kr_s)], axis=1)
                        for kv_s, kr_s in zip(kvb, krp)]
                q = q_ref[rq, hd * MLA_QK:(hd + 1) * MLA_QK]
                yield (rq, hd, [_dot_nt(q, kc) for kc in kcat],
                       [_with_ones(kv_s[:, QK_NOPE:]) for kv_s in kvb])

    def finish(unit):
        rq, hd, ss, vs = unit
        o_ref[rq, hd * V_DIM:(hd + 1) * V_DIM] = _softmax_pv(ss, vs).astype(BF16)

    _run_lagged(scores(), lag, finish)


def _mla(q, srcs, w_kv, *, batch, seq, hb, name, nbs=1, lag=0):
    kvw = QK_NOPE + V_DIM
    in_specs = [pl.BlockSpec((nbs * seq, hb * MLA_QK), lambda b, j: (b, j))]
    args = [q]
    for ckv, kr, t in srcs:
        for a in (ckv, kr):
            wd = a.shape[-1]
            if a.ndim == 3:
                assert nbs == 1
                in_specs.append(pl.BlockSpec((None, t, wd), lambda b, j: (b, 0, 0)))
            else:
                in_specs.append(pl.BlockSpec((nbs * t, wd), lambda b, j: (b, 0)))
            args.append(a)
    in_specs.append(pl.BlockSpec((KV_LORA, hb * kvw), lambda b, j: (0, j)))
    args.append(w_kv)
    return pl.pallas_call(
        functools.partial(_mla_kernel, n_src=len(srcs), hb=hb, nbs=nbs, seq=seq, lag=lag),
        grid=(batch // nbs, MLA_HEADS // hb), in_specs=in_specs,
        out_specs=pl.BlockSpec((nbs * seq, hb * V_DIM), lambda b, j: (b, j)),
        out_shape=jax.ShapeDtypeStruct((batch * seq, L1_WIDTH), BF16),
        compiler_params=_params(), name=name,
    )(*args)


def _rope_tables(seq, d):
    a = d // 2
    t = jnp.arange(seq)
    row = (t // GRID_W).astype(F32)
    col = (t % GRID_W).astype(F32)
    inv = 1.0 / (ROPE_THETA ** (jnp.arange(a // 2, dtype=F32) * 2.0 / a))
    ang = jnp.concatenate([row[:, None] * inv] * 2 + [col[:, None] * inv] * 2, axis=-1)
    ang = jnp.tile(ang, (1, LANES // d))
    lower = jnp.tile((jnp.arange(d) % a) < a // 2, LANES // d)[None, :]
    cos, sin = jnp.cos(ang), jnp.sin(ang)
    zero = jnp.zeros_like(sin)
    return cos, jnp.where(lower, -sin, zero), jnp.where(lower, zero, sin)


def kernel(x_prompt, x_sample, cache_l0_a_k, cache_l0_a_v, cache_l0_b_k, cache_l0_b_v, cache_l1_ckv, cache_l1_krope, c, c_ctx, l0_norm, l0_ada_w, l0_ada_b, l0_w_in, l0_q_norm, l0_k_norm, l0_rpb, l0_w_out, l1_norm, l1_ada_w, l1_ada_b, l1_w_in, l1_q_a_norm, l1_w_q_b, l1_kv_a_norm, l1_w_kv_b, l1_w_out, final_norm):
    batch, seq_c, _ = x_prompt.shape
    dec_batch, seq_l, _ = x_sample.shape
    past = cache_l0_a_k.shape[1]

    w0_qkv, w0_gate, w0_out, w1_all, w1_gate, wq, w_kv, w1_out = _prep_weights(
        l0_w_in, l0_w_out, l1_w_in, l1_w_q_b, l1_w_kv_b, l1_w_out)

    row2 = lambda v: v.reshape(1, -1)
    cs = jnp.zeros((ADA_ROWS, D_MODEL), F32).at[:dec_batch].set(c).at[8].set(c_ctx)
    mod0 = _adaln(cs, l0_ada_w, l0_ada_b)
    mod1 = _adaln(cs, l1_ada_w, l1_ada_b)

    rope_a = _rope_tables(seq_l, HEAD_DIM)
    rope_c = _rope_tables(seq_l, QK_ROPE)

    xc = x_prompt.reshape(batch * seq_c, D_MODEL)
    qa, ka, va, qb, kb, vb, ka_f, va_f, kb_f, vb_f = _in0(
        xc, mod0, row2(l0_norm), w0_qkv, row2(l0_q_norm), row2(l0_k_norm), None,
        latent=False, seq=seq_c)
    oa = _attn(qa, [(ka, va, seq_c)], batch=batch, seq=seq_c, heads=A_Q_HEADS,
               group=A_Q_HEADS // A_KV_HEADS, tq=seq_c, name="attn_a_ctx")
    ob = _attn(qb, [(kb, vb, seq_c)], batch=batch, seq=seq_c, heads=B_HEADS, group=1,
               tq=seq_c, name="attn_b_ctx")
    xc = _out(xc, mod0, row2(l0_norm), [oa, ob], w0_gate, w0_out, None,
              latent=False, seq=seq_c, name="out0_ctx")
    q1, ckv_f, kr_f = _in1(xc, mod1, row2(l1_norm), w1_all, row2(l1_q_a_norm),
                           row2(l1_kv_a_norm), wq, None, latent=False, seq=seq_c)
    oc = _mla(q1, [(ckv_f, kr_f, seq_c)], w_kv, batch=batch, seq=seq_c,
              hb=MLA_HEADS, nbs=2, lag=4, name="mla_ctx")
    y_prompt = _out(xc, mod1, row2(l1_norm), [oc], w1_gate, w1_out, row2(final_norm),
                    latent=False, seq=seq_c, name="out1_ctx")

    xl = x_sample.reshape(dec_batch * seq_l, D_MODEL)
    qa, ka, va, qb, kb, vb = _in0(
        xl, mod0, row2(l0_norm), w0_qkv, row2(l0_q_norm), row2(l0_k_norm), rope_a,
        latent=True, seq=seq_l)
    ck_a, cv_a, ck_b, cv_b = _prep_caches(
        [cache_l0_a_k, cache_l0_a_v, cache_l0_b_k, cache_l0_b_v])
    oa = _attn(qa, [(ck_a, cv_a, past), (ka, va, seq_l)],
               batch=dec_batch, seq=seq_l, heads=A_Q_HEADS, group=A_Q_HEADS // A_KV_HEADS,
               tq=seq_l, name="attn_a_lat")
    bias = _na_bias(l0_rpb, seq_l)
    ob = _na(qb, kb, vb, ck_b, cv_b, bias, batch=dec_batch, seq=seq_l)
    xl = _out(xl, mod0, row2(l0_norm), [oa, ob], w0_gate, w0_out, None,
              latent=True, seq=seq_l, name="out0_lat")
    q1, ckv, kr = _in1(xl, mod1, row2(l1_norm), w1_all, row2(l1_q_a_norm),
                       row2(l1_kv_a_norm), wq, rope_c, latent=True, seq=seq_l)
    oc = _mla(q1, [(cache_l1_ckv, cache_l1_krope, past), (ckv, kr, seq_l)], w_kv,
              batch=dec_batch, seq=seq_l, hb=8, lag=1, name="mla_lat")
    y_sample = _out(xl, mod1, row2(l1_norm), [oc], w1_gate, w1_out, row2(final_norm),
                    latent=True, seq=seq_l, name="out1_lat")

    return (y_prompt.reshape(batch, seq_c, D_MODEL),
            y_sample.reshape(dec_batch, seq_l, D_MODEL),
            ka_f.reshape(batch, seq_c, A_KV_HEADS, HEAD_DIM),
            va_f.reshape(batch, seq_c, A_KV_HEADS, HEAD_DIM),
            kb_f.reshape(batch, seq_c, B_HEADS, HEAD_DIM),
            vb_f.reshape(batch, seq_c, B_HEADS, HEAD_DIM),
            ckv_f.reshape(batch, seq_c, KV_LORA),
            kr_f.reshape(batch, seq_c, QK_ROPE))
```

```python
import functools
import itertools
import math
from typing import Callable, NamedTuple

import jax
import jax.numpy as jnp
import numpy as np
from jax import lax
from jax.experimental import pallas as pl
from jax.experimental.pallas import tpu as pltpu

LANES = 128
VMEM_LIMIT = 56 * 1024 * 1024
ADA_ROWS = 16
F32 = jnp.float32
BF16 = jnp.bfloat16

D_MODEL = 2048
GRID_W = 64
HEAD_DIM = 128
A_Q_HEADS = 8
A_KV_HEADS = 2
B_HEADS = 8
A_WIDTH = A_Q_HEADS * HEAD_DIM
A_KV_WIDTH = A_KV_HEADS * HEAD_DIM
B_WIDTH = B_HEADS * HEAD_DIM
L0_WIDTH = A_WIDTH + B_WIDTH
L0_QKV = A_WIDTH + 2 * A_KV_WIDTH + 3 * B_WIDTH
NA_WIN_H = 8
NA_WIN_W = 16
MLA_HEADS = 16
Q_LORA = 512
KV_LORA = 512
QK_NOPE = 128
QK_ROPE = 64
V_DIM = 128
L1_WIDTH = MLA_HEADS * V_DIM
L1_LAT = Q_LORA + KV_LORA + QK_ROPE
L1_LAT_PAD = L1_LAT + (LANES - QK_ROPE)
MLA_QK = 2 * LANES
ROPE_THETA = 10000.0
EPS = 1e-6
NEG_INF = -1e30
LOG2E = 1.4426950408889634
Q_SCALE_L0 = HEAD_DIM ** -0.5 * LOG2E
Q_SCALE_L1 = (QK_NOPE + QK_ROPE) ** -0.5 * LOG2E


def _params(vmem=VMEM_LIMIT):
    return pltpu.CompilerParams(vmem_limit_bytes=vmem)


def _resident(shape):
    nd = len(shape)
    return pl.BlockSpec(shape, lambda *_: (0,) * nd, pipeline_mode=pl.Buffered(1))


def _dot(a, b):
    return jnp.dot(a, b, preferred_element_type=F32)


def _dot_nt(a, b):
    return lax.dot_general(a, b, (((1,), (1,)), ((), ())), preferred_element_type=F32)


def _rms(x, g):
    return x * lax.rsqrt(jnp.mean(x * x, axis=-1, keepdims=True) + EPS) * g


def _silu(x):
    return x * jax.nn.sigmoid(x)


def _rope(x, cos, sin_lo, sin_hi, half):
    return (x * cos + pltpu.roll(x, half, 1) * sin_hi
            + pltpu.roll(x, LANES - half, 1) * sin_lo)


def _mod_norm(x_ref, g_ref, shift_ref, scale_ref, row):
    x = x_ref[...]
    h = _rms(x, g_ref[...]) * (1.0 + scale_ref[pl.ds(row, 1), :]) + shift_ref[pl.ds(row, 1), :]
    return h.astype(BF16)


def _with_ones(v):
    v = v.astype(BF16)
    return jnp.concatenate([v, jnp.ones_like(v)], axis=1)


def _softmax_pv(ss, vs):
    mx = ss[0].max(axis=-1, keepdims=True)
    for s in ss[1:]:
        mx = jnp.maximum(mx, s.max(axis=-1, keepdims=True))
    acc = _dot(jnp.exp2(ss[0] - mx).astype(BF16), vs[0])
    for s, v in zip(ss[1:], vs[1:]):
        acc = acc + _dot(jnp.exp2(s - mx).astype(BF16), v)
    return acc[:, :LANES] / acc[:, LANES:]


def _run_lagged(units, lag, finish):
    pending = []
    for unit in units:
        pending.append(unit)
        while len(pending) > lag:
            finish(pending.pop(0))
    for unit in pending:
        finish(unit)


WEIGHT_ROW_BLOCKS = 16


def _prep_l0_kernel(w0_ref, w0o_ref, qkv_ref, g0_ref, o0_ref):
    qkv_ref[...] = w0_ref[:, :L0_QKV].astype(BF16)
    g0_ref[...] = w0_ref[:, L0_QKV:].astype(BF16)
    o0_ref[...] = w0o_ref[...].astype(BF16)


def _prep_l0_weights(w0, w0o):
    nb = WEIGHT_ROW_BLOCKS
    out_cols = [(D_MODEL, L0_QKV), (D_MODEL, L0_WIDTH), (L0_WIDTH, D_MODEL)]
    spec = lambda r, c: pl.BlockSpec((r // nb, c), lambda i: (i, 0))
    return pl.pallas_call(
        _prep_l0_kernel, grid=(nb,),
        in_specs=[spec(*w0.shape), spec(*w0o.shape)],
        out_specs=[spec(r, c) for r, c in out_cols],
        out_shape=[jax.ShapeDtypeStruct(rc, BF16) for rc in out_cols],
        compiler_params=_params(), name="prep_l0_weights",
    )(w0, w0o)


class _SideWork(NamedTuple):
    steps: int
    args: list
    in_specs: list
    out_specs: list
    out_shape: list
    body: Callable

    @property
    def n_in(self):
        return len(self.args)


def _l1_weight_side(w1_t, wq, wkv, w1o, steps):
    g_rows = L1_WIDTH // steps
    g_blk = math.gcd(L1_LAT, g_rows)
    g_n = g_rows // g_blk
    lat_rows = 128
    lat_n = L1_LAT_PAD // lat_rows
    row_blk = lambda a: pl.BlockSpec((a.shape[0] // steps, a.shape[1]), lambda i: (i, 0))
    lat_idx = lambda i: (jnp.minimum(i, lat_n - 1), 0)
    in_specs = [pl.BlockSpec((lat_rows, D_MODEL), lat_idx)]
    in_specs += [pl.BlockSpec((g_blk, D_MODEL), lambda i, k=k: (L1_LAT // g_blk + g_n * i + k, 0))
                 for k in range(g_n)]
    in_specs += [row_blk(wq), row_blk(wkv), row_blk(w1o)]
    args = [w1_t] * (1 + g_n) + [wq, wkv, w1o]
    out_dims = [(L1_LAT_PAD, D_MODEL), (L1_WIDTH, D_MODEL), (Q_LORA, MLA_HEADS * MLA_QK),
                wkv.shape, w1o.shape]
    out_specs = [pl.BlockSpec((lat_rows, D_MODEL), lat_idx)]
    out_specs += [pl.BlockSpec((r // steps, c), lambda i: (i, 0)) for r, c in out_dims[1:]]

    def body(ins, outs):
        lat_in, g_in = ins[0], ins[1:1 + g_n]
        wq_in, wkv_in, w1o_in = ins[1 + g_n:]
        lat_o, g_o, wq_o, wkv_o, w1o_o = outs

        @pl.when(pl.program_id(0) < lat_n)
        def _():
            lat_o[...] = lat_in[...].astype(BF16)

        for k, g_ref in enumerate(g_in):
            g_o[k * g_blk:(k + 1) * g_blk, :] = g_ref[...].astype(BF16)
        qw = QK_NOPE + QK_ROPE
        pad = jnp.zeros((wq_in.shape[0], MLA_QK - qw), F32)
        parts = []
        for hd in range(MLA_HEADS):
            parts += [wq_in[:, hd * qw:(hd + 1) * qw], pad]
        wq_o[...] = jnp.concatenate(parts, axis=1).astype(BF16)
        wkv_o[...] = wkv_in[...].astype(BF16)
        w1o_o[...] = w1o_in[...].astype(BF16)

    return _SideWork(steps, args, in_specs, out_specs,
                     [jax.ShapeDtypeStruct(d, BF16) for d in out_dims], body)


def _adaln_kernel(c_ref, w_ref, b_ref, o_ref):
    c = c_ref[...]
    o_ref[...] = _dot(_silu(c).astype(BF16), w_ref[...].astype(BF16)) + b_ref[...]


def _adaln(cs, w, b):
    tn = 1024
    n = w.shape[1]
    return pl.pallas_call(
        _adaln_kernel,
        grid=(n // tn,),
        in_specs=[pl.BlockSpec((ADA_ROWS, D_MODEL), lambda j: (0, 0)),
                  pl.BlockSpec((D_MODEL, tn), lambda j: (0, j)),
                  pl.BlockSpec((1, tn), lambda j: (0, j))],
        out_specs=pl.BlockSpec((ADA_ROWS, tn), lambda j: (0, j)),
        out_shape=jax.ShapeDtypeStruct((ADA_ROWS, n), F32),
        compiler_params=_params(),
        name="adaln",
    )(cs, w, b.reshape(1, n))


def _mod_specs(latent, cols):
    rb = 0 if latent else 1
    return [pl.BlockSpec((8, D_MODEL), lambda i, c=c: (rb, c)) for c in cols]


def _store_heads(o_ref, y):
    o_ref[...] = y.reshape(o_ref.shape)


def _in0_kernel(*refs, latent, tm, seq):
    if latent:
        (x_ref, shift_ref, scale_ref, g_ref, w_ref, qn_ref, kn_ref, cos_ref, slo_ref, shi_ref,
         qa_ref, ka_ref, va_ref, qb_ref, kb_ref, vb_ref) = refs
    else:
        (x_ref, shift_ref, scale_ref, g_ref, w_ref, qn_ref, kn_ref,
         qa_ref, ka_ref, va_ref, qb_ref, kb_ref, vb_ref,
         kaf_ref, vaf_ref, kbf_ref, vbf_ref) = refs
    row = (pl.program_id(0) * tm) // seq if latent else 0
    h = _mod_norm(x_ref, g_ref, shift_ref, scale_ref, row)

    def head_norm(y, g_ref_):
        y = _rms(y, g_ref_[...])
        if latent:
            y = _rope(y, cos_ref[...], slo_ref[...], shi_ref[...], HEAD_DIM // 4)
        return y

    c_ka, c_va, c_qb = A_WIDTH, A_WIDTH + A_KV_WIDTH, A_WIDTH + 2 * A_KV_WIDTH
    c_kb, c_vb = c_qb + B_WIDTH, c_qb + 2 * B_WIDTH
    y = _dot(h, w_ref[:, c_kb:c_kb + B_WIDTH])
    kb_ref[...] = y.astype(BF16)
    if not latent:
        _store_heads(kbf_ref, y)
    y = _dot(h, w_ref[:, c_vb:c_vb + B_WIDTH])
    vb_ref[...] = y.astype(BF16)
    if not latent:
        _store_heads(vbf_ref, y)
    y = _dot(h, w_ref[:, c_ka:c_ka + A_KV_WIDTH])
    for hd in range(A_KV_HEADS):
        sl = slice(hd * HEAD_DIM, (hd + 1) * HEAD_DIM)
        k = head_norm(y[:, sl], kn_ref)
        ka_ref[:, sl] = k.astype(BF16)
        if not latent:
            kaf_ref[:, hd, :] = k
    y = _dot(h, w_ref[:, c_va:c_va + A_KV_WIDTH])
    va_ref[...] = y.astype(BF16)
    if not latent:
        _store_heads(vaf_ref, y)
    y = _dot(h, w_ref[:, :A_WIDTH])
    for hd in range(A_Q_HEADS):
        sl = slice(hd * HEAD_DIM, (hd + 1) * HEAD_DIM)
        qa_ref[:, sl] = (head_norm(y[:, sl], qn_ref) * Q_SCALE_L0).astype(BF16)
    y = _dot(h, w_ref[:, c_qb:c_qb + B_WIDTH])
    qb_ref[...] = (y * Q_SCALE_L0).astype(BF16)


def _in0(x, mod, g, w, qn, kn, rope_tabs, *, latent, seq, tm=512):
    m = x.shape[0]
    row_spec = lambda n: pl.BlockSpec((tm, n), lambda i: (i, 0))
    in_specs = [row_spec(D_MODEL)] + _mod_specs(latent, (0, 1)) + [
        _resident((1, D_MODEL)), _resident((D_MODEL, L0_QKV)),
        _resident((1, HEAD_DIM)), _resident((1, HEAD_DIM))]
    args = [x, mod, mod, g, w, qn, kn]
    if latent:
        nblk = seq // tm
        in_specs += [pl.BlockSpec((tm, LANES), lambda i: (i % nblk, 0))] * 3
        args += list(rope_tabs)
    widths = [A_WIDTH, A_KV_WIDTH, A_KV_WIDTH, B_WIDTH, B_WIDTH, B_WIDTH]
    out_specs = [row_spec(n) for n in widths]
    out_shape = [jax.ShapeDtypeStruct((m, n), BF16) for n in widths]
    if not latent:
        for nh in (A_KV_HEADS, A_KV_HEADS, B_HEADS, B_HEADS):
            out_specs.append(pl.BlockSpec((tm, nh, HEAD_DIM), lambda i: (i, 0, 0)))
            out_shape.append(jax.ShapeDtypeStruct((m, nh, HEAD_DIM), F32))
    return pl.pallas_call(
        functools.partial(_in0_kernel, latent=latent, tm=tm, seq=seq),
        grid=(m // tm,), in_specs=in_specs, out_specs=out_specs, out_shape=out_shape,
        compiler_params=_params(), name="in0_lat" if latent else "in0_ctx",
    )(*args)


def _attn_units(q_ref, kv_refs, o_ref, *, heads, group, sub, nbs=1):
    seq = q_ref.shape[0] // nbs
    for b in range(nbs):
        for j in range(heads // group):
            sl_kv = slice(j * HEAD_DIM, (j + 1) * HEAD_DIM)
            rows = lambda r: slice(b * (r.shape[0] // nbs), (b + 1) * (r.shape[0] // nbs))
            ks = [k_ref[rows(k_ref), sl_kv].astype(BF16) for k_ref, _ in kv_refs]
            vs = [_with_ones(v_ref[rows(v_ref), sl_kv]) for _, v_ref in kv_refs]
            sls = [slice((j * group + gi) * HEAD_DIM, (j * group + gi + 1) * HEAD_DIM)
                   for gi in range(group)]
            for r0 in range(b * seq, (b + 1) * seq, sub):
                q = jnp.concatenate([q_ref[r0:r0 + sub, sl] for sl in sls], axis=0)
                yield o_ref, r0, sub, sls, [_dot_nt(q, k) for k in ks], vs


def _attn_finish(unit):
    o_ref, r0, sub, sls, ss, vs = unit
    o = _softmax_pv(ss, vs).astype(BF16)
    for gi, sl in enumerate(sls):
        o_ref[r0:r0 + sub, sl] = o[gi * sub:(gi + 1) * sub]


def _attn_kernel(*refs, n_src, heads, group, sub, lag):
    q_ref = refs[0]
    kv_refs = [(refs[1 + 2 * s], refs[2 + 2 * s]) for s in range(n_src)]
    o_ref = refs[1 + 2 * n_src]
    _run_lagged(_attn_units(q_ref, kv_refs, o_ref, heads=heads, group=group, sub=sub),
                lag, _attn_finish)


def _attn(q, srcs, *, batch, seq, heads, group, tq, name, lag=1):
    kvw = (heads // group) * HEAD_DIM
    nq = seq // tq
    in_specs = [pl.BlockSpec((tq, heads * HEAD_DIM), lambda b, i: (b * nq + i, 0))]
    args = [q]
    for k, v, t in srcs:
        for a in (k, v):
            if a.ndim == 3:
                in_specs.append(pl.BlockSpec((None, t, kvw), lambda b, i: (b, 0, 0)))
            else:
                in_specs.append(pl.BlockSpec((t, kvw), lambda b, i: (b, 0)))
            args.append(a)
    return pl.pallas_call(
        functools.partial(_attn_kernel, n_src=len(srcs), heads=heads, group=group,
                          sub=min(tq, 256), lag=lag),
        grid=(batch, nq), in_specs=in_specs,
        out_specs=pl.BlockSpec((tq, heads * HEAD_DIM), lambda b, i: (b * nq + i, 0)),
        out_shape=jax.ShapeDtypeStruct((batch * seq, heads * HEAD_DIM), BF16),
        compiler_params=_params(), name=name,
    )(*args)


def _ctx_attn_kernel(qa_ref, ka_ref, va_ref, qb_ref, kb_ref, vb_ref,
                     c_ref, aw_ref, ab_ref, cka_ref, cva_ref, ckb_ref, cvb_ref,
                     oa_ref, ob_ref, mod_ref, oka_ref, ova_ref, okb_ref, ovb_ref, *, nbs, lag):
    mod_ref[...] = _dot(_silu(c_ref[...]).astype(BF16), aw_ref[...].astype(BF16)) + ab_ref[...]
    for c_r, o_r in ((cka_ref, oka_ref), (cva_ref, ova_ref), (ckb_ref, okb_ref), (cvb_ref, ovb_ref)):
        o_r[...] = c_r[...].reshape(o_r.shape).astype(BF16)
    sub = min(qa_ref.shape[0] // nbs, 256)
    units = itertools.chain(
        _attn_units(qa_ref, [(ka_ref, va_ref)], oa_ref, heads=A_Q_HEADS,
                    group=A_Q_HEADS // A_KV_HEADS, sub=sub, nbs=nbs),
        _attn_units(qb_ref, [(kb_ref, vb_ref)], ob_ref, heads=B_HEADS, group=1, sub=sub, nbs=nbs))
    _run_lagged(units, lag, _attn_finish)


def _ctx_attn(qa, ka, va, qb, kb, vb, cs, ada_w, ada_b, caches, *, batch, seq, nbs=2, lag=4):
    steps = batch // nbs
    tok = lambda n: pl.BlockSpec((nbs * seq, n), lambda i: (i, 0))
    n_ada = ada_w.shape[1]
    tn = n_ada // steps
    dec_batch, past = caches[0].shape[:2]
    spb = steps // dec_batch
    rows = past // spb
    in_specs = [tok(A_WIDTH), tok(A_KV_WIDTH), tok(A_KV_WIDTH), tok(B_WIDTH), tok(B_WIDTH),
                tok(B_WIDTH),
                _resident((ADA_ROWS, D_MODEL)),
                pl.BlockSpec((D_MODEL, tn), lambda i: (0, i)),
                pl.BlockSpec((1, tn), lambda i: (0, i))]
    in_specs += [pl.BlockSpec((None, rows) + c.shape[2:], lambda i: (i // spb, i % spb, 0, 0))
                 for c in caches]
    cache_w = [c.shape[2] * HEAD_DIM for c in caches]
    out_specs = [tok(A_WIDTH), tok(B_WIDTH), pl.BlockSpec((ADA_ROWS, tn), lambda i: (0, i))]
    out_specs += [pl.BlockSpec((rows, w), lambda i: (i, 0)) for w in cache_w]
    out_shape = [jax.ShapeDtypeStruct((batch * seq, A_WIDTH), BF16),
                 jax.ShapeDtypeStruct((batch * seq, B_WIDTH), BF16),
                 jax.ShapeDtypeStruct((ADA_ROWS, n_ada), F32)]
    out_shape += [jax.ShapeDtypeStruct((dec_batch * past, w), BF16) for w in cache_w]
    return pl.pallas_call(
        functools.partial(_ctx_attn_kernel, nbs=nbs, lag=lag),
        grid=(steps,), in_specs=in_specs, out_specs=out_specs, out_shape=out_shape,
        compiler_params=_params(), name="ctx_attn",
    )(qa, ka, va, qb, kb, vb, cs, ada_w, ada_b.reshape(1, n_ada), *caches)


def _na_geometry(rows):
    wh = min(NA_WIN_H, rows)
    row_start = np.clip(np.arange(rows) - wh // 2, 0, rows - wh)
    return wh, row_start


def _na_bias_kernel(rpb_ref, o_ref, *, rows):
    hd = pl.program_id(0)
    wh, row_start = _na_geometry(rows)
    nx = 2 * NA_WIN_W - 1
    qc = lax.broadcasted_iota(jnp.int32, (GRID_W, GRID_W), 0)
    kc = lax.broadcasted_iota(jnp.int32, (GRID_W, GRID_W), 1)
    col_start = jnp.clip(qc - NA_WIN_W // 2, 0, GRID_W - NA_WIN_W)
    col_ok = (kc >= col_start) & (kc < col_start + NA_WIN_W)
    dx = kc - qc + NA_WIN_W - 1
    base = hd * ((2 * NA_WIN_H - 1) * nx)
    tiles = {}
    for dy in range(-(wh - 1), wh):
        t = jnp.zeros((GRID_W, GRID_W), F32)
        for j in range(nx):
            t = jnp.where(dx == j, rpb_ref[base + (dy + NA_WIN_H - 1) * nx + j], t)
        tiles[dy] = jnp.where(col_ok, t * LOG2E, NEG_INF)
    o_ref[...] = jnp.full(o_ref.shape, NEG_INF, F32)
    for qr in range(rows):
        for kr in range(int(row_start[qr]), int(row_start[qr]) + wh):
            o_ref[0, qr * GRID_W:(qr + 1) * GRID_W, kr * GRID_W:(kr + 1) * GRID_W] = tiles[kr - qr]


def _na_bias(rpb, seq):
    rows = seq // GRID_W
    return pl.pallas_call(
        functools.partial(_na_bias_kernel, rows=rows),
        grid=(B_HEADS,),
        in_specs=[pl.BlockSpec(memory_space=pltpu.SMEM)],
        out_specs=pl.BlockSpec((1, seq, seq), lambda h: (h, 0, 0)),
        out_shape=jax.ShapeDtypeStruct((B_HEADS, seq, seq), F32),
        compiler_params=_params(), name="na_bias",
    )(rpb.reshape(-1))


def _na_windows(rows, tq):
    wh, row_start = _na_geometry(rows)
    rq = tq // GRID_W
    wins = []
    for qb in range(rows // rq):
        lo = int(row_start[qb * rq]) * GRID_W
        hi = (int(row_start[qb * rq + rq - 1]) + wh) * GRID_W
        wins.append((lo // LANES * LANES, -(-hi // LANES) * LANES))
    return wins


def _na_kernel(q_ref, k_ref, v_ref, kc_ref, vc_ref, bias_ref, o_ref, *, seq, tq, wins, lag):
    nb = q_ref.shape[0] // seq
    past = kc_ref.shape[0] // nb

    def scores():
        for b in range(nb):
            kc = kc_ref[b * past:(b + 1) * past, :]
            vc = _with_ones(vc_ref[b * past:(b + 1) * past, :])
            v = _with_ones(v_ref[b * seq:(b + 1) * seq, :])
            for qb, (lo, hi) in enumerate(wins):
                rs = slice(qb * tq, (qb + 1) * tq)
                rows = slice(b * seq + qb * tq, b * seq + (qb + 1) * tq)
                q = q_ref[rows, :]
                s_loc = _dot_nt(q, k_ref[b * seq + lo:b * seq + hi, :]) + bias_ref[0, rs, lo:hi]
                yield rows, [s_loc, _dot_nt(q, kc)], [v[lo:hi], vc]

    def finish(unit):
        rows, ss, vs = unit
        o_ref[rows, :] = _softmax_pv(ss, vs).astype(BF16)

    _run_lagged(scores(), lag, finish)


def _na(q, k, v, kc, vc, bias, *, batch, seq, tq=512, nb=4, lag=1):
    past = kc.shape[0] // batch
    wins = _na_windows(seq // GRID_W, tq)
    tok = pl.BlockSpec((nb * seq, HEAD_DIM), lambda h, b: (b, h))
    cache = pl.BlockSpec((nb * past, HEAD_DIM), lambda h, b: (b, h))
    return pl.pallas_call(
        functools.partial(_na_kernel, seq=seq, tq=tq, wins=wins, lag=lag),
        grid=(B_HEADS, batch // nb),
        in_specs=[tok, tok, tok, cache, cache,
                  pl.BlockSpec((1, seq, seq), lambda h, b: (h, 0, 0))],
        out_specs=tok,
        out_shape=jax.ShapeDtypeStruct((batch * seq, B_WIDTH), BF16),
        compiler_params=_params(), name="na",
    )(q, k, v, kc, vc, bias)


def _out_kernel(*refs, latent, tm, seq, n_o, final, gate_t, side):
    x_ref, shift_ref, scale_ref, gate_ref, g_ref = refs[:5]
    o_refs = refs[5:5 + n_o]
    wg_ref, wo_ref = refs[5 + n_o:7 + n_o]
    n_fixed = 7 + n_o + (1 if final else 0)
    fn_ref = refs[n_fixed - 1] if final else None
    n_si = side.n_in if side else 0
    y_ref = refs[n_fixed + n_si]
    if side:
        side.body(refs[n_fixed:n_fixed + n_si], refs[n_fixed + n_si + 1:])
    row = (pl.program_id(0) * tm) // seq if latent else 0
    h = _mod_norm(x_ref, g_ref, shift_ref, scale_ref, row)
    gt = _dot_nt(h, wg_ref[...]) if gate_t else _dot(h, wg_ref[...])
    o = jnp.concatenate([r[...] for r in o_refs], axis=-1).astype(F32)
    t = (o * _silu(gt)).astype(BF16)
    y = x_ref[...] + gate_ref[pl.ds(row, 1), :] * _dot(t, wo_ref[...])
    if final:
        y = _rms(y, fn_ref[...])
    y_ref[...] = y


def _out(x, mod, g, os_, wg, wo, fn, *, latent, seq, name, tm=512, gate_t=False, side=None):
    m = x.shape[0]
    width = wo.shape[0]
    final = fn is not None
    in_specs = [pl.BlockSpec((tm, D_MODEL), lambda i: (i, 0))] + _mod_specs(latent, (0, 1, 2))
    in_specs += [_resident((1, D_MODEL))]
    in_specs += [pl.BlockSpec((tm, o.shape[1]), lambda i: (i, 0)) for o in os_]
    in_specs += [_resident(wg.shape), _resident((width, D_MODEL))]
    args = [x, mod, mod, mod, g, *os_, wg, wo]
    if final:
        in_specs.append(_resident((1, D_MODEL)))
        args.append(fn)
    out_specs = [pl.BlockSpec((tm, D_MODEL), lambda i: (i, 0))]
    out_shape = [jax.ShapeDtypeStruct((m, D_MODEL), F32)]
    if side:
        assert side.steps == m // tm
        in_specs += side.in_specs
        args += side.args
        out_specs += side.out_specs
        out_shape += side.out_shape
    res = pl.pallas_call(
        functools.partial(_out_kernel, latent=latent, tm=tm, seq=seq, n_o=len(os_), final=final,
                          gate_t=gate_t, side=side),
        grid=(m // tm,), in_specs=in_specs, out_specs=out_specs, out_shape=out_shape,
        compiler_params=_params(), name=name,
    )(*args)
    return res if side else res[0]


def _in1_kernel(*refs, latent, tm, seq):
    if latent:
        (x_ref, shift_ref, scale_ref, g_ref, w_ref, qan_ref, kvn_ref, wq_ref,
         cos_ref, slo_ref, shi_ref, q_ref, ckv_ref, kr_ref) = refs
    else:
        (x_ref, shift_ref, scale_ref, g_ref, w_ref, qan_ref, kvn_ref, wq_ref,
         q_ref, ckv_ref, kr_ref) = refs
    row = (pl.program_id(0) * tm) // seq if latent else 0
    h = _mod_norm(x_ref, g_ref, shift_ref, scale_ref, row)
    z = _dot_nt(h, w_ref[...])
    q_lat = _rms(z[:, :Q_LORA], qan_ref[...]).astype(BF16)
    q = _dot(q_lat, wq_ref[...])
    kr = z[:, Q_LORA + KV_LORA:]
    if latent:
        cos, slo, shi = cos_ref[...], slo_ref[...], shi_ref[...]
        kr = _rope(kr, cos, slo, shi, QK_ROPE // 4)
    for j in range(MLA_HEADS * MLA_QK // LANES):
        sl = slice(j * LANES, (j + 1) * LANES)
        qj = q[:, sl]
        if latent and j % 2 == 1:
            qj = _rope(qj, cos, slo, shi, QK_ROPE // 4)
        q_ref[:, sl] = (qj * Q_SCALE_L1).astype(BF16)
    ckv_ref[...] = _rms(z[:, Q_LORA:Q_LORA + KV_LORA], kvn_ref[...]).astype(ckv_ref.dtype)
    kr_ref[...] = kr[:, :QK_ROPE].astype(kr_ref.dtype)


def _in1(x, mod, g, w, qan, kvn, wq, rope_tabs, *, latent, seq, tm=512):
    m = x.shape[0]
    row_spec = lambda n: pl.BlockSpec((tm, n), lambda i: (i, 0))
    in_specs = [row_spec(D_MODEL)] + _mod_specs(latent, (0, 1)) + [
        _resident((1, D_MODEL)), _resident(w.shape),
        _resident((1, Q_LORA)), _resident((1, KV_LORA)),
        _resident((Q_LORA, MLA_HEADS * MLA_QK))]
    args = [x, mod, mod, g, w, qan, kvn, wq]
    if latent:
        nblk = seq // tm
        in_specs += [pl.BlockSpec((tm, LANES), lambda i: (i % nblk, 0))] * 3
        args += list(rope_tabs)
    lat_dt = BF16 if latent else F32
    widths = [(MLA_HEADS * MLA_QK, BF16), (KV_LORA, lat_dt), (QK_ROPE, lat_dt)]
    return pl.pallas_call(
        functools.partial(_in1_kernel, latent=latent, tm=tm, seq=seq),
        grid=(m // tm,), in_specs=in_specs,
        out_specs=[row_spec(n) for n, _ in widths],
        out_shape=[jax.ShapeDtypeStruct((m, n), dt) for n, dt in widths],
        compiler_params=_params(), name="in1_lat" if latent else "in1_ctx",
    )(*args)


def _mla_kernel(*refs, n_src, hb, nbs, seq, lag):
    q_ref = refs[0]
    src = refs[1:1 + 2 * n_src]
    w_ref = refs[1 + 2 * n_src]
    o_ref = refs[2 + 2 * n_src]
    ckv = [src[2 * s][...].astype(BF16) for s in range(n_src)]
    krp = []
    for s in range(n_src):
        kr = src[2 * s + 1][...].astype(F32)
        pad = jnp.zeros((kr.shape[0], LANES - QK_ROPE), F32)
        krp.append(jnp.concatenate([kr, pad], axis=1).astype(BF16))
    kvw = QK_NOPE + V_DIM

    def scores():
        for hd in range(hb):
            kv = [_dot(c, w_ref[:, hd * kvw:(hd + 1) * kvw]).astype(BF16) for c in ckv]
            for b in range(nbs):
                rq = slice(b * seq, (b + 1) * seq)
                part = lambda a: a[b * (a.shape[0] // nbs):(b + 1) * (a.shape[0] // nbs)]
                kvb = [part(kv_s) for kv_s in kv]
                kcat = [jnp.concatenate([kv_s[:, :QK_NOPE], part(kr_s)], axis=1)
                        for kv_s, kr_s in zip(kvb, krp)]
                q = q_ref[rq, hd * MLA_QK:(hd + 1) * MLA_QK]
                yield (rq, hd, [_dot_nt(q, kc) for kc in kcat],
                       [_with_ones(kv_s[:, QK_NOPE:]) for kv_s in kvb])

    def finish(unit):
        rq, hd, ss, vs = unit
        o_ref[rq, hd * V_DIM:(hd + 1) * V_DIM] = _softmax_pv(ss, vs).astype(BF16)

    _run_lagged(scores(), lag, finish)


def _mla(q, srcs, w_kv, *, batch, seq, hb, name, nbs=1, lag=0):
    kvw = QK_NOPE + V_DIM
    in_specs = [pl.BlockSpec((nbs * seq, hb * MLA_QK), lambda b, j: (b, j))]
    args = [q]
    for ckv, kr, t in srcs:
        for a in (ckv, kr):
            wd = a.shape[-1]
            if a.ndim == 3:
                assert nbs == 1
                in_specs.append(pl.BlockSpec((None, t, wd), lambda b, j: (b, 0, 0)))
            else:
                in_specs.append(pl.BlockSpec((nbs * t, wd), lambda b, j: (b, 0)))
            args.append(a)
    in_specs.append(pl.BlockSpec((KV_LORA, hb * kvw), lambda b, j: (0, j)))
    args.append(w_kv)
    return pl.pallas_call(
        functools.partial(_mla_kernel, n_src=len(srcs), hb=hb, nbs=nbs, seq=seq, lag=lag),
        grid=(batch // nbs, MLA_HEADS // hb), in_specs=in_specs,
        out_specs=pl.BlockSpec((nbs * seq, hb * V_DIM), lambda b, j: (b, j)),
        out_shape=jax.ShapeDtypeStruct((batch * seq, L1_WIDTH), BF16),
        compiler_params=_params(), name=name,
    )(*args)


def _rope_tables(seq, d):
    a = d // 2
    t = jnp.arange(seq)
    row = (t // GRID_W).astype(F32)
    col = (t % GRID_W).astype(F32)
    inv = 1.0 / (ROPE_THETA ** (jnp.arange(a // 2, dtype=F32) * 2.0 / a))
    ang = jnp.concatenate([row[:, None] * inv] * 2 + [col[:, None] * inv] * 2, axis=-1)
    ang = jnp.tile(ang, (1, LANES // d))
    lower = jnp.tile((jnp.arange(d) % a) < a // 2, LANES // d)[None, :]
    cos, sin = jnp.cos(ang), jnp.sin(ang)
    zero = jnp.zeros_like(sin)
    return cos, jnp.where(lower, -sin, zero), jnp.where(lower, zero, sin)


def kernel(x_prompt, x_sample, cache_l0_a_k, cache_l0_a_v, cache_l0_b_k, cache_l0_b_v, cache_l1_ckv, cache_l1_krope, c, c_ctx, l0_norm, l0_ada_w, l0_ada_b, l0_w_in, l0_q_norm, l0_k_norm, l0_rpb, l0_w_out, l1_norm, l1_ada_w, l1_ada_b, l1_w_in, l1_q_a_norm, l1_w_q_b, l1_kv_a_norm, l1_w_kv_b, l1_w_out, final_norm):
    batch, seq_c, _ = x_prompt.shape
    dec_batch, seq_l, _ = x_sample.shape
    past = cache_l0_a_k.shape[1]

    w0_qkv, w0_gate, w0_out = _prep_l0_weights(l0_w_in, l0_w_out)

    row2 = lambda v: v.reshape(1, -1)
    cs = jnp.zeros((ADA_ROWS, D_MODEL), F32).at[:dec_batch].set(c).at[8].set(c_ctx)
    mod0 = _adaln(cs, l0_ada_w, l0_ada_b)

    rope_a = _rope_tables(seq_l, HEAD_DIM)
    rope_c = _rope_tables(seq_l, QK_ROPE)

    xc = x_prompt.reshape(batch * seq_c, D_MODEL)
    qa, ka, va, qb, kb, vb, ka_f, va_f, kb_f, vb_f = _in0(
        xc, mod0, row2(l0_norm), w0_qkv, row2(l0_q_norm), row2(l0_k_norm), None,
        latent=False, seq=seq_c)
    oa, ob, mod1, ck_a, cv_a, ck_b, cv_b = _ctx_attn(
        qa, ka, va, qb, kb, vb, cs, l1_ada_w, l1_ada_b,
        [cache_l0_a_k, cache_l0_a_v, cache_l0_b_k, cache_l0_b_v], batch=batch, seq=seq_c)
    out_tm = 512
    side = _l1_weight_side(l1_w_in.T, l1_w_q_b, l1_w_kv_b, l1_w_out, batch * seq_c // out_tm)
    xc, w1_all, w1_gate, wq, w_kv, w1_out = _out(
        xc, mod0, row2(l0_norm), [oa, ob], w0_gate, w0_out, None,
        latent=False, seq=seq_c, name="out0_ctx", tm=out_tm, side=side)
    q1, ckv_f, kr_f = _in1(xc, mod1, row2(l1_norm), w1_all, row2(l1_q_a_norm),
                           row2(l1_kv_a_norm), wq, None, latent=False, seq=seq_c)
    oc = _mla(q1, [(ckv_f, kr_f, seq_c)], w_kv, batch=batch, seq=seq_c,
              hb=MLA_HEADS, nbs=2, lag=4, name="mla_ctx")
    y_prompt = _out(xc, mod1, row2(l1_norm), [oc], w1_gate, w1_out, row2(final_norm),
                    latent=False, seq=seq_c, name="out1_ctx", gate_t=True)

    xl = x_sample.reshape(dec_batch * seq_l, D_MODEL)
    qa, ka, va, qb, kb, vb = _in0(
        xl, mod0, row2(l0_norm), w0_qkv, row2(l0_q_norm), row2(l0_k_norm), rope_a,
        latent=True, seq=seq_l)
    oa = _attn(qa, [(ck_a, cv_a, past), (ka, va, seq_l)],
               batch=dec_batch, seq=seq_l, heads=A_Q_HEADS, group=A_Q_HEADS // A_KV_HEADS,
               tq=seq_l, name="attn_a_lat")
    bias = _na_bias(l0_rpb, seq_l)
    ob = _na(qb, kb, vb, ck_b, cv_b, bias, batch=dec_batch, seq=seq_l)
    xl = _out(xl, mod0, row2(l0_norm), [oa, ob], w0_gate, w0_out, None,
              latent=True, seq=seq_l, name="out0_lat")
    q1, ckv, kr = _in1(xl, mod1, row2(l1_norm), w1_all, row2(l1_q_a_norm),
                       row2(l1_kv_a_norm), wq, rope_c, latent=True, seq=seq_l)
    oc = _mla(q1, [(cache_l1_ckv, cache_l1_krope, past), (ckv, kr, seq_l)], w_kv,
              batch=dec_batch, seq=seq_l, hb=8, lag=1, name="mla_lat")
    y_sample = _out(xl, mod1, row2(l1_norm), [oc], w1_gate, w1_out, row2(final_norm),
                    latent=True, seq=seq_l, name="out1_lat", gate_t=True)

    return (y_prompt.reshape(batch, seq_c, D_MODEL),
            y_sample.reshape(dec_batch, seq_l, D_MODEL),
            ka_f.reshape(batch, seq_c, A_KV_HEADS, HEAD_DIM),
            va_f.reshape(batch, seq_c, A_KV_HEADS, HEAD_DIM),
            kb_f.reshape(batch, seq_c, B_HEADS, HEAD_DIM),
            vb_f.reshape(batch, seq_c, B_HEADS, HEAD_DIM),
            ckv_f.reshape(batch, seq_c, KV_LORA),
            kr_f.reshape(batch, seq_c, QK_ROPE))
```

```python
import functools
import itertools
import math
from typing import Callable, NamedTuple

import jax
import jax.numpy as jnp
import numpy as np
from jax import lax
from jax.experimental import pallas as pl
from jax.experimental.pallas import tpu as pltpu

LANES = 128
VMEM_LIMIT = 56 * 1024 * 1024
ADA_ROWS = 16
F32 = jnp.float32
BF16 = jnp.bfloat16

D_MODEL = 2048
GRID_W = 64
HEAD_DIM = 128
A_Q_HEADS = 8
A_KV_HEADS = 2
B_HEADS = 8
A_WIDTH = A_Q_HEADS * HEAD_DIM
A_KV_WIDTH = A_KV_HEADS * HEAD_DIM
B_WIDTH = B_HEADS * HEAD_DIM
L0_WIDTH = A_WIDTH + B_WIDTH
L0_QKV = A_WIDTH + 2 * A_KV_WIDTH + 3 * B_WIDTH
NA_WIN_H = 8
NA_WIN_W = 16
MLA_HEADS = 16
Q_LORA = 512
KV_LORA = 512
QK_NOPE = 128
QK_ROPE = 64
V_DIM = 128
L1_WIDTH = MLA_HEADS * V_DIM
L1_LAT = Q_LORA + KV_LORA + QK_ROPE
L1_LAT_PAD = L1_LAT + (LANES - QK_ROPE)
MLA_QK = 2 * LANES
ROPE_THETA = 10000.0
EPS = 1e-6
NEG_INF = -1e30
LOG2E = 1.4426950408889634
Q_SCALE_L0 = HEAD_DIM ** -0.5 * LOG2E
Q_SCALE_L1 = (QK_NOPE + QK_ROPE) ** -0.5 * LOG2E


def _params(vmem=VMEM_LIMIT):
    return pltpu.CompilerParams(vmem_limit_bytes=vmem)


def _resident(shape):
    nd = len(shape)
    return pl.BlockSpec(shape, lambda *_: (0,) * nd, pipeline_mode=pl.Buffered(1))


def _dot(a, b):
    return jnp.dot(a, b, preferred_element_type=F32)


def _dot_nt(a, b):
    return lax.dot_general(a, b, (((1,), (1,)), ((), ())), preferred_element_type=F32)


def _rms(x, g):
    return x * lax.rsqrt(jnp.mean(x * x, axis=-1, keepdims=True) + EPS) * g


def _silu(x):
    return x * jax.nn.sigmoid(x)


def _rope(x, cos, sin_lo, sin_hi, half):
    return (x * cos + pltpu.roll(x, half, 1) * sin_hi
            + pltpu.roll(x, LANES - half, 1) * sin_lo)


def _mod_norm(x_ref, g_ref, shift_ref, scale_ref, row):
    x = x_ref[...]
    h = _rms(x, g_ref[...]) * (1.0 + scale_ref[pl.ds(row, 1), :]) + shift_ref[pl.ds(row, 1), :]
    return h.astype(BF16)


def _with_ones(v):
    v = v.astype(BF16)
    return jnp.concatenate([v, jnp.ones_like(v)], axis=1)


def _softmax_pv(ss, vs):
    mx = ss[0].max(axis=-1, keepdims=True)
    for s in ss[1:]:
        mx = jnp.maximum(mx, s.max(axis=-1, keepdims=True))
    acc = _dot(jnp.exp2(ss[0] - mx).astype(BF16), vs[0])
    for s, v in zip(ss[1:], vs[1:]):
        acc = acc + _dot(jnp.exp2(s - mx).astype(BF16), v)
    return acc[:, :LANES] / acc[:, LANES:]


def _run_lagged(units, lag, finish):
    pending = []
    for unit in units:
        pending.append(unit)
        while len(pending) > lag:
            finish(pending.pop(0))
    for unit in pending:
        finish(unit)


WEIGHT_ROW_BLOCKS = 16


def _cast_kernel(w_ref, o_ref):
    o_ref[...] = w_ref[...].astype(BF16)


def _cast_qkv(w0):
    nb = WEIGHT_ROW_BLOCKS
    rows = w0.shape[0] // nb
    return pl.pallas_call(
        _cast_kernel, grid=(nb,),
        in_specs=[pl.BlockSpec((rows, L0_QKV), lambda i: (i, 0))],
        out_specs=pl.BlockSpec((rows, L0_QKV), lambda i: (i, 0)),
        out_shape=jax.ShapeDtypeStruct((w0.shape[0], L0_QKV), BF16),
        compiler_params=_params(), name="cast_qkv",
    )(w0)


class _SideWork(NamedTuple):
    steps: int
    args: list
    in_specs: list
    out_specs: list
    out_shape: list
    body: Callable

    @property
    def n_in(self):
        return len(self.args)


def _l1_weight_side(w1_t, wq, wkv, w1o, steps):
    g_rows = L1_WIDTH // steps
    g_blk = math.gcd(L1_LAT, g_rows)
    g_n = g_rows // g_blk
    lat_rows = 128
    lat_n = L1_LAT_PAD // lat_rows
    row_blk = lambda a: pl.BlockSpec((a.shape[0] // steps, a.shape[1]), lambda i: (i, 0))
    lat_idx = lambda i: (jnp.minimum(i, lat_n - 1), 0)
    in_specs = [pl.BlockSpec((lat_rows, D_MODEL), lat_idx)]
    in_specs += [pl.BlockSpec((g_blk, D_MODEL), lambda i, k=k: (L1_LAT // g_blk + g_n * i + k, 0))
                 for k in range(g_n)]
    in_specs += [row_blk(wq), row_blk(wkv), row_blk(w1o)]
    args = [w1_t] * (1 + g_n) + [wq, wkv, w1o]
    out_dims = [(L1_LAT_PAD, D_MODEL), (L1_WIDTH, D_MODEL), (Q_LORA, MLA_HEADS * MLA_QK),
                wkv.shape, w1o.shape]
    out_specs = [pl.BlockSpec((lat_rows, D_MODEL), lat_idx)]
    out_specs += [pl.BlockSpec((r // steps, c), lambda i: (i, 0)) for r, c in out_dims[1:]]

    def body(ins, outs):
        lat_in, g_in = ins[0], ins[1:1 + g_n]
        wq_in, wkv_in, w1o_in = ins[1 + g_n:]
        lat_o, g_o, wq_o, wkv_o, w1o_o = outs

        @pl.when(pl.program_id(0) < lat_n)
        def _():
            lat_o[...] = lat_in[...].astype(BF16)

        for k, g_ref in enumerate(g_in):
            g_o[k * g_blk:(k + 1) * g_blk, :] = g_ref[...].astype(BF16)
        qw = QK_NOPE + QK_ROPE
        pad = jnp.zeros((wq_in.shape[0], MLA_QK - qw), F32)
        parts = []
        for hd in range(MLA_HEADS):
            parts += [wq_in[:, hd * qw:(hd + 1) * qw], pad]
        wq_o[...] = jnp.concatenate(parts, axis=1).astype(BF16)
        wkv_o[...] = wkv_in[...].astype(BF16)
        w1o_o[...] = w1o_in[...].astype(BF16)

    return _SideWork(steps, args, in_specs, out_specs,
                     [jax.ShapeDtypeStruct(d, BF16) for d in out_dims], body)


def _adaln_kernel(c_ref, w_ref, b_ref, o_ref):
    c = c_ref[...]
    o_ref[...] = _dot(_silu(c).astype(BF16), w_ref[...].astype(BF16)) + b_ref[...]


def _adaln(cs, w, b):
    tn = 1024
    n = w.shape[1]
    return pl.pallas_call(
        _adaln_kernel,
        grid=(n // tn,),
        in_specs=[pl.BlockSpec((ADA_ROWS, D_MODEL), lambda j: (0, 0)),
                  pl.BlockSpec((D_MODEL, tn), lambda j: (0, j)),
                  pl.BlockSpec((1, tn), lambda j: (0, j))],
        out_specs=pl.BlockSpec((ADA_ROWS, tn), lambda j: (0, j)),
        out_shape=jax.ShapeDtypeStruct((ADA_ROWS, n), F32),
        compiler_params=_params(),
        name="adaln",
    )(cs, w, b.reshape(1, n))


def _mod_specs(latent, cols):
    rb = 0 if latent else 1
    return [pl.BlockSpec((8, D_MODEL), lambda i, c=c: (rb, c)) for c in cols]


def _store_heads(o_ref, y):
    o_ref[...] = y.reshape(o_ref.shape)


def _in0_kernel(*refs, latent, tm, seq, side):
    n_in = 10 if latent else 7
    n_si = side.n_in if side else 0
    n_out = 6 if latent else 10
    outs = refs[n_in + n_si:n_in + n_si + n_out]
    if side:
        side.body(refs[n_in:n_in + n_si], refs[n_in + n_si + n_out:])
    if latent:
        (x_ref, shift_ref, scale_ref, g_ref, w_ref, qn_ref, kn_ref,
         cos_ref, slo_ref, shi_ref) = refs[:n_in]
        qa_ref, ka_ref, va_ref, qb_ref, kb_ref, vb_ref = outs
    else:
        x_ref, shift_ref, scale_ref, g_ref, w_ref, qn_ref, kn_ref = refs[:n_in]
        (qa_ref, ka_ref, va_ref, qb_ref, kb_ref, vb_ref,
         kaf_ref, vaf_ref, kbf_ref, vbf_ref) = outs
    row = (pl.program_id(0) * tm) // seq if latent else 0
    h = _mod_norm(x_ref, g_ref, shift_ref, scale_ref, row)

    def head_norm(y, g_ref_):
        y = _rms(y, g_ref_[...])
        if latent:
            y = _rope(y, cos_ref[...], slo_ref[...], shi_ref[...], HEAD_DIM // 4)
        return y

    c_ka, c_va, c_qb = A_WIDTH, A_WIDTH + A_KV_WIDTH, A_WIDTH + 2 * A_KV_WIDTH
    c_kb, c_vb = c_qb + B_WIDTH, c_qb + 2 * B_WIDTH
    y = _dot(h, w_ref[:, c_kb:c_kb + B_WIDTH])
    kb_ref[...] = y.astype(BF16)
    if not latent:
        _store_heads(kbf_ref, y)
    y = _dot(h, w_ref[:, c_vb:c_vb + B_WIDTH])
    vb_ref[...] = y.astype(BF16)
    if not latent:
        _store_heads(vbf_ref, y)
    y = _dot(h, w_ref[:, c_ka:c_ka + A_KV_WIDTH])
    for hd in range(A_KV_HEADS):
        sl = slice(hd * HEAD_DIM, (hd + 1) * HEAD_DIM)
        k = head_norm(y[:, sl], kn_ref)
        ka_ref[:, sl] = k.astype(BF16)
        if not latent:
            kaf_ref[:, hd, :] = k
    y = _dot(h, w_ref[:, c_va:c_va + A_KV_WIDTH])
    va_ref[...] = y.astype(BF16)
    if not latent:
        _store_heads(vaf_ref, y)
    y = _dot(h, w_ref[:, :A_WIDTH])
    for hd in range(A_Q_HEADS):
        sl = slice(hd * HEAD_DIM, (hd + 1) * HEAD_DIM)
        qa_ref[:, sl] = (head_norm(y[:, sl], qn_ref) * Q_SCALE_L0).astype(BF16)
    y = _dot(h, w_ref[:, c_qb:c_qb + B_WIDTH])
    qb_ref[...] = (y * Q_SCALE_L0).astype(BF16)


def _in0(x, mod, g, w, qn, kn, rope_tabs, *, latent, seq, tm=512, side=None):
    m = x.shape[0]
    row_spec = lambda n: pl.BlockSpec((tm, n), lambda i: (i, 0))
    in_specs = [row_spec(D_MODEL)] + _mod_specs(latent, (0, 1)) + [
        _resident((1, D_MODEL)), _resident((D_MODEL, L0_QKV)),
        _resident((1, HEAD_DIM)), _resident((1, HEAD_DIM))]
    args = [x, mod, mod, g, w, qn, kn]
    if latent:
        nblk = seq // tm
        in_specs += [pl.BlockSpec((tm, LANES), lambda i: (i % nblk, 0))] * 3
        args += list(rope_tabs)
    widths = [A_WIDTH, A_KV_WIDTH, A_KV_WIDTH, B_WIDTH, B_WIDTH, B_WIDTH]
    out_specs = [row_spec(n) for n in widths]
    out_shape = [jax.ShapeDtypeStruct((m, n), BF16) for n in widths]
    if not latent:
        for nh in (A_KV_HEADS, A_KV_HEADS, B_HEADS, B_HEADS):
            out_specs.append(pl.BlockSpec((tm, nh, HEAD_DIM), lambda i: (i, 0, 0)))
            out_shape.append(jax.ShapeDtypeStruct((m, nh, HEAD_DIM), F32))
    if side:
        assert side.steps == m // tm
        in_specs += side.in_specs
        args += side.args
        out_specs += side.out_specs
        out_shape += side.out_shape
    return pl.pallas_call(
        functools.partial(_in0_kernel, latent=latent, tm=tm, seq=seq, side=side),
        grid=(m // tm,), in_specs=in_specs, out_specs=out_specs, out_shape=out_shape,
        compiler_params=_params(), name="in0_lat" if latent else "in0_ctx",
    )(*args)


def _attn_units(q_ref, kv_refs, o_ref, *, heads, group, sub, nbs=1):
    seq = q_ref.shape[0] // nbs
    for b in range(nbs):
        for j in range(heads // group):
            sl_kv = slice(j * HEAD_DIM, (j + 1) * HEAD_DIM)
            rows = lambda r: slice(b * (r.shape[0] // nbs), (b + 1) * (r.shape[0] // nbs))
            ks = [k_ref[rows(k_ref), sl_kv].astype(BF16) for k_ref, _ in kv_refs]
            vs = [_with_ones(v_ref[rows(v_ref), sl_kv]) for _, v_ref in kv_refs]
            sls = [slice((j * group + gi) * HEAD_DIM, (j * group + gi + 1) * HEAD_DIM)
                   for gi in range(group)]
            for r0 in range(b * seq, (b + 1) * seq, sub):
                q = jnp.concatenate([q_ref[r0:r0 + sub, sl] for sl in sls], axis=0)
                yield o_ref, r0, sub, sls, [_dot_nt(q, k) for k in ks], vs


def _attn_finish(unit):
    o_ref, r0, sub, sls, ss, vs = unit
    o = _softmax_pv(ss, vs).astype(BF16)
    for gi, sl in enumerate(sls):
        o_ref[r0:r0 + sub, sl] = o[gi * sub:(gi + 1) * sub]


def _attn_kernel(*refs, n_src, heads, group, sub, lag):
    q_ref = refs[0]
    kv_refs = [(refs[1 + 2 * s], refs[2 + 2 * s]) for s in range(n_src)]
    o_ref = refs[1 + 2 * n_src]
    _run_lagged(_attn_units(q_ref, kv_refs, o_ref, heads=heads, group=group, sub=sub),
                lag, _attn_finish)


def _attn(q, srcs, *, batch, seq, heads, group, tq, name, lag=1):
    kvw = (heads // group) * HEAD_DIM
    nq = seq // tq
    in_specs = [pl.BlockSpec((tq, heads * HEAD_DIM), lambda b, i: (b * nq + i, 0))]
    args = [q]
    for k, v, t in srcs:
        for a in (k, v):
            if a.ndim == 3:
                in_specs.append(pl.BlockSpec((None, t, kvw), lambda b, i: (b, 0, 0)))
            else:
                in_specs.append(pl.BlockSpec((t, kvw), lambda b, i: (b, 0)))
            args.append(a)
    return pl.pallas_call(
        functools.partial(_attn_kernel, n_src=len(srcs), heads=heads, group=group,
                          sub=min(tq, 256), lag=lag),
        grid=(batch, nq), in_specs=in_specs,
        out_specs=pl.BlockSpec((tq, heads * HEAD_DIM), lambda b, i: (b * nq + i, 0)),
        out_shape=jax.ShapeDtypeStruct((batch * seq, heads * HEAD_DIM), BF16),
        compiler_params=_params(), name=name,
    )(*args)


def _cache_side(caches, steps):
    dec_batch, past = caches[0].shape[:2]
    spb = steps // dec_batch
    rows = past // spb
    in_specs = [pl.BlockSpec((None, rows) + c.shape[2:],
                             lambda i, *_: (i // spb, i % spb, 0, 0)) for c in caches]
    widths = [c.shape[2] * HEAD_DIM for c in caches]
    out_specs = [pl.BlockSpec((rows, w), lambda i, *_: (i, 0)) for w in widths]
    out_shape = [jax.ShapeDtypeStruct((dec_batch * past, w), BF16) for w in widths]

    def body(ins, outs):
        for c_ref, o_ref in zip(ins, outs):
            o_ref[...] = c_ref[...].reshape(o_ref.shape).astype(BF16)

    return _SideWork(steps, list(caches), in_specs, out_specs, out_shape, body)


def _ctx_attn_kernel(qa_ref, ka_ref, va_ref, qb_ref, kb_ref, vb_ref, c_ref, aw_ref, ab_ref,
                     oa_ref, ob_ref, mod_ref, *, nbs, lag):
    mod_ref[...] = _dot(_silu(c_ref[...]).astype(BF16), aw_ref[...].astype(BF16)) + ab_ref[...]
    sub = min(qa_ref.shape[0] // nbs, 256)
    units = itertools.chain(
        _attn_units(qa_ref, [(ka_ref, va_ref)], oa_ref, heads=A_Q_HEADS,
                    group=A_Q_HEADS // A_KV_HEADS, sub=sub, nbs=nbs),
        _attn_units(qb_ref, [(kb_ref, vb_ref)], ob_ref, heads=B_HEADS, group=1, sub=sub, nbs=nbs))
    _run_lagged(units, lag, _attn_finish)


def _ctx_attn(qa, ka, va, qb, kb, vb, cs, ada_w, ada_b, *, batch, seq, nbs=2, lag=4):
    steps = batch // nbs
    tok = lambda n: pl.BlockSpec((nbs * seq, n), lambda i: (i, 0))
    n_ada = ada_w.shape[1]
    tn = n_ada // steps
    in_specs = [tok(A_WIDTH), tok(A_KV_WIDTH), tok(A_KV_WIDTH), tok(B_WIDTH), tok(B_WIDTH),
                tok(B_WIDTH),
                _resident((ADA_ROWS, D_MODEL)),
                pl.BlockSpec((D_MODEL, tn), lambda i: (0, i)),
                pl.BlockSpec((1, tn), lambda i: (0, i))]
    out_specs = [tok(A_WIDTH), tok(B_WIDTH), pl.BlockSpec((ADA_ROWS, tn), lambda i: (0, i))]
    out_shape = [jax.ShapeDtypeStruct((batch * seq, A_WIDTH), BF16),
                 jax.ShapeDtypeStruct((batch * seq, B_WIDTH), BF16),
                 jax.ShapeDtypeStruct((ADA_ROWS, n_ada), F32)]
    return pl.pallas_call(
        functools.partial(_ctx_attn_kernel, nbs=nbs, lag=lag),
        grid=(steps,), in_specs=in_specs, out_specs=out_specs, out_shape=out_shape,
        compiler_params=_params(), name="ctx_attn",
    )(qa, ka, va, qb, kb, vb, cs, ada_w, ada_b.reshape(1, n_ada))


def _na_geometry(rows):
    wh = min(NA_WIN_H, rows)
    row_start = np.clip(np.arange(rows) - wh // 2, 0, rows - wh)
    return wh, row_start


def _na_bias_fill(rpb_ref, hd, o_ref, q_rows, rows):
    wh, row_start = _na_geometry(rows)
    nx = 2 * NA_WIN_W - 1
    qc = lax.broadcasted_iota(jnp.int32, (GRID_W, GRID_W), 0)
    kc = lax.broadcasted_iota(jnp.int32, (GRID_W, GRID_W), 1)
    col_start = jnp.clip(qc - NA_WIN_W // 2, 0, GRID_W - NA_WIN_W)
    col_ok = (kc >= col_start) & (kc < col_start + NA_WIN_W)
    dx = kc - qc + NA_WIN_W - 1
    base = hd * ((2 * NA_WIN_H - 1) * nx)
    tiles = {}
    for dy in range(-(wh - 1), wh):
        t = jnp.zeros((GRID_W, GRID_W), F32)
        for j in range(nx):
            t = jnp.where(dx == j, rpb_ref[base + (dy + NA_WIN_H - 1) * nx + j], t)
        tiles[dy] = jnp.where(col_ok, t * LOG2E, NEG_INF)
    o_ref[...] = jnp.full(o_ref.shape, NEG_INF, F32)
    for li, qr in enumerate(q_rows):
        for kr in range(int(row_start[qr]), int(row_start[qr]) + wh):
            o_ref[0, li * GRID_W:(li + 1) * GRID_W, kr * GRID_W:(kr + 1) * GRID_W] = tiles[kr - qr]


def _na_bias_kernel(rpb_ref, o_ref, *, rows):
    _na_bias_fill(rpb_ref, pl.program_id(0), o_ref, range(rows), rows)


def _na_bias(rpb, seq):
    rows = seq // GRID_W
    return pl.pallas_call(
        functools.partial(_na_bias_kernel, rows=rows),
        grid=(B_HEADS,),
        in_specs=[pl.BlockSpec(memory_space=pltpu.SMEM)],
        out_specs=pl.BlockSpec((1, seq, seq), lambda h: (h, 0, 0)),
        out_shape=jax.ShapeDtypeStruct((B_HEADS, seq, seq), F32),
        compiler_params=_params(), name="na_bias",
    )(rpb.reshape(-1))


def _l0_weight_side(w0, w0o, steps):
    cw = 512
    assert L0_QKV % cw == 0 and L0_WIDTH % cw == 0
    n_c = L0_WIDTH // cw
    r_blk = D_MODEL // steps
    o_blk = pl.BlockSpec((w0o.shape[0] // steps, w0o.shape[1]), lambda i: (i, 0))
    in_specs = [pl.BlockSpec((r_blk, cw), lambda i, k=k: (i, L0_QKV // cw + k)) for k in range(n_c)]
    in_specs.append(o_blk)
    out_specs = [pl.BlockSpec((r_blk, L0_WIDTH), lambda i: (i, 0)), o_blk]
    out_shape = [jax.ShapeDtypeStruct((D_MODEL, L0_WIDTH), BF16),
                 jax.ShapeDtypeStruct(w0o.shape, BF16)]

    def body(ins, outs):
        outs[0][...] = jnp.concatenate([r[...] for r in ins[:n_c]], axis=1).astype(BF16)
        outs[1][...] = ins[n_c][...].astype(BF16)

    return _SideWork(steps, [w0] * n_c + [w0o], in_specs, out_specs, out_shape, body)


def _na_windows(rows, tq):
    wh, row_start = _na_geometry(rows)
    rq = tq // GRID_W
    wins = []
    for qb in range(rows // rq):
        lo = int(row_start[qb * rq]) * GRID_W
        hi = (int(row_start[qb * rq + rq - 1]) + wh) * GRID_W
        wins.append((lo // LANES * LANES, -(-hi // LANES) * LANES))
    return wins


def _na_kernel(q_ref, k_ref, v_ref, kc_ref, vc_ref, bias_ref, o_ref, *, seq, tq, wins, lag):
    nb = q_ref.shape[0] // seq
    past = kc_ref.shape[0] // nb

    def scores():
        for b in range(nb):
            kc = kc_ref[b * past:(b + 1) * past, :]
            vc = _with_ones(vc_ref[b * past:(b + 1) * past, :])
            v = _with_ones(v_ref[b * seq:(b + 1) * seq, :])
            for qb, (lo, hi) in enumerate(wins):
                rs = slice(qb * tq, (qb + 1) * tq)
                rows = slice(b * seq + qb * tq, b * seq + (qb + 1) * tq)
                q = q_ref[rows, :]
                s_loc = _dot_nt(q, k_ref[b * seq + lo:b * seq + hi, :]) + bias_ref[0, rs, lo:hi]
                yield rows, [s_loc, _dot_nt(q, kc)], [v[lo:hi], vc]

    def finish(unit):
        rows, ss, vs = unit
        o_ref[rows, :] = _softmax_pv(ss, vs).astype(BF16)

    _run_lagged(scores(), lag, finish)


def _na(q, k, v, kc, vc, bias, *, batch, seq, tq=512, nb=4, lag=1):
    past = kc.shape[0] // batch
    wins = _na_windows(seq // GRID_W, tq)
    tok = pl.BlockSpec((nb * seq, HEAD_DIM), lambda h, b: (b, h))
    cache = pl.BlockSpec((nb * past, HEAD_DIM), lambda h, b: (b, h))
    return pl.pallas_call(
        functools.partial(_na_kernel, seq=seq, tq=tq, wins=wins, lag=lag),
        grid=(B_HEADS, batch // nb),
        in_specs=[tok, tok, tok, cache, cache,
                  pl.BlockSpec((1, seq, seq), lambda h, b: (h, 0, 0))],
        out_specs=tok,
        out_shape=jax.ShapeDtypeStruct((batch * seq, B_WIDTH), BF16),
        compiler_params=_params(), name="na",
    )(q, k, v, kc, vc, bias)


def _out_kernel(*refs, latent, tm, seq, n_o, final, gate_t, side):
    x_ref, shift_ref, scale_ref, gate_ref, g_ref = refs[:5]
    o_refs = refs[5:5 + n_o]
    wg_ref, wo_ref = refs[5 + n_o:7 + n_o]
    n_fixed = 7 + n_o + (1 if final else 0)
    fn_ref = refs[n_fixed - 1] if final else None
    n_si = side.n_in if side else 0
    y_ref = refs[n_fixed + n_si]
    if side:
        side.body(refs[n_fixed:n_fixed + n_si], refs[n_fixed + n_si + 1:])
    row = (pl.program_id(0) * tm) // seq if latent else 0
    h = _mod_norm(x_ref, g_ref, shift_ref, scale_ref, row)
    gt = _dot_nt(h, wg_ref[...]) if gate_t else _dot(h, wg_ref[...])
    o = jnp.concatenate([r[...] for r in o_refs], axis=-1).astype(F32)
    t = (o * _silu(gt)).astype(BF16)
    y = x_ref[...] + gate_ref[pl.ds(row, 1), :] * _dot(t, wo_ref[...])
    if final:
        y = _rms(y, fn_ref[...])
    y_ref[...] = y


def _out(x, mod, g, os_, wg, wo, fn, *, latent, seq, name, tm=512, gate_t=False, side=None):
    m = x.shape[0]
    width = wo.shape[0]
    final = fn is not None
    in_specs = [pl.BlockSpec((tm, D_MODEL), lambda i: (i, 0))] + _mod_specs(latent, (0, 1, 2))
    in_specs += [_resident((1, D_MODEL))]
    in_specs += [pl.BlockSpec((tm, o.shape[1]), lambda i: (i, 0)) for o in os_]
    in_specs += [_resident(wg.shape), _resident((width, D_MODEL))]
    args = [x, mod, mod, mod, g, *os_, wg, wo]
    if final:
        in_specs.append(_resident((1, D_MODEL)))
        args.append(fn)
    out_specs = [pl.BlockSpec((tm, D_MODEL), lambda i: (i, 0))]
    out_shape = [jax.ShapeDtypeStruct((m, D_MODEL), F32)]
    if side:
        assert side.steps == m // tm
        in_specs += side.in_specs
        args += side.args
        out_specs += side.out_specs
        out_shape += side.out_shape
    res = pl.pallas_call(
        functools.partial(_out_kernel, latent=latent, tm=tm, seq=seq, n_o=len(os_), final=final,
                          gate_t=gate_t, side=side),
        grid=(m // tm,), in_specs=in_specs, out_specs=out_specs, out_shape=out_shape,
        compiler_params=_params(), name=name,
    )(*args)
    return res if side else res[0]


def _in1_kernel(*refs, latent, tm, seq):
    if latent:
        (x_ref, shift_ref, scale_ref, g_ref, w_ref, qan_ref, kvn_ref, wq_ref,
         cos_ref, slo_ref, shi_ref, q_ref, ckv_ref, kr_ref) = refs
    else:
        (x_ref, shift_ref, scale_ref, g_ref, w_ref, qan_ref, kvn_ref, wq_ref,
         q_ref, ckv_ref, kr_ref) = refs
    row = (pl.program_id(0) * tm) // seq if latent else 0
    h = _mod_norm(x_ref, g_ref, shift_ref, scale_ref, row)
    z = _dot_nt(h, w_ref[...])
    q_lat = _rms(z[:, :Q_LORA], qan_ref[...]).astype(BF16)
    q = _dot(q_lat, wq_ref[...])
    kr = z[:, Q_LORA + KV_LORA:]
    if latent:
        cos, slo, shi = cos_ref[...], slo_ref[...], shi_ref[...]
        kr = _rope(kr, cos, slo, shi, QK_ROPE // 4)
    for j in range(MLA_HEADS * MLA_QK // LANES):
        sl = slice(j * LANES, (j + 1) * LANES)
        qj = q[:, sl]
        if latent and j % 2 == 1:
            qj = _rope(qj, cos, slo, shi, QK_ROPE // 4)
        q_ref[:, sl] = (qj * Q_SCALE_L1).astype(BF16)
    ckv_ref[...] = _rms(z[:, Q_LORA:Q_LORA + KV_LORA], kvn_ref[...]).astype(ckv_ref.dtype)
    kr_ref[...] = kr[:, :QK_ROPE].astype(kr_ref.dtype)


def _in1(x, mod, g, w, qan, kvn, wq, rope_tabs, *, latent, seq, tm=512):
    m = x.shape[0]
    row_spec = lambda n: pl.BlockSpec((tm, n), lambda i: (i, 0))
    in_specs = [row_spec(D_MODEL)] + _mod_specs(latent, (0, 1)) + [
        _resident((1, D_MODEL)), _resident(w.shape),
        _resident((1, Q_LORA)), _resident((1, KV_LORA)),
        _resident((Q_LORA, MLA_HEADS * MLA_QK))]
    args = [x, mod, mod, g, w, qan, kvn, wq]
    if latent:
        nblk = seq // tm
        in_specs += [pl.BlockSpec((tm, LANES), lambda i: (i % nblk, 0))] * 3
        args += list(rope_tabs)
    lat_dt = BF16 if latent else F32
    widths = [(MLA_HEADS * MLA_QK, BF16), (KV_LORA, lat_dt), (QK_ROPE, lat_dt)]
    return pl.pallas_call(
        functools.partial(_in1_kernel, latent=latent, tm=tm, seq=seq),
        grid=(m // tm,), in_specs=in_specs,
        out_specs=[row_spec(n) for n, _ in widths],
        out_shape=[jax.ShapeDtypeStruct((m, n), dt) for n, dt in widths],
        compiler_params=_params(), name="in1_lat" if latent else "in1_ctx",
    )(*args)


def _mla_kernel(*refs, n_src, hb, nbs, seq, lag, side):
    q_ref = refs[0]
    src = refs[1:1 + 2 * n_src]
    w_ref = refs[1 + 2 * n_src]
    n_si = side.n_in if side else 0
    o_ref = refs[2 + 2 * n_src + n_si]
    if side:
        side.body(refs[2 + 2 * n_src:2 + 2 * n_src + n_si], refs[3 + 2 * n_src + n_si:])
    ckv = [src[2 * s][...].astype(BF16) for s in range(n_src)]
    krp = []
    for s in range(n_src):
        kr = src[2 * s + 1][...].astype(F32)
        pad = jnp.zeros((kr.shape[0], LANES - QK_ROPE), F32)
        krp.append(jnp.concatenate([kr, pad], axis=1).astype(BF16))
    kvw = QK_NOPE + V_DIM

    def scores():
        for hd in range(hb):
            kv = [_dot(c, w_ref[:, hd * kvw:(hd + 1) * kvw]).astype(BF16) for c in ckv]
            for b in range(nbs):
                rq = slice(b * seq, (b + 1) * seq)
                part = lambda a: a[b * (a.shape[0] // nbs):(b + 1) * (a.shape[0] // nbs)]
                kvb = [part(kv_s) for kv_s in kv]
                kcat = [jnp.concatenate([kv_s[:, :QK_NOPE], part(kr_s)], axis=1)
                        for kv_s, kr_s in zip(kvb, krp)]
                q = q_ref[rq, hd * MLA_QK:(hd + 1) * MLA_QK]
                yield (rq, hd, [_dot_nt(q, kc) for kc in kcat],
                       [_with_ones(kv_s[:, QK_NOPE:]) for kv_s in kvb])

    def finish(unit):
        rq, hd, ss, vs = unit
        o_ref[rq, hd * V_DIM:(hd + 1) * V_DIM] = _softmax_pv(ss, vs).astype(BF16)

    _run_lagged(scores(), lag, finish)


def _mla(q, srcs, w_kv, *, batch, seq, hb, name, nbs=1, lag=0, side=None):
    kvw = QK_NOPE + V_DIM
    in_specs = [pl.BlockSpec((nbs * seq, hb * MLA_QK), lambda b, j: (b, j))]
    args = [q]
    for ckv, kr, t in srcs:
        for a in (ckv, kr):
            wd = a.shape[-1]
            if a.ndim == 3:
                assert nbs == 1
                in_specs.append(pl.BlockSpec((None, t, wd), lambda b, j: (b, 0, 0)))
            else:
                in_specs.append(pl.BlockSpec((nbs * t, wd), lambda b, j: (b, 0)))
            args.append(a)
    in_specs.append(pl.BlockSpec((KV_LORA, hb * kvw), lambda b, j: (0, j)))
    args.append(w_kv)
    out_specs = [pl.BlockSpec((nbs * seq, hb * V_DIM), lambda b, j: (b, j))]
    out_shape = [jax.ShapeDtypeStruct((batch * seq, L1_WIDTH), BF16)]
    if side:
        assert side.steps == batch // nbs and hb == MLA_HEADS
        in_specs += side.in_specs
        args += side.args
        out_specs += side.out_specs
        out_shape += side.out_shape
    res = pl.pallas_call(
        functools.partial(_mla_kernel, n_src=len(srcs), hb=hb, nbs=nbs, seq=seq, lag=lag,
                          side=side),
        grid=(batch // nbs, MLA_HEADS // hb), in_specs=in_specs,
        out_specs=out_specs, out_shape=out_shape,
        compiler_params=_params(), name=name,
    )(*args)
    return res if side else res[0]


def _rope_tables(seq, d):
    a = d // 2
    t = jnp.arange(seq)
    row = (t // GRID_W).astype(F32)
    col = (t % GRID_W).astype(F32)
    inv = 1.0 / (ROPE_THETA ** (jnp.arange(a // 2, dtype=F32) * 2.0 / a))
    ang = jnp.concatenate([row[:, None] * inv] * 2 + [col[:, None] * inv] * 2, axis=-1)
    ang = jnp.tile(ang, (1, LANES // d))
    lower = jnp.tile((jnp.arange(d) % a) < a // 2, LANES // d)[None, :]
    cos, sin = jnp.cos(ang), jnp.sin(ang)
    zero = jnp.zeros_like(sin)
    return cos, jnp.where(lower, -sin, zero), jnp.where(lower, zero, sin)


def kernel(x_prompt, x_sample, cache_l0_a_k, cache_l0_a_v, cache_l0_b_k, cache_l0_b_v, cache_l1_ckv, cache_l1_krope, c, c_ctx, l0_norm, l0_ada_w, l0_ada_b, l0_w_in, l0_q_norm, l0_k_norm, l0_rpb, l0_w_out, l1_norm, l1_ada_w, l1_ada_b, l1_w_in, l1_q_a_norm, l1_w_q_b, l1_kv_a_norm, l1_w_kv_b, l1_w_out, final_norm):
    batch, seq_c, _ = x_prompt.shape
    dec_batch, seq_l, _ = x_sample.shape
    past = cache_l0_a_k.shape[1]

    w0_qkv = _cast_qkv(l0_w_in)

    row2 = lambda v: v.reshape(1, -1)
    cs = jnp.zeros((ADA_ROWS, D_MODEL), F32).at[:dec_batch].set(c).at[8].set(c_ctx)
    mod0 = _adaln(cs, l0_ada_w, l0_ada_b)

    rope_a = _rope_tables(seq_l, HEAD_DIM)
    rope_c = _rope_tables(seq_l, QK_ROPE)

    tm = 512
    xc = x_prompt.reshape(batch * seq_c, D_MODEL)
    xl = x_sample.reshape(dec_batch * seq_l, D_MODEL)
    qa, ka, va, qb, kb, vb, ka_f, va_f, kb_f, vb_f = _in0(
        xc, mod0, row2(l0_norm), w0_qkv, row2(l0_q_norm), row2(l0_k_norm), None,
        latent=False, seq=seq_c, tm=tm)
    side = _l0_weight_side(l0_w_in, l0_w_out, dec_batch * seq_l // tm)
    qa_l, ka_l, va_l, qb_l, kb_l, vb_l, w0_gate, w0_out = _in0(
        xl, mod0, row2(l0_norm), w0_qkv, row2(l0_q_norm), row2(l0_k_norm), rope_a,
        latent=True, seq=seq_l, tm=tm, side=side)

    oa, ob, mod1 = _ctx_attn(qa, ka, va, qb, kb, vb, cs, l1_ada_w, l1_ada_b,
                             batch=batch, seq=seq_c)
    side = _l1_weight_side(l1_w_in.T, l1_w_q_b, l1_w_kv_b, l1_w_out, batch * seq_c // tm)
    xc, w1_all, w1_gate, wq, w_kv, w1_out = _out(
        xc, mod0, row2(l0_norm), [oa, ob], w0_gate, w0_out, None,
        latent=False, seq=seq_c, name="out0_ctx", tm=tm, side=side)
    q1, ckv_f, kr_f = _in1(xc, mod1, row2(l1_norm), w1_all, row2(l1_q_a_norm),
                           row2(l1_kv_a_norm), wq, None, latent=False, seq=seq_c)
    mla_nbs = 2
    side = _cache_side([cache_l0_a_k, cache_l0_a_v, cache_l0_b_k, cache_l0_b_v], batch // mla_nbs)
    oc, ck_a, cv_a, ck_b, cv_b = _mla(
        q1, [(ckv_f, kr_f, seq_c)], w_kv, batch=batch, seq=seq_c,
        hb=MLA_HEADS, nbs=mla_nbs, lag=4, name="mla_ctx", side=side)
    y_prompt = _out(xc, mod1, row2(l1_norm), [oc], w1_gate, w1_out, row2(final_norm),
                    latent=False, seq=seq_c, name="out1_ctx", gate_t=True)

    oa = _attn(qa_l, [(ck_a, cv_a, past), (ka_l, va_l, seq_l)],
               batch=dec_batch, seq=seq_l, heads=A_Q_HEADS, group=A_Q_HEADS // A_KV_HEADS,
               tq=seq_l, name="attn_a_lat")
    bias = _na_bias(l0_rpb, seq_l)
    ob = _na(qb_l, kb_l, vb_l, ck_b, cv_b, bias, batch=dec_batch, seq=seq_l)
    xl = _out(xl, mod0, row2(l0_norm), [oa, ob], w0_gate, w0_out, None,
              latent=True, seq=seq_l, name="out0_lat")
    q1, ckv, kr = _in1(xl, mod1, row2(l1_norm), w1_all, row2(l1_q_a_norm),
                       row2(l1_kv_a_norm), wq, rope_c, latent=True, seq=seq_l)
    oc = _mla(q1, [(cache_l1_ckv, cache_l1_krope, past), (ckv, kr, seq_l)], w_kv,
              batch=dec_batch, seq=seq_l, hb=8, lag=1, name="mla_lat")
    y_sample = _out(xl, mod1, row2(l1_norm), [oc], w1_gate, w1_out, row2(final_norm),
                    latent=True, seq=seq_l, name="out1_lat", gate_t=True)

    return (y_prompt.reshape(batch, seq_c, D_MODEL),
            y_sample.reshape(dec_batch, seq_l, D_MODEL),
            ka_f.reshape(batch, seq_c, A_KV_HEADS, HEAD_DIM),
            va_f.reshape(batch, seq_c, A_KV_HEADS, HEAD_DIM),
            kb_f.reshape(batch, seq_c, B_HEADS, HEAD_DIM),
            vb_f.reshape(batch, seq_c, B_HEADS, HEAD_DIM),
            ckv_f.reshape(batch, seq_c, KV_LORA),
            kr_f.reshape(batch, seq_c, QK_ROPE))
```

```python
import functools
import itertools
import math
from typing import Callable, NamedTuple

import jax
import jax.numpy as jnp
import numpy as np
from jax import lax
from jax.experimental import pallas as pl
from jax.experimental.pallas import tpu as pltpu

LANES = 128
VMEM_LIMIT = 56 * 1024 * 1024
ADA_ROWS = 16
F32 = jnp.float32
BF16 = jnp.bfloat16

D_MODEL = 2048
GRID_W = 64
HEAD_DIM = 128
A_Q_HEADS = 8
A_KV_HEADS = 2
B_HEADS = 8
A_WIDTH = A_Q_HEADS * HEAD_DIM
A_KV_WIDTH = A_KV_HEADS * HEAD_DIM
B_WIDTH = B_HEADS * HEAD_DIM
L0_WIDTH = A_WIDTH + B_WIDTH
L0_QKV = A_WIDTH + 2 * A_KV_WIDTH + 3 * B_WIDTH
NA_WIN_H = 8
NA_WIN_W = 16
MLA_HEADS = 16
Q_LORA = 512
KV_LORA = 512
QK_NOPE = 128
QK_ROPE = 64
V_DIM = 128
L1_WIDTH = MLA_HEADS * V_DIM
L1_LAT = Q_LORA + KV_LORA + QK_ROPE
L1_LAT_PAD = L1_LAT + (LANES - QK_ROPE)
MLA_QK = 2 * LANES
ROPE_THETA = 10000.0
EPS = 1e-6
NEG_INF = -1e30
LOG2E = 1.4426950408889634
Q_SCALE_L0 = HEAD_DIM ** -0.5 * LOG2E
Q_SCALE_L1 = (QK_NOPE + QK_ROPE) ** -0.5 * LOG2E


def _params(vmem=VMEM_LIMIT):
    return pltpu.CompilerParams(vmem_limit_bytes=vmem)


def _resident(shape):
    nd = len(shape)
    return pl.BlockSpec(shape, lambda *_: (0,) * nd, pipeline_mode=pl.Buffered(1))


def _dot(a, b):
    return jnp.dot(a, b, preferred_element_type=F32)


def _dot_nt(a, b):
    return lax.dot_general(a, b, (((1,), (1,)), ((), ())), preferred_element_type=F32)


def _rms(x, g):
    return x * lax.rsqrt(jnp.mean(x * x, axis=-1, keepdims=True) + EPS) * g


def _silu(x):
    return x * jax.nn.sigmoid(x)


def _rope(x, cos, sin_lo, sin_hi, half):
    return (x * cos + pltpu.roll(x, half, 1) * sin_hi
            + pltpu.roll(x, LANES - half, 1) * sin_lo)


def _mod_norm(x_ref, g_ref, shift_ref, scale_ref, row):
    x = x_ref[...]
    h = _rms(x, g_ref[...]) * (1.0 + scale_ref[pl.ds(row, 1), :]) + shift_ref[pl.ds(row, 1), :]
    return h.astype(BF16)


def _with_ones(v):
    v = v.astype(BF16)
    return jnp.concatenate([v, jnp.ones_like(v)], axis=1)


def _softmax_pv(ss, vs):
    mx = ss[0].max(axis=-1, keepdims=True)
    for s in ss[1:]:
        mx = jnp.maximum(mx, s.max(axis=-1, keepdims=True))
    acc = _dot(jnp.exp2(ss[0] - mx).astype(BF16), vs[0])
    for s, v in zip(ss[1:], vs[1:]):
        acc = acc + _dot(jnp.exp2(s - mx).astype(BF16), v)
    return acc[:, :LANES] / acc[:, LANES:]


def _run_lagged(units, lag, finish):
    pending = []
    for unit in units:
        pending.append(unit)
        while len(pending) > lag:
            finish(pending.pop(0))
    for unit in pending:
        finish(unit)


WEIGHT_ROW_BLOCKS = 16


class _SideWork(NamedTuple):
    steps: int
    args: list
    in_specs: list
    out_specs: list
    out_shape: list
    body: Callable

    @property
    def n_in(self):
        return len(self.args)


def _l1_weight_side(w1_t, wq, wkv, w1o, steps):
    g_rows = L1_WIDTH // steps
    g_blk = math.gcd(L1_LAT, g_rows)
    g_n = g_rows // g_blk
    lat_rows = 128
    lat_n = L1_LAT_PAD // lat_rows
    row_blk = lambda a: pl.BlockSpec((a.shape[0] // steps, a.shape[1]), lambda i: (i, 0))
    lat_idx = lambda i: (jnp.minimum(i, lat_n - 1), 0)
    in_specs = [pl.BlockSpec((lat_rows, D_MODEL), lat_idx)]
    in_specs += [pl.BlockSpec((g_blk, D_MODEL), lambda i, k=k: (L1_LAT // g_blk + g_n * i + k, 0))
                 for k in range(g_n)]
    in_specs += [row_blk(wq), row_blk(wkv), row_blk(w1o)]
    args = [w1_t] * (1 + g_n) + [wq, wkv, w1o]
    out_dims = [(L1_LAT_PAD, D_MODEL), (L1_WIDTH, D_MODEL), (Q_LORA, MLA_HEADS * MLA_QK),
                wkv.shape, w1o.shape]
    out_specs = [pl.BlockSpec((lat_rows, D_MODEL), lat_idx)]
    out_specs += [pl.BlockSpec((r // steps, c), lambda i: (i, 0)) for r, c in out_dims[1:]]

    def body(ins, outs):
        lat_in, g_in = ins[0], ins[1:1 + g_n]
        wq_in, wkv_in, w1o_in = ins[1 + g_n:]
        lat_o, g_o, wq_o, wkv_o, w1o_o = outs

        @pl.when(pl.program_id(0) < lat_n)
        def _():
            lat_o[...] = lat_in[...].astype(BF16)

        for k, g_ref in enumerate(g_in):
            g_o[k * g_blk:(k + 1) * g_blk, :] = g_ref[...].astype(BF16)
        qw = QK_NOPE + QK_ROPE
        pad = jnp.zeros((wq_in.shape[0], MLA_QK - qw), F32)
        parts = []
        for hd in range(MLA_HEADS):
            parts += [wq_in[:, hd * qw:(hd + 1) * qw], pad]
        wq_o[...] = jnp.concatenate(parts, axis=1).astype(BF16)
        wkv_o[...] = wkv_in[...].astype(BF16)
        w1o_o[...] = w1o_in[...].astype(BF16)

    return _SideWork(steps, args, in_specs, out_specs,
                     [jax.ShapeDtypeStruct(d, BF16) for d in out_dims], body)


def _adaln_kernel(c_ref, w_ref, b_ref, o_ref):
    c = c_ref[...]
    o_ref[...] = _dot(_silu(c).astype(BF16), w_ref[...].astype(BF16)) + b_ref[...]


def _adaln(cs, w, b):
    tn = 1024
    n = w.shape[1]
    return pl.pallas_call(
        _adaln_kernel,
        grid=(n // tn,),
        in_specs=[pl.BlockSpec((ADA_ROWS, D_MODEL), lambda j: (0, 0)),
                  pl.BlockSpec((D_MODEL, tn), lambda j: (0, j)),
                  pl.BlockSpec((1, tn), lambda j: (0, j))],
        out_specs=pl.BlockSpec((ADA_ROWS, tn), lambda j: (0, j)),
        out_shape=jax.ShapeDtypeStruct((ADA_ROWS, n), F32),
        compiler_params=_params(),
        name="adaln",
    )(cs, w, b.reshape(1, n))


def _mod_specs(latent, cols):
    rb = 0 if latent else 1
    return [pl.BlockSpec((8, D_MODEL), lambda i, c=c: (rb, c)) for c in cols]


def _store_heads(o_ref, y):
    o_ref[...] = y.reshape(o_ref.shape)


def _in0_kernel(*refs, latent, tm, seq, side):
    n_in = 10 if latent else 7
    n_si = side.n_in if side else 0
    n_out = 6 if latent else 10
    outs = refs[n_in + n_si:n_in + n_si + n_out]
    if side:
        side.body(refs[n_in:n_in + n_si], refs[n_in + n_si + n_out:])
    if latent:
        (x_ref, shift_ref, scale_ref, g_ref, w_ref, qn_ref, kn_ref,
         cos_ref, slo_ref, shi_ref) = refs[:n_in]
        qa_ref, ka_ref, va_ref, qb_ref, kb_ref, vb_ref = outs
    else:
        x_ref, shift_ref, scale_ref, g_ref, w_ref, qn_ref, kn_ref = refs[:n_in]
        (qa_ref, ka_ref, va_ref, qb_ref, kb_ref, vb_ref,
         kaf_ref, vaf_ref, kbf_ref, vbf_ref) = outs
    row = (pl.program_id(0) * tm) // seq if latent else 0
    h = _mod_norm(x_ref, g_ref, shift_ref, scale_ref, row)

    def head_norm(y, g_ref_):
        y = _rms(y, g_ref_[...])
        if latent:
            y = _rope(y, cos_ref[...], slo_ref[...], shi_ref[...], HEAD_DIM // 4)
        return y

    c_ka, c_va, c_qb = A_WIDTH, A_WIDTH + A_KV_WIDTH, A_WIDTH + 2 * A_KV_WIDTH
    c_kb, c_vb = c_qb + B_WIDTH, c_qb + 2 * B_WIDTH
    y = _dot(h, w_ref[:, c_kb:c_kb + B_WIDTH])
    kb_ref[...] = y.astype(BF16)
    if not latent:
        _store_heads(kbf_ref, y)
    y = _dot(h, w_ref[:, c_vb:c_vb + B_WIDTH])
    vb_ref[...] = y.astype(BF16)
    if not latent:
        _store_heads(vbf_ref, y)
    y = _dot(h, w_ref[:, c_ka:c_ka + A_KV_WIDTH])
    for hd in range(A_KV_HEADS):
        sl = slice(hd * HEAD_DIM, (hd + 1) * HEAD_DIM)
        k = head_norm(y[:, sl], kn_ref)
        ka_ref[:, sl] = k.astype(BF16)
        if not latent:
            kaf_ref[:, hd, :] = k
    y = _dot(h, w_ref[:, c_va:c_va + A_KV_WIDTH])
    va_ref[...] = y.astype(BF16)
    if not latent:
        _store_heads(vaf_ref, y)
    y = _dot(h, w_ref[:, :A_WIDTH])
    for hd in range(A_Q_HEADS):
        sl = slice(hd * HEAD_DIM, (hd + 1) * HEAD_DIM)
        qa_ref[:, sl] = (head_norm(y[:, sl], qn_ref) * Q_SCALE_L0).astype(BF16)
    y = _dot(h, w_ref[:, c_qb:c_qb + B_WIDTH])
    qb_ref[...] = (y * Q_SCALE_L0).astype(BF16)


def _in0(x, mod, g, w, qn, kn, rope_tabs, *, latent, seq, tm=512, side=None):
    m = x.shape[0]
    row_spec = lambda n: pl.BlockSpec((tm, n), lambda i: (i, 0))
    in_specs = [row_spec(D_MODEL)] + _mod_specs(latent, (0, 1)) + [
        _resident((1, D_MODEL)), _resident((D_MODEL, L0_QKV)),
        _resident((1, HEAD_DIM)), _resident((1, HEAD_DIM))]
    args = [x, mod, mod, g, w, qn, kn]
    if latent:
        nblk = seq // tm
        in_specs += [pl.BlockSpec((tm, LANES), lambda i: (i % nblk, 0))] * 3
        args += list(rope_tabs)
    widths = [A_WIDTH, A_KV_WIDTH, A_KV_WIDTH, B_WIDTH, B_WIDTH, B_WIDTH]
    out_specs = [row_spec(n) for n in widths]
    out_shape = [jax.ShapeDtypeStruct((m, n), BF16) for n in widths]
    if not latent:
        for nh in (A_KV_HEADS, A_KV_HEADS, B_HEADS, B_HEADS):
            out_specs.append(pl.BlockSpec((tm, nh, HEAD_DIM), lambda i: (i, 0, 0)))
            out_shape.append(jax.ShapeDtypeStruct((m, nh, HEAD_DIM), F32))
    if side:
        assert side.steps == m // tm
        in_specs += side.in_specs
        args += side.args
        out_specs += side.out_specs
        out_shape += side.out_shape
    return pl.pallas_call(
        functools.partial(_in0_kernel, latent=latent, tm=tm, seq=seq, side=side),
        grid=(m // tm,), in_specs=in_specs, out_specs=out_specs, out_shape=out_shape,
        compiler_params=_params(), name="in0_lat" if latent else "in0_ctx",
    )(*args)


def _attn_units(q_ref, kv_refs, o_ref, *, heads, group, sub, nbs=1):
    seq = q_ref.shape[0] // nbs
    for b in range(nbs):
        for j in range(heads // group):
            sl_kv = slice(j * HEAD_DIM, (j + 1) * HEAD_DIM)
            rows = lambda r: slice(b * (r.shape[0] // nbs), (b + 1) * (r.shape[0] // nbs))
            ks = [k_ref[rows(k_ref), sl_kv].astype(BF16) for k_ref, _ in kv_refs]
            vs = [_with_ones(v_ref[rows(v_ref), sl_kv]) for _, v_ref in kv_refs]
            sls = [slice((j * group + gi) * HEAD_DIM, (j * group + gi + 1) * HEAD_DIM)
                   for gi in range(group)]
            for r0 in range(b * seq, (b + 1) * seq, sub):
                q = jnp.concatenate([q_ref[r0:r0 + sub, sl] for sl in sls], axis=0)
                yield o_ref, r0, sub, sls, [_dot_nt(q, k) for k in ks], vs


def _attn_finish(unit):
    o_ref, r0, sub, sls, ss, vs = unit
    o = _softmax_pv(ss, vs).astype(BF16)
    for gi, sl in enumerate(sls):
        o_ref[r0:r0 + sub, sl] = o[gi * sub:(gi + 1) * sub]


def _attn_kernel(*refs, n_src, heads, group, sub, lag):
    q_ref = refs[0]
    kv_refs = [(refs[1 + 2 * s], refs[2 + 2 * s]) for s in range(n_src)]
    o_ref = refs[1 + 2 * n_src]
    _run_lagged(_attn_units(q_ref, kv_refs, o_ref, heads=heads, group=group, sub=sub),
                lag, _attn_finish)


def _attn(q, srcs, *, batch, seq, heads, group, name, lag=1):
    kvw = (heads // group) * HEAD_DIM
    in_specs = [pl.BlockSpec((seq, heads * HEAD_DIM), lambda b: (b, 0))]
    args = [q]
    for k, v, t in srcs:
        in_specs += [pl.BlockSpec((t, kvw), lambda b: (b, 0))] * 2
        args += [k, v]
    return pl.pallas_call(
        functools.partial(_attn_kernel, n_src=len(srcs), heads=heads, group=group,
                          sub=min(seq, 256), lag=lag),
        grid=(batch,), in_specs=in_specs,
        out_specs=pl.BlockSpec((seq, heads * HEAD_DIM), lambda b: (b, 0)),
        out_shape=jax.ShapeDtypeStruct((batch * seq, heads * HEAD_DIM), BF16),
        compiler_params=_params(), name=name,
    )(*args)


def _cache_side(caches, steps):
    dec_batch, past = caches[0].shape[:2]
    spb = steps // dec_batch
    rows = past // spb
    in_specs = [pl.BlockSpec((None, rows) + c.shape[2:],
                             lambda i, *_: (i // spb, i % spb, 0, 0)) for c in caches]
    widths = [c.shape[2] * HEAD_DIM for c in caches]
    out_specs = [pl.BlockSpec((rows, w), lambda i, *_: (i, 0)) for w in widths]
    out_shape = [jax.ShapeDtypeStruct((dec_batch * past, w), BF16) for w in widths]

    def body(ins, outs):
        for c_ref, o_ref in zip(ins, outs):
            o_ref[...] = c_ref[...].reshape(o_ref.shape).astype(BF16)

    return _SideWork(steps, list(caches), in_specs, out_specs, out_shape, body)


def _ctx_attn_kernel(qa_ref, ka_ref, va_ref, qb_ref, kb_ref, vb_ref, oa_ref, ob_ref, *, nbs, lag):
    sub = min(qa_ref.shape[0] // nbs, 256)
    units = itertools.chain(
        _attn_units(qa_ref, [(ka_ref, va_ref)], oa_ref, heads=A_Q_HEADS,
                    group=A_Q_HEADS // A_KV_HEADS, sub=sub, nbs=nbs),
        _attn_units(qb_ref, [(kb_ref, vb_ref)], ob_ref, heads=B_HEADS, group=1, sub=sub, nbs=nbs))
    _run_lagged(units, lag, _attn_finish)


def _ctx_attn(qa, ka, va, qb, kb, vb, *, batch, seq, nbs=2, lag=4):
    tok = lambda n: pl.BlockSpec((nbs * seq, n), lambda i: (i, 0))
    return pl.pallas_call(
        functools.partial(_ctx_attn_kernel, nbs=nbs, lag=lag),
        grid=(batch // nbs,),
        in_specs=[tok(A_WIDTH), tok(A_KV_WIDTH), tok(A_KV_WIDTH), tok(B_WIDTH), tok(B_WIDTH),
                  tok(B_WIDTH)],
        out_specs=[tok(A_WIDTH), tok(B_WIDTH)],
        out_shape=[jax.ShapeDtypeStruct((batch * seq, A_WIDTH), BF16),
                   jax.ShapeDtypeStruct((batch * seq, B_WIDTH), BF16)],
        compiler_params=_params(), name="ctx_attn",
    )(qa, ka, va, qb, kb, vb)


def _na_geometry(rows):
    wh = min(NA_WIN_H, rows)
    row_start = np.clip(np.arange(rows) - wh // 2, 0, rows - wh)
    return wh, row_start


def _na_bias_fill(rpb_ref, hd, o_ref, q_rows, rows):
    wh, row_start = _na_geometry(rows)
    nx = 2 * NA_WIN_W - 1
    qc = lax.broadcasted_iota(jnp.int32, (GRID_W, GRID_W), 0)
    kc = lax.broadcasted_iota(jnp.int32, (GRID_W, GRID_W), 1)
    col_start = jnp.clip(qc - NA_WIN_W // 2, 0, GRID_W - NA_WIN_W)
    col_ok = (kc >= col_start) & (kc < col_start + NA_WIN_W)
    dx = kc - qc + NA_WIN_W - 1
    base = hd * ((2 * NA_WIN_H - 1) * nx)
    tiles = {}
    for dy in range(-(wh - 1), wh):
        t = jnp.zeros((GRID_W, GRID_W), F32)
        for j in range(nx):
            t = jnp.where(dx == j, rpb_ref[base + (dy + NA_WIN_H - 1) * nx + j], t)
        tiles[dy] = jnp.where(col_ok, t * LOG2E, NEG_INF)
    o_ref[...] = jnp.full(o_ref.shape, NEG_INF, F32)
    for li, qr in enumerate(q_rows):
        for kr in range(int(row_start[qr]), int(row_start[qr]) + wh):
            o_ref[0, li * GRID_W:(li + 1) * GRID_W, kr * GRID_W:(kr + 1) * GRID_W] = tiles[kr - qr]


def _prep0_kernel(w_ref, rpb_ref, o_ref, bias_ref, *, rows, sph):
    o_ref[...] = w_ref[...].astype(BF16)
    i = pl.program_id(0)
    rps = rows // sph
    for part in range(sph):
        @pl.when(i % sph == part)
        def _():
            _na_bias_fill(rpb_ref, i // sph, bias_ref, range(part * rps, (part + 1) * rps), rows)


def _prep0(w0, rpb, seq):
    nb = WEIGHT_ROW_BLOCKS
    sph = nb // B_HEADS
    rows = seq // GRID_W
    wr = w0.shape[0] // nb
    return pl.pallas_call(
        functools.partial(_prep0_kernel, rows=rows, sph=sph),
        grid=(nb,),
        in_specs=[pl.BlockSpec((wr, L0_QKV), lambda i: (i, 0)),
                  pl.BlockSpec(memory_space=pltpu.SMEM)],
        out_specs=[pl.BlockSpec((wr, L0_QKV), lambda i: (i, 0)),
                   pl.BlockSpec((1, seq // sph, seq), lambda i: (i // sph, i % sph, 0))],
        out_shape=[jax.ShapeDtypeStruct((w0.shape[0], L0_QKV), BF16),
                   jax.ShapeDtypeStruct((B_HEADS, seq, seq), F32)],
        compiler_params=_params(), name="prep0",
    )(w0, rpb.reshape(-1))


def _l0_side(w0, w0o, cs, ada_w, ada_b, steps):
    cw = 512
    assert L0_QKV % cw == 0 and L0_WIDTH % cw == 0
    n_c = L0_WIDTH // cw
    r_blk = D_MODEL // steps
    n_ada = ada_w.shape[1]
    tn = n_ada // steps
    o_blk = pl.BlockSpec((w0o.shape[0] // steps, w0o.shape[1]), lambda i: (i, 0))
    in_specs = [pl.BlockSpec((r_blk, cw), lambda i, k=k: (i, L0_QKV // cw + k)) for k in range(n_c)]
    in_specs += [o_blk, _resident((ADA_ROWS, D_MODEL)),
                 pl.BlockSpec((D_MODEL, tn), lambda i: (0, i)),
                 pl.BlockSpec((1, tn), lambda i: (0, i))]
    out_specs = [pl.BlockSpec((r_blk, L0_WIDTH), lambda i: (i, 0)), o_blk,
                 pl.BlockSpec((ADA_ROWS, tn), lambda i: (0, i))]
    out_shape = [jax.ShapeDtypeStruct((D_MODEL, L0_WIDTH), BF16),
                 jax.ShapeDtypeStruct(w0o.shape, BF16),
                 jax.ShapeDtypeStruct((ADA_ROWS, n_ada), F32)]

    def body(ins, outs):
        outs[0][...] = jnp.concatenate([r[...] for r in ins[:n_c]], axis=1).astype(BF16)
        outs[1][...] = ins[n_c][...].astype(BF16)
        c_ref, aw_ref, ab_ref = ins[n_c + 1:]
        outs[2][...] = (_dot(_silu(c_ref[...]).astype(BF16), aw_ref[...].astype(BF16))
                        + ab_ref[...])

    args = [w0] * n_c + [w0o, cs, ada_w, ada_b.reshape(1, n_ada)]
    return _SideWork(steps, args, in_specs, out_specs, out_shape, body)


def _na_windows(rows, tq):
    wh, row_start = _na_geometry(rows)
    rq = tq // GRID_W
    wins = []
    for qb in range(rows // rq):
        lo = int(row_start[qb * rq]) * GRID_W
        hi = (int(row_start[qb * rq + rq - 1]) + wh) * GRID_W
        wins.append((lo // LANES * LANES, -(-hi // LANES) * LANES))
    return wins


def _na_kernel(q_ref, k_ref, v_ref, kc_ref, vc_ref, bias_ref, o_ref, *, seq, tq, wins, lag):
    nb = q_ref.shape[0] // seq
    past = kc_ref.shape[0] // nb

    def scores():
        for b in range(nb):
            kc = kc_ref[b * past:(b + 1) * past, :]
            vc = _with_ones(vc_ref[b * past:(b + 1) * past, :])
            v = _with_ones(v_ref[b * seq:(b + 1) * seq, :])
            for qb, (lo, hi) in enumerate(wins):
                rs = slice(qb * tq, (qb + 1) * tq)
                rows = slice(b * seq + qb * tq, b * seq + (qb + 1) * tq)
                q = q_ref[rows, :]
                s_loc = _dot_nt(q, k_ref[b * seq + lo:b * seq + hi, :]) + bias_ref[0, rs, lo:hi]
                yield rows, [s_loc, _dot_nt(q, kc)], [v[lo:hi], vc]

    def finish(unit):
        rows, ss, vs = unit
        o_ref[rows, :] = _softmax_pv(ss, vs).astype(BF16)

    _run_lagged(scores(), lag, finish)


def _na(q, k, v, kc, vc, bias, *, batch, seq, tq=256, nb=4, lag=1):
    past = kc.shape[0] // batch
    wins = _na_windows(seq // GRID_W, tq)
    tok = pl.BlockSpec((nb * seq, HEAD_DIM), lambda h, b: (b, h))
    cache = pl.BlockSpec((nb * past, HEAD_DIM), lambda h, b: (b, h))
    return pl.pallas_call(
        functools.partial(_na_kernel, seq=seq, tq=tq, wins=wins, lag=lag),
        grid=(B_HEADS, batch // nb),
        in_specs=[tok, tok, tok, cache, cache,
                  pl.BlockSpec((1, seq, seq), lambda h, b: (h, 0, 0))],
        out_specs=tok,
        out_shape=jax.ShapeDtypeStruct((batch * seq, B_WIDTH), BF16),
        compiler_params=_params(), name="na",
    )(q, k, v, kc, vc, bias)


def _out_kernel(*refs, latent, tm, seq, n_o, final, gate_t, side):
    x_ref, shift_ref, scale_ref, gate_ref, g_ref = refs[:5]
    o_refs = refs[5:5 + n_o]
    wg_ref, wo_ref = refs[5 + n_o:7 + n_o]
    n_fixed = 7 + n_o + (1 if final else 0)
    fn_ref = refs[n_fixed - 1] if final else None
    n_si = side.n_in if side else 0
    y_ref = refs[n_fixed + n_si]
    if side:
        side.body(refs[n_fixed:n_fixed + n_si], refs[n_fixed + n_si + 1:])
    row = (pl.program_id(0) * tm) // seq if latent else 0
    h = _mod_norm(x_ref, g_ref, shift_ref, scale_ref, row)
    gt = _dot_nt(h, wg_ref[...]) if gate_t else _dot(h, wg_ref[...])
    o = jnp.concatenate([r[...] for r in o_refs], axis=-1).astype(F32)
    t = (o * _silu(gt)).astype(BF16)
    y = x_ref[...] + gate_ref[pl.ds(row, 1), :] * _dot(t, wo_ref[...])
    if final:
        y = _rms(y, fn_ref[...])
    y_ref[...] = y


def _out(x, mod, g, os_, wg, wo, fn, *, latent, seq, name, tm=512, gate_t=False, side=None):
    m = x.shape[0]
    width = wo.shape[0]
    final = fn is not None
    in_specs = [pl.BlockSpec((tm, D_MODEL), lambda i: (i, 0))] + _mod_specs(latent, (0, 1, 2))
    in_specs += [_resident((1, D_MODEL))]
    in_specs += [pl.BlockSpec((tm, o.shape[1]), lambda i: (i, 0)) for o in os_]
    in_specs += [_resident(wg.shape), _resident((width, D_MODEL))]
    args = [x, mod, mod, mod, g, *os_, wg, wo]
    if final:
        in_specs.append(_resident((1, D_MODEL)))
        args.append(fn)
    out_specs = [pl.BlockSpec((tm, D_MODEL), lambda i: (i, 0))]
    out_shape = [jax.ShapeDtypeStruct((m, D_MODEL), F32)]
    if side:
        assert side.steps == m // tm
        in_specs += side.in_specs
        args += side.args
        out_specs += side.out_specs
        out_shape += side.out_shape
    res = pl.pallas_call(
        functools.partial(_out_kernel, latent=latent, tm=tm, seq=seq, n_o=len(os_), final=final,
                          gate_t=gate_t, side=side),
        grid=(m // tm,), in_specs=in_specs, out_specs=out_specs, out_shape=out_shape,
        compiler_params=_params(), name=name,
    )(*args)
    return res if side else res[0]


def _in1_body(x_ref, shift_ref, scale_ref, g_ref, w_ref, qan_ref, kvn_ref, wq_ref, rope_refs,
              q_ref, ckv_ref, kr_ref, row):
    h = _mod_norm(x_ref, g_ref, shift_ref, scale_ref, row)
    z = _dot_nt(h, w_ref[...])
    q_lat = _rms(z[:, :Q_LORA], qan_ref[...]).astype(BF16)
    q = _dot(q_lat, wq_ref[...])
    kr = z[:, Q_LORA + KV_LORA:]
    if rope_refs:
        cos, slo, shi = (r[...] for r in rope_refs)
        kr = _rope(kr, cos, slo, shi, QK_ROPE // 4)
    for j in range(MLA_HEADS * MLA_QK // LANES):
        sl = slice(j * LANES, (j + 1) * LANES)
        qj = q[:, sl]
        if rope_refs and j % 2 == 1:
            qj = _rope(qj, cos, slo, shi, QK_ROPE // 4)
        q_ref[:, sl] = (qj * Q_SCALE_L1).astype(BF16)
    ckv_ref[...] = _rms(z[:, Q_LORA:Q_LORA + KV_LORA], kvn_ref[...]).astype(ckv_ref.dtype)
    kr_ref[...] = kr[:, :QK_ROPE].astype(kr_ref.dtype)


def _in1_kernel(*refs, latent, tm, seq):
    n_in = 11 if latent else 8
    rope_refs = refs[8:11] if latent else None
    row = (pl.program_id(0) * tm) // seq if latent else 0
    _in1_body(*refs[:8], rope_refs, *refs[n_in:], row)


def _in1(x, mod, g, w, qan, kvn, wq, rope_tabs, *, latent, seq, tm=512):
    m = x.shape[0]
    row_spec = lambda n: pl.BlockSpec((tm, n), lambda i: (i, 0))
    in_specs = [row_spec(D_MODEL)] + _mod_specs(latent, (0, 1)) + [
        _resident((1, D_MODEL)), _resident(w.shape),
        _resident((1, Q_LORA)), _resident((1, KV_LORA)),
        _resident((Q_LORA, MLA_HEADS * MLA_QK))]
    args = [x, mod, mod, g, w, qan, kvn, wq]
    if latent:
        nblk = seq // tm
        in_specs += [pl.BlockSpec((tm, LANES), lambda i: (i % nblk, 0))] * 3
        args += list(rope_tabs)
    lat_dt = BF16 if latent else F32
    widths = [(MLA_HEADS * MLA_QK, BF16), (KV_LORA, lat_dt), (QK_ROPE, lat_dt)]
    return pl.pallas_call(
        functools.partial(_in1_kernel, latent=latent, tm=tm, seq=seq),
        grid=(m // tm,), in_specs=in_specs,
        out_specs=[row_spec(n) for n, _ in widths],
        out_shape=[jax.ShapeDtypeStruct((m, n), dt) for n, dt in widths],
        compiler_params=_params(), name="in1_lat" if latent else "in1_ctx",
    )(*args)


def _mla_body(q_ref, src, w_ref, o_ref, *, hb, nbs, seq, lag):
    n_src = len(src) // 2
    ckv = [src[2 * s][...].astype(BF16) for s in range(n_src)]
    krp = []
    for s in range(n_src):
        kr = src[2 * s + 1][...].astype(F32)
        pad = jnp.zeros((kr.shape[0], LANES - QK_ROPE), F32)
        krp.append(jnp.concatenate([kr, pad], axis=1).astype(BF16))
    kvw = QK_NOPE + V_DIM

    def scores():
        for hd in range(hb):
            kv = [_dot(c, w_ref[:, hd * kvw:(hd + 1) * kvw]).astype(BF16) for c in ckv]
            for b in range(nbs):
                rq = slice(b * seq, (b + 1) * seq)
                part = lambda a: a[b * (a.shape[0] // nbs):(b + 1) * (a.shape[0] // nbs)]
                kvb = [part(kv_s) for kv_s in kv]
                kcat = [jnp.concatenate([kv_s[:, :QK_NOPE], part(kr_s)], axis=1)
                        for kv_s, kr_s in zip(kvb, krp)]
                q = q_ref[rq, hd * MLA_QK:(hd + 1) * MLA_QK]
                yield (rq, hd, [_dot_nt(q, kc) for kc in kcat],
                       [_with_ones(kv_s[:, QK_NOPE:]) for kv_s in kvb])

    def finish(unit):
        rq, hd, ss, vs = unit
        o_ref[rq, hd * V_DIM:(hd + 1) * V_DIM] = _softmax_pv(ss, vs).astype(BF16)

    _run_lagged(scores(), lag, finish)


def _mla_kernel(*refs, n_src, hb, nbs, seq, lag):
    _mla_body(refs[0], refs[1:1 + 2 * n_src], refs[1 + 2 * n_src], refs[2 + 2 * n_src],
              hb=hb, nbs=nbs, seq=seq, lag=lag)


def _mla(q, srcs, w_kv, *, batch, seq, hb, name, lag=1):
    kvw = QK_NOPE + V_DIM
    in_specs = [pl.BlockSpec((seq, hb * MLA_QK), lambda b, j: (b, j))]
    args = [q]
    for ckv, kr, t in srcs:
        for a in (ckv, kr):
            wd = a.shape[-1]
            if a.ndim == 3:
                in_specs.append(pl.BlockSpec((None, t, wd), lambda b, j: (b, 0, 0)))
            else:
                in_specs.append(pl.BlockSpec((t, wd), lambda b, j: (b, 0)))
            args.append(a)
    in_specs.append(pl.BlockSpec((KV_LORA, hb * kvw), lambda b, j: (0, j)))
    args.append(w_kv)
    return pl.pallas_call(
        functools.partial(_mla_kernel, n_src=len(srcs), hb=hb, nbs=1, seq=seq, lag=lag),
        grid=(batch, MLA_HEADS // hb), in_specs=in_specs,
        out_specs=pl.BlockSpec((seq, hb * V_DIM), lambda b, j: (b, j)),
        out_shape=jax.ShapeDtypeStruct((batch * seq, L1_WIDTH), BF16),
        compiler_params=_params(), name=name,
    )(*args)


def _l1_ctx_kernel(*refs, nbs, seq, lag, side):
    fixed = refs[:9]
    n_si = side.n_in
    o_ref, ckv_ref, kr_ref = refs[9 + n_si:12 + n_si]
    q_scr = refs[-1]
    side.body(refs[9:9 + n_si], refs[12 + n_si:-1])
    _in1_body(*fixed[:8], None, q_scr, ckv_ref, kr_ref, 0)
    _mla_body(q_scr, [ckv_ref, kr_ref], fixed[8], o_ref, hb=MLA_HEADS, nbs=nbs, seq=seq, lag=lag)


def _l1_ctx(x, mod, g, w, qan, kvn, wq, w_kv, side, *, seq, nbs=2, lag=4):
    m = x.shape[0]
    tm = nbs * seq
    row_spec = lambda n: pl.BlockSpec((tm, n), lambda i: (i, 0))
    in_specs = [row_spec(D_MODEL)] + _mod_specs(False, (0, 1)) + [
        _resident((1, D_MODEL)), _resident(w.shape),
        _resident((1, Q_LORA)), _resident((1, KV_LORA)),
        _resident(wq.shape), _resident(w_kv.shape)]
    assert side.steps == m // tm
    widths = [(L1_WIDTH, BF16), (KV_LORA, F32), (QK_ROPE, F32)]
    return pl.pallas_call(
        functools.partial(_l1_ctx_kernel, nbs=nbs, seq=seq, lag=lag, side=side),
        grid=(m // tm,), in_specs=in_specs + side.in_specs,
        out_specs=[row_spec(n) for n, _ in widths] + side.out_specs,
        out_shape=[jax.ShapeDtypeStruct((m, n), dt) for n, dt in widths] + side.out_shape,
        scratch_shapes=[pltpu.VMEM((tm, MLA_HEADS * MLA_QK), BF16)],
        compiler_params=_params(), name="l1_ctx",
    )(x, mod, mod, g, w, qan, kvn, wq, w_kv, *side.args)


def _rope_tables(seq, d):
    a = d // 2
    t = np.arange(seq)
    row = (t // GRID_W).astype(np.float64)
    col = (t % GRID_W).astype(np.float64)
    inv = 1.0 / (ROPE_THETA ** (np.arange(a // 2, dtype=np.float64) * 2.0 / a))
    ang = np.concatenate([row[:, None] * inv] * 2 + [col[:, None] * inv] * 2, axis=-1)
    ang = np.tile(ang, (1, LANES // d))
    lower = np.tile((np.arange(d) % a) < a // 2, LANES // d)[None, :]
    cos, sin = np.cos(ang), np.sin(ang)
    tabs = (cos, np.where(lower, -sin, 0.0), np.where(lower, 0.0, sin))
    return tuple(jnp.asarray(x, F32) for x in tabs)


def kernel(x_prompt, x_sample, cache_l0_a_k, cache_l0_a_v, cache_l0_b_k, cache_l0_b_v, cache_l1_ckv, cache_l1_krope, c, c_ctx, l0_norm, l0_ada_w, l0_ada_b, l0_w_in, l0_q_norm, l0_k_norm, l0_rpb, l0_w_out, l1_norm, l1_ada_w, l1_ada_b, l1_w_in, l1_q_a_norm, l1_w_q_b, l1_kv_a_norm, l1_w_kv_b, l1_w_out, final_norm):
    batch, seq_c, _ = x_prompt.shape
    dec_batch, seq_l, _ = x_sample.shape
    past = cache_l0_a_k.shape[1]

    w0_qkv, bias = _prep0(l0_w_in, l0_rpb, seq_l)

    row2 = lambda v: v.reshape(1, -1)
    cs = jnp.zeros((ADA_ROWS, D_MODEL), F32).at[:dec_batch].set(c).at[8].set(c_ctx)
    mod0 = _adaln(cs, l0_ada_w, l0_ada_b)

    rope_a = _rope_tables(seq_l, HEAD_DIM)
    rope_c = _rope_tables(seq_l, QK_ROPE)

    tm = 512
    xc = x_prompt.reshape(batch * seq_c, D_MODEL)
    xl = x_sample.reshape(dec_batch * seq_l, D_MODEL)
    qa, ka, va, qb, kb, vb, ka_f, va_f, kb_f, vb_f = _in0(
        xc, mod0, row2(l0_norm), w0_qkv, row2(l0_q_norm), row2(l0_k_norm), None,
        latent=False, seq=seq_c, tm=tm)
    side = _l0_side(l0_w_in, l0_w_out, cs, l1_ada_w, l1_ada_b, dec_batch * seq_l // tm)
    qa_l, ka_l, va_l, qb_l, kb_l, vb_l, w0_gate, w0_out, mod1 = _in0(
        xl, mod0, row2(l0_norm), w0_qkv, row2(l0_q_norm), row2(l0_k_norm), rope_a,
        latent=True, seq=seq_l, tm=tm, side=side)

    oa, ob = _ctx_attn(qa, ka, va, qb, kb, vb, batch=batch, seq=seq_c)
    side = _l1_weight_side(l1_w_in.T, l1_w_q_b, l1_w_kv_b, l1_w_out, batch * seq_c // tm)
    xc, w1_all, w1_gate, wq, w_kv, w1_out = _out(
        xc, mod0, row2(l0_norm), [oa, ob], w0_gate, w0_out, None,
        latent=False, seq=seq_c, name="out0_ctx", tm=tm, side=side)
    l1_nbs = tm // seq_c
    side = _cache_side([cache_l0_a_k, cache_l0_a_v, cache_l0_b_k, cache_l0_b_v], batch // l1_nbs)
    oc, ckv_f, kr_f, ck_a, cv_a, ck_b, cv_b = _l1_ctx(
        xc, mod1, row2(l1_norm), w1_all, row2(l1_q_a_norm), row2(l1_kv_a_norm), wq, w_kv, side,
        seq=seq_c, nbs=l1_nbs)
    y_prompt = _out(xc, mod1, row2(l1_norm), [oc], w1_gate, w1_out, row2(final_norm),
                    latent=False, seq=seq_c, name="out1_ctx", gate_t=True)

    oa = _attn(qa_l, [(ck_a, cv_a, past), (ka_l, va_l, seq_l)],
               batch=dec_batch, seq=seq_l, heads=A_Q_HEADS, group=A_Q_HEADS // A_KV_HEADS,
               name="attn_a_lat")
    ob = _na(qb_l, kb_l, vb_l, ck_b, cv_b, bias, batch=dec_batch, seq=seq_l)
    xl = _out(xl, mod0, row2(l0_norm), [oa, ob], w0_gate, w0_out, None,
              latent=True, seq=seq_l, name="out0_lat")
    q1, ckv, kr = _in1(xl, mod1, row2(l1_norm), w1_all, row2(l1_q_a_norm),
                       row2(l1_kv_a_norm), wq, rope_c, latent=True, seq=seq_l)
    oc = _mla(q1, [(cache_l1_ckv, cache_l1_krope, past), (ckv, kr, seq_l)], w_kv,
              batch=dec_batch, seq=seq_l, hb=8, lag=1, name="mla_lat")
    y_sample = _out(xl, mod1, row2(l1_norm), [oc], w1_gate, w1_out, row2(final_norm),
                    latent=True, seq=seq_l, name="out1_lat", gate_t=True)

    return (y_prompt.reshape(batch, seq_c, D_MODEL),
            y_sample.reshape(dec_batch, seq_l, D_MODEL),
            ka_f.reshape(batch, seq_c, A_KV_HEADS, HEAD_DIM),
            va_f.reshape(batch, seq_c, A_KV_HEADS, HEAD_DIM),
            kb_f.reshape(batch, seq_c, B_HEADS, HEAD_DIM),
            vb_f.reshape(batch, seq_c, B_HEADS, HEAD_DIM),
            ckv_f.reshape(batch, seq_c, KV_LORA),
            kr_f.reshape(batch, seq_c, QK_ROPE))
```

```python
import functools
import itertools
import math
from typing import Callable, NamedTuple

import jax
import jax.numpy as jnp
import numpy as np
from jax import lax
from jax.experimental import pallas as pl
from jax.experimental.pallas import tpu as pltpu

LANES = 128
VMEM_LIMIT = 56 * 1024 * 1024
ADA_ROWS = 16
F32 = jnp.float32
BF16 = jnp.bfloat16

D_MODEL = 2048
GRID_W = 64
HEAD_DIM = 128
A_Q_HEADS = 8
A_KV_HEADS = 2
B_HEADS = 8
A_WIDTH = A_Q_HEADS * HEAD_DIM
A_KV_WIDTH = A_KV_HEADS * HEAD_DIM
B_WIDTH = B_HEADS * HEAD_DIM
L0_WIDTH = A_WIDTH + B_WIDTH
L0_QKV = A_WIDTH + 2 * A_KV_WIDTH + 3 * B_WIDTH
NA_WIN_H = 8
NA_WIN_W = 16
MLA_HEADS = 16
Q_LORA = 512
KV_LORA = 512
QK_NOPE = 128
QK_ROPE = 64
V_DIM = 128
L1_WIDTH = MLA_HEADS * V_DIM
L1_LAT = Q_LORA + KV_LORA + QK_ROPE
L1_LAT_PAD = L1_LAT + (LANES - QK_ROPE)
MLA_QK = 2 * LANES
ROPE_THETA = 10000.0
EPS = 1e-6
NEG_INF = -1e30
LOG2E = 1.4426950408889634
Q_SCALE_L0 = HEAD_DIM ** -0.5 * LOG2E
Q_SCALE_L1 = (QK_NOPE + QK_ROPE) ** -0.5 * LOG2E


def _params(vmem=VMEM_LIMIT):
    return pltpu.CompilerParams(vmem_limit_bytes=vmem)


def _resident(shape):
    nd = len(shape)
    return pl.BlockSpec(shape, lambda *_: (0,) * nd, pipeline_mode=pl.Buffered(1))


def _dot(a, b):
    return jnp.dot(a, b, preferred_element_type=F32)


def _dot_nt(a, b):
    return lax.dot_general(a, b, (((1,), (1,)), ((), ())), preferred_element_type=F32)


def _rms(x, g):
    return x * lax.rsqrt(jnp.mean(x * x, axis=-1, keepdims=True) + EPS) * g


def _silu(x):
    return x * jax.nn.sigmoid(x)


def _rope(x, cos, sin_lo, sin_hi, half):
    return (x * cos + pltpu.roll(x, half, 1) * sin_hi
            + pltpu.roll(x, LANES - half, 1) * sin_lo)


def _mod_norm(x_ref, g_ref, shift_ref, scale_ref, row):
    x = x_ref[...]
    h = _rms(x, g_ref[...]) * (1.0 + scale_ref[pl.ds(row, 1), :]) + shift_ref[pl.ds(row, 1), :]
    return h.astype(BF16)


def _with_ones(v):
    v = v.astype(BF16)
    return jnp.concatenate([v, jnp.ones_like(v)], axis=1)


def _softmax_pv(ss, vs):
    mx = ss[0].max(axis=-1, keepdims=True)
    for s in ss[1:]:
        mx = jnp.maximum(mx, s.max(axis=-1, keepdims=True))
    acc = _dot(jnp.exp2(ss[0] - mx).astype(BF16), vs[0])
    for s, v in zip(ss[1:], vs[1:]):
        acc = acc + _dot(jnp.exp2(s - mx).astype(BF16), v)
    return acc[:, :LANES] / acc[:, LANES:]


def _run_lagged(units, lag, finish):
    pending = []
    for unit in units:
        pending.append(unit)
        while len(pending) > lag:
            finish(pending.pop(0))
    for unit in pending:
        finish(unit)


WEIGHT_ROW_BLOCKS = 8


class _SideWork(NamedTuple):
    steps: int
    args: list
    in_specs: list
    out_specs: list
    out_shape: list
    body: Callable

    @property
    def n_in(self):
        return len(self.args)


def _l1_weight_side(w1_t, wq, wkv, w1o, steps):
    g_rows = L1_WIDTH // steps
    g_blk = math.gcd(L1_LAT, g_rows)
    g_n = g_rows // g_blk
    lat_rows = 128
    lat_n = L1_LAT_PAD // lat_rows
    row_blk = lambda a: pl.BlockSpec((a.shape[0] // steps, a.shape[1]), lambda i: (i, 0))
    lat_idx = lambda i: (jnp.minimum(i, lat_n - 1), 0)
    in_specs = [pl.BlockSpec((lat_rows, D_MODEL), lat_idx)]
    in_specs += [pl.BlockSpec((g_blk, D_MODEL), lambda i, k=k: (L1_LAT // g_blk + g_n * i + k, 0))
                 for k in range(g_n)]
    in_specs += [row_blk(wq), row_blk(wkv), row_blk(w1o)]
    args = [w1_t] * (1 + g_n) + [wq, wkv, w1o]
    out_dims = [(L1_LAT_PAD, D_MODEL), (L1_WIDTH, D_MODEL), (Q_LORA, MLA_HEADS * MLA_QK),
                wkv.shape, w1o.shape]
    out_specs = [pl.BlockSpec((lat_rows, D_MODEL), lat_idx)]
    out_specs += [pl.BlockSpec((r // steps, c), lambda i: (i, 0)) for r, c in out_dims[1:]]

    def body(ins, outs):
        lat_in, g_in = ins[0], ins[1:1 + g_n]
        wq_in, wkv_in, w1o_in = ins[1 + g_n:]
        lat_o, g_o, wq_o, wkv_o, w1o_o = outs

        @pl.when(pl.program_id(0) < lat_n)
        def _():
            lat_o[...] = lat_in[...].astype(BF16)

        for k, g_ref in enumerate(g_in):
            g_o[k * g_blk:(k + 1) * g_blk, :] = g_ref[...].astype(BF16)
        qw = QK_NOPE + QK_ROPE
        pad = jnp.zeros((wq_in.shape[0], MLA_QK - qw), F32)
        parts = []
        for hd in range(MLA_HEADS):
            parts += [wq_in[:, hd * qw:(hd + 1) * qw], pad]
        wq_o[...] = jnp.concatenate(parts, axis=1).astype(BF16)
        wkv_o[...] = wkv_in[...].astype(BF16)
        w1o_o[...] = w1o_in[...].astype(BF16)

    return _SideWork(steps, args, in_specs, out_specs,
                     [jax.ShapeDtypeStruct(d, BF16) for d in out_dims], body)


def _adaln_kernel(c_ref, w_ref, b_ref, o_ref):
    c = c_ref[...]
    o_ref[...] = _dot(_silu(c).astype(BF16), w_ref[...].astype(BF16)) + b_ref[...]


def _adaln(cs, w, b):
    tn = 1024
    n = w.shape[1]
    return pl.pallas_call(
        _adaln_kernel,
        grid=(n // tn,),
        in_specs=[pl.BlockSpec((ADA_ROWS, D_MODEL), lambda j: (0, 0)),
                  pl.BlockSpec((D_MODEL, tn), lambda j: (0, j)),
                  pl.BlockSpec((1, tn), lambda j: (0, j))],
        out_specs=pl.BlockSpec((ADA_ROWS, tn), lambda j: (0, j)),
        out_shape=jax.ShapeDtypeStruct((ADA_ROWS, n), F32),
        compiler_params=_params(),
        name="adaln",
    )(cs, w, b.reshape(1, n))


def _mod_specs(latent, cols):
    rb = 0 if latent else 1
    return [pl.BlockSpec((8, D_MODEL), lambda i, c=c: (rb, c)) for c in cols]


def _store_heads(o_ref, y):
    o_ref[...] = y.reshape(o_ref.shape)


def _in0_kernel(*refs, latent, tm, seq, side):
    n_in = 10 if latent else 7
    n_si = side.n_in if side else 0
    n_out = 6 if latent else 10
    outs = refs[n_in + n_si:n_in + n_si + n_out]
    if side:
        side.body(refs[n_in:n_in + n_si], refs[n_in + n_si + n_out:])
    if latent:
        (x_ref, shift_ref, scale_ref, g_ref, w_ref, qn_ref, kn_ref,
         cos_ref, slo_ref, shi_ref) = refs[:n_in]
        qa_ref, ka_ref, va_ref, qb_ref, kb_ref, vb_ref = outs
    else:
        x_ref, shift_ref, scale_ref, g_ref, w_ref, qn_ref, kn_ref = refs[:n_in]
        (qa_ref, ka_ref, va_ref, qb_ref, kb_ref, vb_ref,
         kaf_ref, vaf_ref, kbf_ref, vbf_ref) = outs
    row = (pl.program_id(0) * tm) // seq if latent else 0
    h = _mod_norm(x_ref, g_ref, shift_ref, scale_ref, row)

    def head_norm(y, g_ref_):
        y = _rms(y, g_ref_[...])
        if latent:
            y = _rope(y, cos_ref[...], slo_ref[...], shi_ref[...], HEAD_DIM // 4)
        return y

    c_ka, c_va, c_qb = A_WIDTH, A_WIDTH + A_KV_WIDTH, A_WIDTH + 2 * A_KV_WIDTH
    c_kb, c_vb = c_qb + B_WIDTH, c_qb + 2 * B_WIDTH
    y = _dot(h, w_ref[:, c_kb:c_kb + B_WIDTH])
    kb_ref[...] = y.astype(BF16)
    if not latent:
        _store_heads(kbf_ref, y)
    y = _dot(h, w_ref[:, c_vb:c_vb + B_WIDTH])
    vb_ref[...] = y.astype(BF16)
    if not latent:
        _store_heads(vbf_ref, y)
    y = _dot(h, w_ref[:, c_ka:c_ka + A_KV_WIDTH])
    for hd in range(A_KV_HEADS):
        sl = slice(hd * HEAD_DIM, (hd + 1) * HEAD_DIM)
        k = head_norm(y[:, sl], kn_ref)
        ka_ref[:, sl] = k.astype(BF16)
        if not latent:
            kaf_ref[:, hd, :] = k
    y = _dot(h, w_ref[:, c_va:c_va + A_KV_WIDTH])
    va_ref[...] = y.astype(BF16)
    if not latent:
        _store_heads(vaf_ref, y)
    y = _dot(h, w_ref[:, :A_WIDTH])
    for hd in range(A_Q_HEADS):
        sl = slice(hd * HEAD_DIM, (hd + 1) * HEAD_DIM)
        qa_ref[:, sl] = (head_norm(y[:, sl], qn_ref) * Q_SCALE_L0).astype(BF16)
    y = _dot(h, w_ref[:, c_qb:c_qb + B_WIDTH])
    qb_ref[...] = (y * Q_SCALE_L0).astype(BF16)


def _in0(x, mod, g, w, qn, kn, rope_tabs, *, latent, seq, tm=512, side=None):
    m = x.shape[0]
    row_spec = lambda n: pl.BlockSpec((tm, n), lambda i: (i, 0))
    in_specs = [row_spec(D_MODEL)] + _mod_specs(latent, (0, 1)) + [
        _resident((1, D_MODEL)), _resident((D_MODEL, L0_QKV)),
        _resident((1, HEAD_DIM)), _resident((1, HEAD_DIM))]
    args = [x, mod, mod, g, w, qn, kn]
    if latent:
        nblk = seq // tm
        in_specs += [pl.BlockSpec((tm, LANES), lambda i: (i % nblk, 0))] * 3
        args += list(rope_tabs)
    widths = [A_WIDTH, A_KV_WIDTH, A_KV_WIDTH, B_WIDTH, B_WIDTH, B_WIDTH]
    out_specs = [row_spec(n) for n in widths]
    out_shape = [jax.ShapeDtypeStruct((m, n), BF16) for n in widths]
    if not latent:
        for nh in (A_KV_HEADS, A_KV_HEADS, B_HEADS, B_HEADS):
            out_specs.append(pl.BlockSpec((tm, nh, HEAD_DIM), lambda i: (i, 0, 0)))
            out_shape.append(jax.ShapeDtypeStruct((m, nh, HEAD_DIM), F32))
    if side:
        assert side.steps == m // tm
        in_specs += side.in_specs
        args += side.args
        out_specs += side.out_specs
        out_shape += side.out_shape
    return pl.pallas_call(
        functools.partial(_in0_kernel, latent=latent, tm=tm, seq=seq, side=side),
        grid=(m // tm,), in_specs=in_specs, out_specs=out_specs, out_shape=out_shape,
        compiler_params=_params(), name="in0_lat" if latent else "in0_ctx",
    )(*args)


def _attn_units(q_ref, kv_refs, o_ref, *, heads, group, sub, nbs=1):
    seq = q_ref.shape[0] // nbs
    for b in range(nbs):
        for j in range(heads // group):
            sl_kv = slice(j * HEAD_DIM, (j + 1) * HEAD_DIM)
            rows = lambda r: slice(b * (r.shape[0] // nbs), (b + 1) * (r.shape[0] // nbs))
            ks = [k_ref[rows(k_ref), sl_kv].astype(BF16) for k_ref, _ in kv_refs]
            vs = [_with_ones(v_ref[rows(v_ref), sl_kv]) for _, v_ref in kv_refs]
            sls = [slice((j * group + gi) * HEAD_DIM, (j * group + gi + 1) * HEAD_DIM)
                   for gi in range(group)]
            for r0 in range(b * seq, (b + 1) * seq, sub):
                q = jnp.concatenate([q_ref[r0:r0 + sub, sl] for sl in sls], axis=0)
                yield o_ref, r0, sub, sls, [_dot_nt(q, k) for k in ks], vs


def _attn_finish(unit):
    o_ref, r0, sub, sls, ss, vs = unit
    o = _softmax_pv(ss, vs).astype(BF16)
    for gi, sl in enumerate(sls):
        o_ref[r0:r0 + sub, sl] = o[gi * sub:(gi + 1) * sub]


def _attn_kernel(*refs, n_src, heads, group, sub, lag):
    q_ref = refs[0]
    kv_refs = [(refs[1 + 2 * s], refs[2 + 2 * s]) for s in range(n_src)]
    o_ref = refs[1 + 2 * n_src]
    _run_lagged(_attn_units(q_ref, kv_refs, o_ref, heads=heads, group=group, sub=sub),
                lag, _attn_finish)


def _attn(q, srcs, *, batch, seq, heads, group, name, lag=1):
    kvw = (heads // group) * HEAD_DIM
    in_specs = [pl.BlockSpec((seq, heads * HEAD_DIM), lambda b: (b, 0))]
    args = [q]
    for k, v, t in srcs:
        in_specs += [pl.BlockSpec((t, kvw), lambda b: (b, 0))] * 2
        args += [k, v]
    return pl.pallas_call(
        functools.partial(_attn_kernel, n_src=len(srcs), heads=heads, group=group,
                          sub=min(seq, 256), lag=lag),
        grid=(batch,), in_specs=in_specs,
        out_specs=pl.BlockSpec((seq, heads * HEAD_DIM), lambda b: (b, 0)),
        out_shape=jax.ShapeDtypeStruct((batch * seq, heads * HEAD_DIM), BF16),
        compiler_params=_params(), name=name,
    )(*args)


def _cache_side(caches, steps):
    dec_batch, past = caches[0].shape[:2]
    spb = steps // dec_batch
    rows = past // spb
    in_specs = [pl.BlockSpec((None, rows) + c.shape[2:],
                             lambda i, *_: (i // spb, i % spb, 0, 0)) for c in caches]
    widths = [c.shape[2] * HEAD_DIM for c in caches]
    out_specs = [pl.BlockSpec((rows, w), lambda i, *_: (i, 0)) for w in widths]
    out_shape = [jax.ShapeDtypeStruct((dec_batch * past, w), BF16) for w in widths]

    def body(ins, outs):
        for c_ref, o_ref in zip(ins, outs):
            o_ref[...] = c_ref[...].reshape(o_ref.shape).astype(BF16)

    return _SideWork(steps, list(caches), in_specs, out_specs, out_shape, body)


def _ctx_attn_kernel(qa_ref, ka_ref, va_ref, qb_ref, kb_ref, vb_ref, oa_ref, ob_ref, *, nbs, lag):
    sub = min(qa_ref.shape[0] // nbs, 256)
    units = itertools.chain(
        _attn_units(qa_ref, [(ka_ref, va_ref)], oa_ref, heads=A_Q_HEADS,
                    group=A_Q_HEADS // A_KV_HEADS, sub=sub, nbs=nbs),
        _attn_units(qb_ref, [(kb_ref, vb_ref)], ob_ref, heads=B_HEADS, group=1, sub=sub, nbs=nbs))
    _run_lagged(units, lag, _attn_finish)


def _ctx_attn(qa, ka, va, qb, kb, vb, *, batch, seq, nbs=4, lag=4):
    tok = lambda n: pl.BlockSpec((nbs * seq, n), lambda i: (i, 0))
    return pl.pallas_call(
        functools.partial(_ctx_attn_kernel, nbs=nbs, lag=lag),
        grid=(batch // nbs,),
        in_specs=[tok(A_WIDTH), tok(A_KV_WIDTH), tok(A_KV_WIDTH), tok(B_WIDTH), tok(B_WIDTH),
                  tok(B_WIDTH)],
        out_specs=[tok(A_WIDTH), tok(B_WIDTH)],
        out_shape=[jax.ShapeDtypeStruct((batch * seq, A_WIDTH), BF16),
                   jax.ShapeDtypeStruct((batch * seq, B_WIDTH), BF16)],
        compiler_params=_params(), name="ctx_attn",
    )(qa, ka, va, qb, kb, vb)


def _na_geometry(rows):
    wh = min(NA_WIN_H, rows)
    row_start = np.clip(np.arange(rows) - wh // 2, 0, rows - wh)
    return wh, row_start


def _na_bias_fill(rpb_ref, hd, o_ref, q_rows, rows):
    wh, row_start = _na_geometry(rows)
    nx = 2 * NA_WIN_W - 1
    qc = lax.broadcasted_iota(jnp.int32, (GRID_W, GRID_W), 0)
    kc = lax.broadcasted_iota(jnp.int32, (GRID_W, GRID_W), 1)
    col_start = jnp.clip(qc - NA_WIN_W // 2, 0, GRID_W - NA_WIN_W)
    col_ok = (kc >= col_start) & (kc < col_start + NA_WIN_W)
    dx = kc - qc + NA_WIN_W - 1
    base = hd * ((2 * NA_WIN_H - 1) * nx)
    tiles = {}
    for dy in range(-(wh - 1), wh):
        t = jnp.zeros((GRID_W, GRID_W), F32)
        for j in range(nx):
            t = jnp.where(dx == j, rpb_ref[base + (dy + NA_WIN_H - 1) * nx + j], t)
        tiles[dy] = jnp.where(col_ok, t * LOG2E, NEG_INF)
    o_ref[...] = jnp.full(o_ref.shape, NEG_INF, F32)
    for li, qr in enumerate(q_rows):
        for kr in range(int(row_start[qr]), int(row_start[qr]) + wh):
            o_ref[0, li * GRID_W:(li + 1) * GRID_W, kr * GRID_W:(kr + 1) * GRID_W] = tiles[kr - qr]


def _prep0_kernel(w_ref, rpb_ref, o_ref, bias_ref, *, rows, sph):
    o_ref[...] = w_ref[...].astype(BF16)
    i = pl.program_id(0)
    rps = rows // sph
    for part in range(sph):
        @pl.when(i % sph == part)
        def _():
            _na_bias_fill(rpb_ref, i // sph, bias_ref, range(part * rps, (part + 1) * rps), rows)


def _prep0(w0, rpb, seq):
    nb = WEIGHT_ROW_BLOCKS
    sph = nb // B_HEADS
    rows = seq // GRID_W
    wr = w0.shape[0] // nb
    return pl.pallas_call(
        functools.partial(_prep0_kernel, rows=rows, sph=sph),
        grid=(nb,),
        in_specs=[pl.BlockSpec((wr, L0_QKV), lambda i: (i, 0)),
                  pl.BlockSpec(memory_space=pltpu.SMEM)],
        out_specs=[pl.BlockSpec((wr, L0_QKV), lambda i: (i, 0)),
                   pl.BlockSpec((1, seq // sph, seq), lambda i: (i // sph, i % sph, 0))],
        out_shape=[jax.ShapeDtypeStruct((w0.shape[0], L0_QKV), BF16),
                   jax.ShapeDtypeStruct((B_HEADS, seq, seq), F32)],
        compiler_params=_params(), name="prep0",
    )(w0, rpb.reshape(-1))


def _l0_side(w0, w0o, cs, ada_w, ada_b, steps):
    cw = 512
    assert L0_QKV % cw == 0 and L0_WIDTH % cw == 0
    n_c = L0_WIDTH // cw
    r_blk = D_MODEL // steps
    n_ada = ada_w.shape[1]
    tn = n_ada // steps
    o_blk = pl.BlockSpec((w0o.shape[0] // steps, w0o.shape[1]), lambda i: (i, 0))
    in_specs = [pl.BlockSpec((r_blk, cw), lambda i, k=k: (i, L0_QKV // cw + k)) for k in range(n_c)]
    in_specs += [o_blk, _resident((ADA_ROWS, D_MODEL)),
                 pl.BlockSpec((D_MODEL, tn), lambda i: (0, i)),
                 pl.BlockSpec((1, tn), lambda i: (0, i))]
    out_specs = [pl.BlockSpec((r_blk, L0_WIDTH), lambda i: (i, 0)), o_blk,
                 pl.BlockSpec((ADA_ROWS, tn), lambda i: (0, i))]
    out_shape = [jax.ShapeDtypeStruct((D_MODEL, L0_WIDTH), BF16),
                 jax.ShapeDtypeStruct(w0o.shape, BF16),
                 jax.ShapeDtypeStruct((ADA_ROWS, n_ada), F32)]

    def body(ins, outs):
        outs[0][...] = jnp.concatenate([r[...] for r in ins[:n_c]], axis=1).astype(BF16)
        outs[1][...] = ins[n_c][...].astype(BF16)
        c_ref, aw_ref, ab_ref = ins[n_c + 1:]
        outs[2][...] = (_dot(_silu(c_ref[...]).astype(BF16), aw_ref[...].astype(BF16))
                        + ab_ref[...])

    args = [w0] * n_c + [w0o, cs, ada_w, ada_b.reshape(1, n_ada)]
    return _SideWork(steps, args, in_specs, out_specs, out_shape, body)


def _na_windows(rows, tq):
    wh, row_start = _na_geometry(rows)
    rq = tq // GRID_W
    wins = []
    for qb in range(rows // rq):
        lo = int(row_start[qb * rq]) * GRID_W
        hi = (int(row_start[qb * rq + rq - 1]) + wh) * GRID_W
        wins.append((lo // LANES * LANES, -(-hi // LANES) * LANES))
    return wins


def _na_kernel(q_ref, k_ref, v_ref, kc_ref, vc_ref, bias_ref, o_ref, *, seq, tq, wins, lag):
    nb = q_ref.shape[0] // seq
    past = kc_ref.shape[0] // nb

    def scores():
        for b in range(nb):
            kc = kc_ref[b * past:(b + 1) * past, :]
            vc = _with_ones(vc_ref[b * past:(b + 1) * past, :])
            v = _with_ones(v_ref[b * seq:(b + 1) * seq, :])
            for qb, (lo, hi) in enumerate(wins):
                rs = slice(qb * tq, (qb + 1) * tq)
                rows = slice(b * seq + qb * tq, b * seq + (qb + 1) * tq)
                q = q_ref[rows, :]
                s_loc = _dot_nt(q, k_ref[b * seq + lo:b * seq + hi, :]) + bias_ref[0, rs, lo:hi]
                yield rows, [s_loc, _dot_nt(q, kc)], [v[lo:hi], vc]

    def finish(unit):
        rows, ss, vs = unit
        o_ref[rows, :] = _softmax_pv(ss, vs).astype(BF16)

    _run_lagged(scores(), lag, finish)


def _na(q, k, v, kc, vc, bias, *, batch, seq, tq=256, nb=8, lag=1):
    past = kc.shape[0] // batch
    wins = _na_windows(seq // GRID_W, tq)
    tok = pl.BlockSpec((nb * seq, HEAD_DIM), lambda h, b: (b, h))
    cache = pl.BlockSpec((nb * past, HEAD_DIM), lambda h, b: (b, h))
    return pl.pallas_call(
        functools.partial(_na_kernel, seq=seq, tq=tq, wins=wins, lag=lag),
        grid=(B_HEADS, batch // nb),
        in_specs=[tok, tok, tok, cache, cache,
                  pl.BlockSpec((1, seq, seq), lambda h, b: (h, 0, 0))],
        out_specs=tok,
        out_shape=jax.ShapeDtypeStruct((batch * seq, B_WIDTH), BF16),
        compiler_params=_params(), name="na",
    )(q, k, v, kc, vc, bias)


def _out_kernel(*refs, latent, tm, seq, n_o, final, gate_t, side):
    x_ref, shift_ref, scale_ref, gate_ref, g_ref = refs[:5]
    o_refs = refs[5:5 + n_o]
    wg_ref, wo_ref = refs[5 + n_o:7 + n_o]
    n_fixed = 7 + n_o + (1 if final else 0)
    fn_ref = refs[n_fixed - 1] if final else None
    n_si = side.n_in if side else 0
    y_ref = refs[n_fixed + n_si]
    if side:
        side.body(refs[n_fixed:n_fixed + n_si], refs[n_fixed + n_si + 1:])
    row = (pl.program_id(0) * tm) // seq if latent else 0
    h = _mod_norm(x_ref, g_ref, shift_ref, scale_ref, row)
    gt = _dot_nt(h, wg_ref[...]) if gate_t else _dot(h, wg_ref[...])
    o = jnp.concatenate([r[...] for r in o_refs], axis=-1).astype(F32)
    t = (o * _silu(gt)).astype(BF16)
    y = x_ref[...] + gate_ref[pl.ds(row, 1), :] * _dot(t, wo_ref[...])
    if final:
        y = _rms(y, fn_ref[...])
    y_ref[...] = y


def _out(x, mod, g, os_, wg, wo, fn, *, latent, seq, name, tm=512, gate_t=False, side=None):
    m = x.shape[0]
    width = wo.shape[0]
    final = fn is not None
    in_specs = [pl.BlockSpec((tm, D_MODEL), lambda i: (i, 0))] + _mod_specs(latent, (0, 1, 2))
    in_specs += [_resident((1, D_MODEL))]
    in_specs += [pl.BlockSpec((tm, o.shape[1]), lambda i: (i, 0)) for o in os_]
    in_specs += [_resident(wg.shape), _resident((width, D_MODEL))]
    args = [x, mod, mod, mod, g, *os_, wg, wo]
    if final:
        in_specs.append(_resident((1, D_MODEL)))
        args.append(fn)
    out_specs = [pl.BlockSpec((tm, D_MODEL), lambda i: (i, 0))]
    out_shape = [jax.ShapeDtypeStruct((m, D_MODEL), F32)]
    if side:
        assert side.steps == m // tm
        in_specs += side.in_specs
        args += side.args
        out_specs += side.out_specs
        out_shape += side.out_shape
    res = pl.pallas_call(
        functools.partial(_out_kernel, latent=latent, tm=tm, seq=seq, n_o=len(os_), final=final,
                          gate_t=gate_t, side=side),
        grid=(m // tm,), in_specs=in_specs, out_specs=out_specs, out_shape=out_shape,
        compiler_params=_params(), name=name,
    )(*args)
    return res if side else res[0]


def _in1_body(x_ref, shift_ref, scale_ref, g_ref, w_ref, qan_ref, kvn_ref, wq_ref, rope_refs,
              q_ref, ckv_ref, kr_ref, row):
    h = _mod_norm(x_ref, g_ref, shift_ref, scale_ref, row)
    z = _dot_nt(h, w_ref[...])
    kr = z[:, Q_LORA + KV_LORA:]
    if rope_refs:
        cos, slo, shi = (r[...] for r in rope_refs)
        kr = _rope(kr, cos, slo, shi, QK_ROPE // 4)
    ckv_ref[...] = _rms(z[:, Q_LORA:Q_LORA + KV_LORA], kvn_ref[...]).astype(ckv_ref.dtype)
    kr_ref[...] = kr[:, :QK_ROPE].astype(kr_ref.dtype)
    q_lat = _rms(z[:, :Q_LORA], qan_ref[...]).astype(BF16)
    q = _dot(q_lat, wq_ref[...])
    for j in range(MLA_HEADS * MLA_QK // LANES):
        sl = slice(j * LANES, (j + 1) * LANES)
        qj = q[:, sl]
        if rope_refs and j % 2 == 1:
            qj = _rope(qj, cos, slo, shi, QK_ROPE // 4)
        q_ref[:, sl] = (qj * Q_SCALE_L1).astype(BF16)


def _in1_kernel(*refs, latent, tm, seq):
    n_in = 11 if latent else 8
    rope_refs = refs[8:11] if latent else None
    row = (pl.program_id(0) * tm) // seq if latent else 0
    _in1_body(*refs[:8], rope_refs, *refs[n_in:], row)


def _in1(x, mod, g, w, qan, kvn, wq, rope_tabs, *, latent, seq, tm=512):
    m = x.shape[0]
    row_spec = lambda n: pl.BlockSpec((tm, n), lambda i: (i, 0))
    in_specs = [row_spec(D_MODEL)] + _mod_specs(latent, (0, 1)) + [
        _resident((1, D_MODEL)), _resident(w.shape),
        _resident((1, Q_LORA)), _resident((1, KV_LORA)),
        _resident((Q_LORA, MLA_HEADS * MLA_QK))]
    args = [x, mod, mod, g, w, qan, kvn, wq]
    if latent:
        nblk = seq // tm
        in_specs += [pl.BlockSpec((tm, LANES), lambda i: (i % nblk, 0))] * 3
        args += list(rope_tabs)
    lat_dt = BF16 if latent else F32
    widths = [(MLA_HEADS * MLA_QK, BF16), (KV_LORA, lat_dt), (QK_ROPE, lat_dt)]
    return pl.pallas_call(
        functools.partial(_in1_kernel, latent=latent, tm=tm, seq=seq),
        grid=(m // tm,), in_specs=in_specs,
        out_specs=[row_spec(n) for n, _ in widths],
        out_shape=[jax.ShapeDtypeStruct((m, n), dt) for n, dt in widths],
        compiler_params=_params(), name="in1_lat" if latent else "in1_ctx",
    )(*args)


def _mla_body(q_ref, src, w_ref, o_ref, *, hb, nbs, seq, lag):
    n_src = len(src) // 2
    ckv = [src[2 * s][...].astype(BF16) for s in range(n_src)]
    krp = []
    for s in range(n_src):
        kr = src[2 * s + 1][...].astype(F32)
        pad = jnp.zeros((kr.shape[0], LANES - QK_ROPE), F32)
        krp.append(jnp.concatenate([kr, pad], axis=1).astype(BF16))
    kvw = QK_NOPE + V_DIM

    def scores():
        for hd in range(hb):
            kv = [_dot(c, w_ref[:, hd * kvw:(hd + 1) * kvw]).astype(BF16) for c in ckv]
            for b in range(nbs):
                rq = slice(b * seq, (b + 1) * seq)
                part = lambda a: a[b * (a.shape[0] // nbs):(b + 1) * (a.shape[0] // nbs)]
                kvb = [part(kv_s) for kv_s in kv]
                kcat = [jnp.concatenate([kv_s[:, :QK_NOPE], part(kr_s)], axis=1)
                        for kv_s, kr_s in zip(kvb, krp)]
                q = q_ref[rq, hd * MLA_QK:(hd + 1) * MLA_QK]
                yield (rq, hd, [_dot_nt(q, kc) for kc in kcat],
                       [_with_ones(kv_s[:, QK_NOPE:]) for kv_s in kvb])

    def finish(unit):
        rq, hd, ss, vs = unit
        o_ref[rq, hd * V_DIM:(hd + 1) * V_DIM] = _softmax_pv(ss, vs).astype(BF16)

    _run_lagged(scores(), lag, finish)


def _mla_kernel(*refs, n_src, hb, nbs, seq, lag):
    _mla_body(refs[0], refs[1:1 + 2 * n_src], refs[1 + 2 * n_src], refs[2 + 2 * n_src],
              hb=hb, nbs=nbs, seq=seq, lag=lag)


def _mla(q, srcs, w_kv, *, batch, seq, hb, name, lag=1):
    kvw = QK_NOPE + V_DIM
    in_specs = [pl.BlockSpec((seq, hb * MLA_QK), lambda b, j: (b, j))]
    args = [q]
    for ckv, kr, t in srcs:
        for a in (ckv, kr):
            wd = a.shape[-1]
            if a.ndim == 3:
                in_specs.append(pl.BlockSpec((None, t, wd), lambda b, j: (b, 0, 0)))
            else:
                in_specs.append(pl.BlockSpec((t, wd), lambda b, j: (b, 0)))
            args.append(a)
    in_specs.append(pl.BlockSpec((KV_LORA, hb * kvw), lambda b, j: (0, j)))
    args.append(w_kv)
    return pl.pallas_call(
        functools.partial(_mla_kernel, n_src=len(srcs), hb=hb, nbs=1, seq=seq, lag=lag),
        grid=(batch, MLA_HEADS // hb), in_specs=in_specs,
        out_specs=pl.BlockSpec((seq, hb * V_DIM), lambda b, j: (b, j)),
        out_shape=jax.ShapeDtypeStruct((batch * seq, L1_WIDTH), BF16),
        compiler_params=_params(), name=name,
    )(*args)


def _l1_ctx_kernel(*refs, nbs, seq, lag, side):
    fixed = refs[:9]
    n_si = side.n_in
    o_ref, ckv_ref, kr_ref = refs[9 + n_si:12 + n_si]
    q_scr = refs[-1]
    side.body(refs[9:9 + n_si], refs[12 + n_si:-1])
    _in1_body(*fixed[:8], None, q_scr, ckv_ref, kr_ref, 0)
    _mla_body(q_scr, [ckv_ref, kr_ref], fixed[8], o_ref, hb=MLA_HEADS, nbs=nbs, seq=seq, lag=lag)


def _l1_ctx(x, mod, g, w, qan, kvn, wq, w_kv, side, *, seq, nbs=2, lag=4):
    m = x.shape[0]
    tm = nbs * seq
    row_spec = lambda n: pl.BlockSpec((tm, n), lambda i: (i, 0))
    in_specs = [row_spec(D_MODEL)] + _mod_specs(False, (0, 1)) + [
        _resident((1, D_MODEL)), _resident(w.shape),
        _resident((1, Q_LORA)), _resident((1, KV_LORA)),
        _resident(wq.shape), _resident(w_kv.shape)]
    assert side.steps == m // tm
    widths = [(L1_WIDTH, BF16), (KV_LORA, F32), (QK_ROPE, F32)]
    return pl.pallas_call(
        functools.partial(_l1_ctx_kernel, nbs=nbs, seq=seq, lag=lag, side=side),
        grid=(m // tm,), in_specs=in_specs + side.in_specs,
        out_specs=[row_spec(n) for n, _ in widths] + side.out_specs,
        out_shape=[jax.ShapeDtypeStruct((m, n), dt) for n, dt in widths] + side.out_shape,
        scratch_shapes=[pltpu.VMEM((tm, MLA_HEADS * MLA_QK), BF16)],
        compiler_params=_params(), name="l1_ctx",
    )(x, mod, mod, g, w, qan, kvn, wq, w_kv, *side.args)


def _rope_tables(seq, d):
    a = d // 2
    t = np.arange(seq)
    row = (t // GRID_W).astype(np.float64)
    col = (t % GRID_W).astype(np.float64)
    inv = 1.0 / (ROPE_THETA ** (np.arange(a // 2, dtype=np.float64) * 2.0 / a))
    ang = np.concatenate([row[:, None] * inv] * 2 + [col[:, None] * inv] * 2, axis=-1)
    ang = np.tile(ang, (1, LANES // d))
    lower = np.tile((np.arange(d) % a) < a // 2, LANES // d)[None, :]
    cos, sin = np.cos(ang), np.sin(ang)
    tabs = (cos, np.where(lower, -sin, 0.0), np.where(lower, 0.0, sin))
    return tuple(jnp.asarray(x, F32) for x in tabs)


def kernel(x_prompt, x_sample, cache_l0_a_k, cache_l0_a_v, cache_l0_b_k, cache_l0_b_v, cache_l1_ckv, cache_l1_krope, c, c_ctx, l0_norm, l0_ada_w, l0_ada_b, l0_w_in, l0_q_norm, l0_k_norm, l0_rpb, l0_w_out, l1_norm, l1_ada_w, l1_ada_b, l1_w_in, l1_q_a_norm, l1_w_q_b, l1_kv_a_norm, l1_w_kv_b, l1_w_out, final_norm):
    batch, seq_c, _ = x_prompt.shape
    dec_batch, seq_l, _ = x_sample.shape
    past = cache_l0_a_k.shape[1]

    w0_qkv, bias = _prep0(l0_w_in, l0_rpb, seq_l)

    row2 = lambda v: v.reshape(1, -1)
    cs = jnp.zeros((ADA_ROWS, D_MODEL), F32).at[:dec_batch].set(c).at[8].set(c_ctx)
    mod0 = _adaln(cs, l0_ada_w, l0_ada_b)

    rope_a = _rope_tables(seq_l, HEAD_DIM)
    rope_c = _rope_tables(seq_l, QK_ROPE)

    tm = 512
    xc = x_prompt.reshape(batch * seq_c, D_MODEL)
    xl = x_sample.reshape(dec_batch * seq_l, D_MODEL)
    qa, ka, va, qb, kb, vb, ka_f, va_f, kb_f, vb_f = _in0(
        xc, mod0, row2(l0_norm), w0_qkv, row2(l0_q_norm), row2(l0_k_norm), None,
        latent=False, seq=seq_c, tm=tm)
    side = _l0_side(l0_w_in, l0_w_out, cs, l1_ada_w, l1_ada_b, dec_batch * seq_l // tm)
    qa_l, ka_l, va_l, qb_l, kb_l, vb_l, w0_gate, w0_out, mod1 = _in0(
        xl, mod0, row2(l0_norm), w0_qkv, row2(l0_q_norm), row2(l0_k_norm), rope_a,
        latent=True, seq=seq_l, tm=tm, side=side)

    oa, ob = _ctx_attn(qa, ka, va, qb, kb, vb, batch=batch, seq=seq_c)
    side = _l1_weight_side(l1_w_in.T, l1_w_q_b, l1_w_kv_b, l1_w_out, batch * seq_c // tm)
    xc, w1_all, w1_gate, wq, w_kv, w1_out = _out(
        xc, mod0, row2(l0_norm), [oa, ob], w0_gate, w0_out, None,
        latent=False, seq=seq_c, name="out0_ctx", tm=tm, side=side)
    l1_nbs = tm // seq_c
    side = _cache_side([cache_l0_a_k, cache_l0_a_v, cache_l0_b_k, cache_l0_b_v], batch // l1_nbs)
    oc, ckv_f, kr_f, ck_a, cv_a, ck_b, cv_b = _l1_ctx(
        xc, mod1, row2(l1_norm), w1_all, row2(l1_q_a_norm), row2(l1_kv_a_norm), wq, w_kv, side,
        seq=seq_c, nbs=l1_nbs)
    y_prompt = _out(xc, mod1, row2(l1_norm), [oc], w1_gate, w1_out, row2(final_norm),
                    latent=False, seq=seq_c, name="out1_ctx", gate_t=True)

    oa = _attn(qa_l, [(ck_a, cv_a, past), (ka_l, va_l, seq_l)],
               batch=dec_batch, seq=seq_l, heads=A_Q_HEADS, group=A_Q_HEADS // A_KV_HEADS,
               name="attn_a_lat")
    ob = _na(qb_l, kb_l, vb_l, ck_b, cv_b, bias, batch=dec_batch, seq=seq_l)
    xl = _out(xl, mod0, row2(l0_norm), [oa, ob], w0_gate, w0_out, None,
              latent=True, seq=seq_l, name="out0_lat")
    q1, ckv, kr = _in1(xl, mod1, row2(l1_norm), w1_all, row2(l1_q_a_norm),
                       row2(l1_kv_a_norm), wq, rope_c, latent=True, seq=seq_l, tm=1024)
    oc = _mla(q1, [(cache_l1_ckv, cache_l1_krope, past), (ckv, kr, seq_l)], w_kv,
              batch=dec_batch, seq=seq_l, hb=MLA_HEADS, lag=1, name="mla_lat")
    y_sample = _out(xl, mod1, row2(l1_norm), [oc], w1_gate, w1_out, row2(final_norm),
                    latent=True, seq=seq_l, name="out1_lat", gate_t=True)

    return (y_prompt.reshape(batch, seq_c, D_MODEL),
            y_sample.reshape(dec_batch, seq_l, D_MODEL),
            ka_f.reshape(batch, seq_c, A_KV_HEADS, HEAD_DIM),
            va_f.reshape(batch, seq_c, A_KV_HEADS, HEAD_DIM),
            kb_f.reshape(batch, seq_c, B_HEADS, HEAD_DIM),
            vb_f.reshape(batch, seq_c, B_HEADS, HEAD_DIM),
            ckv_f.reshape(batch, seq_c, KV_LORA),
            kr_f.reshape(batch, seq_c, QK_ROPE))
```

```python
import functools
import itertools
import math
from typing import Callable, NamedTuple

import jax
import jax.numpy as jnp
import numpy as np
from jax import lax
from jax.experimental import pallas as pl
from jax.experimental.pallas import tpu as pltpu

LANES = 128
VMEM_LIMIT = 56 * 1024 * 1024
ADA_ROWS = 16
F32 = jnp.float32
BF16 = jnp.bfloat16

D_MODEL = 2048
GRID_W = 64
HEAD_DIM = 128
A_Q_HEADS = 8
A_KV_HEADS = 2
B_HEADS = 8
A_WIDTH = A_Q_HEADS * HEAD_DIM
A_KV_WIDTH = A_KV_HEADS * HEAD_DIM
B_WIDTH = B_HEADS * HEAD_DIM
L0_WIDTH = A_WIDTH + B_WIDTH
L0_QKV = A_WIDTH + 2 * A_KV_WIDTH + 3 * B_WIDTH
NA_WIN_H = 8
NA_WIN_W = 16
MLA_HEADS = 16
Q_LORA = 512
KV_LORA = 512
QK_NOPE = 128
QK_ROPE = 64
V_DIM = 128
L1_WIDTH = MLA_HEADS * V_DIM
L1_LAT = Q_LORA + KV_LORA + QK_ROPE
L1_LAT_PAD = L1_LAT + (LANES - QK_ROPE)
MLA_QK = 2 * LANES
ROPE_THETA = 10000.0
EPS = 1e-6
NEG_INF = -1e30
LOG2E = 1.4426950408889634
Q_SCALE_L0 = HEAD_DIM ** -0.5 * LOG2E
Q_SCALE_L1 = (QK_NOPE + QK_ROPE) ** -0.5 * LOG2E


def _params(vmem=VMEM_LIMIT):
    return pltpu.CompilerParams(vmem_limit_bytes=vmem)


def _resident(shape):
    nd = len(shape)
    return pl.BlockSpec(shape, lambda *_: (0,) * nd, pipeline_mode=pl.Buffered(1))


def _dot(a, b):
    return jnp.dot(a, b, preferred_element_type=F32)


def _dot_nt(a, b):
    return lax.dot_general(a, b, (((1,), (1,)), ((), ())), preferred_element_type=F32)


def _rms(x, g):
    return x * lax.rsqrt(jnp.mean(x * x, axis=-1, keepdims=True) + EPS) * g


def _silu(x):
    return x * jax.nn.sigmoid(x)


def _rope(x, cos, sin_lo, sin_hi, half):
    return (x * cos + pltpu.roll(x, half, 1) * sin_hi
            + pltpu.roll(x, LANES - half, 1) * sin_lo)


def _mod_norm(x_ref, g_ref, shift_ref, scale_ref, row):
    x = x_ref[...]
    h = _rms(x, g_ref[...]) * (1.0 + scale_ref[pl.ds(row, 1), :]) + shift_ref[pl.ds(row, 1), :]
    return h.astype(BF16)


def _with_ones(v):
    v = v.astype(BF16)
    return jnp.concatenate([v, jnp.ones_like(v)], axis=1)


def _softmax_pv(ss, vs):
    mx = ss[0].max(axis=-1, keepdims=True)
    for s in ss[1:]:
        mx = jnp.maximum(mx, s.max(axis=-1, keepdims=True))
    acc = _dot(jnp.exp2(ss[0] - mx).astype(BF16), vs[0])
    for s, v in zip(ss[1:], vs[1:]):
        acc = acc + _dot(jnp.exp2(s - mx).astype(BF16), v)
    return acc[:, :LANES] / acc[:, LANES:]


def _run_lagged(units, lag, finish):
    pending = []
    for unit in units:
        pending.append(unit)
        while len(pending) > lag:
            finish(pending.pop(0))
    for unit in pending:
        finish(unit)


WEIGHT_ROW_BLOCKS = 8


class _SideWork(NamedTuple):
    steps: int
    args: list
    in_specs: list
    out_specs: list
    out_shape: list
    body: Callable

    @property
    def n_in(self):
        return len(self.args)


def _l1_weight_side(w1_t, wq, wkv, w1o, steps):
    g_rows = L1_WIDTH // steps
    g_blk = math.gcd(L1_LAT, g_rows)
    g_n = g_rows // g_blk
    lat_rows = 128
    lat_n = L1_LAT_PAD // lat_rows
    row_blk = lambda a: pl.BlockSpec((a.shape[0] // steps, a.shape[1]), lambda i: (i, 0))
    lat_idx = lambda i: (jnp.minimum(i, lat_n - 1), 0)
    in_specs = [pl.BlockSpec((lat_rows, D_MODEL), lat_idx)]
    in_specs += [pl.BlockSpec((g_blk, D_MODEL), lambda i, k=k: (L1_LAT // g_blk + g_n * i + k, 0))
                 for k in range(g_n)]
    in_specs += [row_blk(wq), row_blk(wkv), row_blk(w1o)]
    args = [w1_t] * (1 + g_n) + [wq, wkv, w1o]
    out_dims = [(L1_LAT_PAD, D_MODEL), (L1_WIDTH, D_MODEL), (Q_LORA, MLA_HEADS * MLA_QK),
                wkv.shape, w1o.shape]
    out_specs = [pl.BlockSpec((lat_rows, D_MODEL), lat_idx)]
    out_specs += [pl.BlockSpec((r // steps, c), lambda i: (i, 0)) for r, c in out_dims[1:]]

    def body(ins, outs):
        lat_in, g_in = ins[0], ins[1:1 + g_n]
        wq_in, wkv_in, w1o_in = ins[1 + g_n:]
        lat_o, g_o, wq_o, wkv_o, w1o_o = outs

        @pl.when(pl.program_id(0) < lat_n)
        def _():
            lat_o[...] = lat_in[...].astype(BF16)

        for k, g_ref in enumerate(g_in):
            g_o[k * g_blk:(k + 1) * g_blk, :] = g_ref[...].astype(BF16)
        qw = QK_NOPE + QK_ROPE
        pad = jnp.zeros((wq_in.shape[0], MLA_QK - qw), F32)
        parts = []
        for hd in range(MLA_HEADS):
            parts += [wq_in[:, hd * qw:(hd + 1) * qw], pad]
        wq_o[...] = jnp.concatenate(parts, axis=1).astype(BF16)
        wkv_o[...] = wkv_in[...].astype(BF16)
        w1o_o[...] = w1o_in[...].astype(BF16)

    return _SideWork(steps, args, in_specs, out_specs,
                     [jax.ShapeDtypeStruct(d, BF16) for d in out_dims], body)


def _adaln_kernel(c_ref, w_ref, b_ref, o_ref):
    c = c_ref[...]
    o_ref[...] = _dot(_silu(c).astype(BF16), w_ref[...].astype(BF16)) + b_ref[...]


def _adaln(cs, w, b):
    tn = 1024
    n = w.shape[1]
    return pl.pallas_call(
        _adaln_kernel,
        grid=(n // tn,),
        in_specs=[pl.BlockSpec((ADA_ROWS, D_MODEL), lambda j: (0, 0)),
                  pl.BlockSpec((D_MODEL, tn), lambda j: (0, j)),
                  pl.BlockSpec((1, tn), lambda j: (0, j))],
        out_specs=pl.BlockSpec((ADA_ROWS, tn), lambda j: (0, j)),
        out_shape=jax.ShapeDtypeStruct((ADA_ROWS, n), F32),
        compiler_params=_params(),
        name="adaln",
    )(cs, w, b.reshape(1, n))


def _mod_specs(latent, cols):
    rb = 0 if latent else 1
    return [pl.BlockSpec((8, D_MODEL), lambda i, c=c: (rb, c)) for c in cols]


def _store_heads(o_ref, y):
    o_ref[...] = y.reshape(o_ref.shape)


def _in0_kernel(*refs, latent, tm, seq, side):
    n_in = 10 if latent else 7
    n_si = side.n_in if side else 0
    n_out = 6 if latent else 10
    outs = refs[n_in + n_si:n_in + n_si + n_out]
    if side:
        side.body(refs[n_in:n_in + n_si], refs[n_in + n_si + n_out:])
    if latent:
        (x_ref, shift_ref, scale_ref, g_ref, w_ref, qn_ref, kn_ref,
         cos_ref, slo_ref, shi_ref) = refs[:n_in]
        qa_ref, ka_ref, va_ref, qb_ref, kb_ref, vb_ref = outs
    else:
        x_ref, shift_ref, scale_ref, g_ref, w_ref, qn_ref, kn_ref = refs[:n_in]
        (qa_ref, ka_ref, va_ref, qb_ref, kb_ref, vb_ref,
         kaf_ref, vaf_ref, kbf_ref, vbf_ref) = outs
    row = (pl.program_id(0) * tm) // seq if latent else 0
    h = _mod_norm(x_ref, g_ref, shift_ref, scale_ref, row)

    def head_norm(y, g_ref_):
        y = _rms(y, g_ref_[...])
        if latent:
            y = _rope(y, cos_ref[...], slo_ref[...], shi_ref[...], HEAD_DIM // 4)
        return y

    c_ka, c_va, c_qb = A_WIDTH, A_WIDTH + A_KV_WIDTH, A_WIDTH + 2 * A_KV_WIDTH
    c_kb, c_vb = c_qb + B_WIDTH, c_qb + 2 * B_WIDTH
    y = _dot(h, w_ref[:, c_kb:c_kb + B_WIDTH])
    kb_ref[...] = y.astype(BF16)
    if not latent:
        _store_heads(kbf_ref, y)
    y = _dot(h, w_ref[:, c_vb:c_vb + B_WIDTH])
    vb_ref[...] = y.astype(BF16)
    if not latent:
        _store_heads(vbf_ref, y)
    y = _dot(h, w_ref[:, c_ka:c_ka + A_KV_WIDTH])
    for hd in range(A_KV_HEADS):
        sl = slice(hd * HEAD_DIM, (hd + 1) * HEAD_DIM)
        k = head_norm(y[:, sl], kn_ref)
        ka_ref[:, sl] = k.astype(BF16)
        if not latent:
            kaf_ref[:, hd, :] = k
    y = _dot(h, w_ref[:, c_va:c_va + A_KV_WIDTH])
    va_ref[...] = y.astype(BF16)
    if not latent:
        _store_heads(vaf_ref, y)
    y = _dot(h, w_ref[:, :A_WIDTH])
    for hd in range(A_Q_HEADS):
        sl = slice(hd * HEAD_DIM, (hd + 1) * HEAD_DIM)
        qa_ref[:, sl] = (head_norm(y[:, sl], qn_ref) * Q_SCALE_L0).astype(BF16)
    y = _dot(h, w_ref[:, c_qb:c_qb + B_WIDTH])
    qb_ref[...] = (y * Q_SCALE_L0).astype(BF16)


def _in0(x, mod, g, w, qn, kn, rope_tabs, *, latent, seq, tm=512, side=None):
    m = x.shape[0]
    row_spec = lambda n: pl.BlockSpec((tm, n), lambda i: (i, 0))
    in_specs = [row_spec(D_MODEL)] + _mod_specs(latent, (0, 1)) + [
        _resident((1, D_MODEL)), _resident((D_MODEL, L0_QKV)),
        _resident((1, HEAD_DIM)), _resident((1, HEAD_DIM))]
    args = [x, mod, mod, g, w, qn, kn]
    if latent:
        nblk = seq // tm
        in_specs += [pl.BlockSpec((tm, LANES), lambda i: (i % nblk, 0))] * 3
        args += list(rope_tabs)
    widths = [A_WIDTH, A_KV_WIDTH, A_KV_WIDTH, B_WIDTH, B_WIDTH, B_WIDTH]
    out_specs = [row_spec(n) for n in widths]
    out_shape = [jax.ShapeDtypeStruct((m, n), BF16) for n in widths]
    if not latent:
        for nh in (A_KV_HEADS, A_KV_HEADS, B_HEADS, B_HEADS):
            out_specs.append(pl.BlockSpec((tm, nh, HEAD_DIM), lambda i: (i, 0, 0)))
            out_shape.append(jax.ShapeDtypeStruct((m, nh, HEAD_DIM), F32))
    if side:
        assert side.steps == m // tm
        in_specs += side.in_specs
        args += side.args
        out_specs += side.out_specs
        out_shape += side.out_shape
    return pl.pallas_call(
        functools.partial(_in0_kernel, latent=latent, tm=tm, seq=seq, side=side),
        grid=(m // tm,), in_specs=in_specs, out_specs=out_specs, out_shape=out_shape,
        compiler_params=_params(), name="in0_lat" if latent else "in0_ctx",
    )(*args)


def _attn_units(q_ref, kv_refs, o_ref, *, heads, group, sub, nbs=1):
    seq = q_ref.shape[0] // nbs
    for b in range(nbs):
        for j in range(heads // group):
            sl_kv = slice(j * HEAD_DIM, (j + 1) * HEAD_DIM)
            rows = lambda r: slice(b * (r.shape[0] // nbs), (b + 1) * (r.shape[0] // nbs))
            ks = [k_ref[rows(k_ref), sl_kv].astype(BF16) for k_ref, _ in kv_refs]
            vs = [_with_ones(v_ref[rows(v_ref), sl_kv]) for _, v_ref in kv_refs]
            sls = [slice((j * group + gi) * HEAD_DIM, (j * group + gi + 1) * HEAD_DIM)
                   for gi in range(group)]
            for r0 in range(b * seq, (b + 1) * seq, sub):
                q = jnp.concatenate([q_ref[r0:r0 + sub, sl] for sl in sls], axis=0)
                yield o_ref, r0, sub, sls, [_dot_nt(q, k) for k in ks], vs


def _attn_finish(unit):
    o_ref, r0, sub, sls, ss, vs = unit
    o = _softmax_pv(ss, vs).astype(BF16)
    for gi, sl in enumerate(sls):
        o_ref[r0:r0 + sub, sl] = o[gi * sub:(gi + 1) * sub]


def _attn_kernel(*refs, n_src, heads, group, sub, lag):
    q_ref = refs[0]
    kv_refs = [(refs[1 + 2 * s], refs[2 + 2 * s]) for s in range(n_src)]
    o_ref = refs[1 + 2 * n_src]
    _run_lagged(_attn_units(q_ref, kv_refs, o_ref, heads=heads, group=group, sub=sub),
                lag, _attn_finish)


def _attn(q, srcs, *, batch, seq, heads, group, name, lag=1):
    kvw = (heads // group) * HEAD_DIM
    in_specs = [pl.BlockSpec((seq, heads * HEAD_DIM), lambda b: (b, 0))]
    args = [q]
    for k, v, t in srcs:
        in_specs += [pl.BlockSpec((t, kvw), lambda b: (b, 0))] * 2
        args += [k, v]
    return pl.pallas_call(
        functools.partial(_attn_kernel, n_src=len(srcs), heads=heads, group=group,
                          sub=min(seq, 256), lag=lag),
        grid=(batch,), in_specs=in_specs,
        out_specs=pl.BlockSpec((seq, heads * HEAD_DIM), lambda b: (b, 0)),
        out_shape=jax.ShapeDtypeStruct((batch * seq, heads * HEAD_DIM), BF16),
        compiler_params=_params(), name=name,
    )(*args)


def _cache_side(caches, steps):
    dec_batch, past = caches[0].shape[:2]
    spb = steps // dec_batch
    rows = past // spb
    in_specs = [pl.BlockSpec((None, rows) + c.shape[2:],
                             lambda i, *_: (i // spb, i % spb, 0, 0)) for c in caches]
    widths = [c.shape[2] * HEAD_DIM for c in caches]
    out_specs = [pl.BlockSpec((rows, w), lambda i, *_: (i, 0)) for w in widths]
    out_shape = [jax.ShapeDtypeStruct((dec_batch * past, w), BF16) for w in widths]

    def body(ins, outs):
        for c_ref, o_ref in zip(ins, outs):
            o_ref[...] = c_ref[...].reshape(o_ref.shape).astype(BF16)

    return _SideWork(steps, list(caches), in_specs, out_specs, out_shape, body)


def _ctx_attn_kernel(qa_ref, ka_ref, va_ref, qb_ref, kb_ref, vb_ref, oa_ref, ob_ref, *, nbs, lag):
    sub = min(qa_ref.shape[0] // nbs, 256)
    units = itertools.chain(
        _attn_units(qa_ref, [(ka_ref, va_ref)], oa_ref, heads=A_Q_HEADS,
                    group=A_Q_HEADS // A_KV_HEADS, sub=sub, nbs=nbs),
        _attn_units(qb_ref, [(kb_ref, vb_ref)], ob_ref, heads=B_HEADS, group=1, sub=sub, nbs=nbs))
    _run_lagged(units, lag, _attn_finish)


def _ctx_attn(qa, ka, va, qb, kb, vb, *, batch, seq, nbs=4, lag=4):
    tok = lambda n: pl.BlockSpec((nbs * seq, n), lambda i: (i, 0))
    return pl.pallas_call(
        functools.partial(_ctx_attn_kernel, nbs=nbs, lag=lag),
        grid=(batch // nbs,),
        in_specs=[tok(A_WIDTH), tok(A_KV_WIDTH), tok(A_KV_WIDTH), tok(B_WIDTH), tok(B_WIDTH),
                  tok(B_WIDTH)],
        out_specs=[tok(A_WIDTH), tok(B_WIDTH)],
        out_shape=[jax.ShapeDtypeStruct((batch * seq, A_WIDTH), BF16),
                   jax.ShapeDtypeStruct((batch * seq, B_WIDTH), BF16)],
        compiler_params=_params(), name="ctx_attn",
    )(qa, ka, va, qb, kb, vb)


def _na_geometry(rows):
    wh = min(NA_WIN_H, rows)
    row_start = np.clip(np.arange(rows) - wh // 2, 0, rows - wh)
    return wh, row_start


def _na_bias_fill(rpb_ref, hd, o_ref, q_rows, rows):
    wh, row_start = _na_geometry(rows)
    nx = 2 * NA_WIN_W - 1
    qc = lax.broadcasted_iota(jnp.int32, (GRID_W, GRID_W), 0)
    kc = lax.broadcasted_iota(jnp.int32, (GRID_W, GRID_W), 1)
    col_start = jnp.clip(qc - NA_WIN_W // 2, 0, GRID_W - NA_WIN_W)
    col_ok = (kc >= col_start) & (kc < col_start + NA_WIN_W)
    dx = kc - qc + NA_WIN_W - 1
    base = hd * ((2 * NA_WIN_H - 1) * nx)
    tiles = {}
    for dy in range(-(wh - 1), wh):
        t = jnp.zeros((GRID_W, GRID_W), F32)
        for j in range(nx):
            t = jnp.where(dx == j, rpb_ref[base + (dy + NA_WIN_H - 1) * nx + j], t)
        tiles[dy] = jnp.where(col_ok, t * LOG2E, NEG_INF)
    o_ref[...] = jnp.full(o_ref.shape, NEG_INF, F32)
    for li, qr in enumerate(q_rows):
        for kr in range(int(row_start[qr]), int(row_start[qr]) + wh):
            o_ref[0, li * GRID_W:(li + 1) * GRID_W, kr * GRID_W:(kr + 1) * GRID_W] = tiles[kr - qr]


def _prep0_kernel(w_ref, rpb_ref, o_ref, bias_ref, *, rows, sph):
    o_ref[...] = w_ref[...].astype(BF16)
    i = pl.program_id(0)
    rps = rows // sph
    for part in range(sph):
        @pl.when(i % sph == part)
        def _():
            _na_bias_fill(rpb_ref, i // sph, bias_ref, range(part * rps, (part + 1) * rps), rows)


def _prep0(w0, rpb, seq):
    nb = WEIGHT_ROW_BLOCKS
    sph = nb // B_HEADS
    rows = seq // GRID_W
    wr = w0.shape[0] // nb
    return pl.pallas_call(
        functools.partial(_prep0_kernel, rows=rows, sph=sph),
        grid=(nb,),
        in_specs=[pl.BlockSpec((wr, L0_QKV), lambda i: (i, 0)),
                  pl.BlockSpec(memory_space=pltpu.SMEM)],
        out_specs=[pl.BlockSpec((wr, L0_QKV), lambda i: (i, 0)),
                   pl.BlockSpec((1, seq // sph, seq), lambda i: (i // sph, i % sph, 0))],
        out_shape=[jax.ShapeDtypeStruct((w0.shape[0], L0_QKV), BF16),
                   jax.ShapeDtypeStruct((B_HEADS, seq, seq), F32)],
        compiler_params=_params(), name="prep0",
    )(w0, rpb.reshape(-1))


def _l0_side(w0, w0o, cs, ada_w, ada_b, steps):
    cw = 512
    assert L0_QKV % cw == 0 and L0_WIDTH % cw == 0
    n_c = L0_WIDTH // cw
    r_blk = D_MODEL // steps
    n_ada = ada_w.shape[1]
    tn = n_ada // steps
    o_blk = pl.BlockSpec((w0o.shape[0] // steps, w0o.shape[1]), lambda i: (i, 0))
    in_specs = [pl.BlockSpec((r_blk, cw), lambda i, k=k: (i, L0_QKV // cw + k)) for k in range(n_c)]
    in_specs += [o_blk, _resident((ADA_ROWS, D_MODEL)),
                 pl.BlockSpec((D_MODEL, tn), lambda i: (0, i)),
                 pl.BlockSpec((1, tn), lambda i: (0, i))]
    out_specs = [pl.BlockSpec((r_blk, L0_WIDTH), lambda i: (i, 0)), o_blk,
                 pl.BlockSpec((ADA_ROWS, tn), lambda i: (0, i))]
    out_shape = [jax.ShapeDtypeStruct((D_MODEL, L0_WIDTH), BF16),
                 jax.ShapeDtypeStruct(w0o.shape, BF16),
                 jax.ShapeDtypeStruct((ADA_ROWS, n_ada), F32)]

    def body(ins, outs):
        outs[0][...] = jnp.concatenate([r[...] for r in ins[:n_c]], axis=1).astype(BF16)
        outs[1][...] = ins[n_c][...].astype(BF16)
        c_ref, aw_ref, ab_ref = ins[n_c + 1:]
        outs[2][...] = (_dot(_silu(c_ref[...]).astype(BF16), aw_ref[...].astype(BF16))
                        + ab_ref[...])

    args = [w0] * n_c + [w0o, cs, ada_w, ada_b.reshape(1, n_ada)]
    return _SideWork(steps, args, in_specs, out_specs, out_shape, body)


def _na_windows(rows, tq):
    wh, row_start = _na_geometry(rows)
    rq = tq // GRID_W
    wins = []
    for qb in range(rows // rq):
        lo = int(row_start[qb * rq]) * GRID_W
        hi = (int(row_start[qb * rq + rq - 1]) + wh) * GRID_W
        wins.append((lo // LANES * LANES, -(-hi // LANES) * LANES))
    return wins


def _na_kernel(q_ref, k_ref, v_ref, kc_ref, vc_ref, bias_ref, o_ref, *, seq, tq, wins, lag):
    nb = q_ref.shape[0] // seq
    past = kc_ref.shape[0] // nb

    def scores():
        for b in range(nb):
            kc = kc_ref[b * past:(b + 1) * past, :]
            vc = _with_ones(vc_ref[b * past:(b + 1) * past, :])
            v = _with_ones(v_ref[b * seq:(b + 1) * seq, :])
            for qb, (lo, hi) in enumerate(wins):
                rs = slice(qb * tq, (qb + 1) * tq)
                rows = slice(b * seq + qb * tq, b * seq + (qb + 1) * tq)
                q = q_ref[rows, :]
                s_loc = _dot_nt(q, k_ref[b * seq + lo:b * seq + hi, :]) + bias_ref[0, rs, lo:hi]
                yield rows, [s_loc, _dot_nt(q, kc)], [v[lo:hi], vc]

    def finish(unit):
        rows, ss, vs = unit
        o_ref[rows, :] = _softmax_pv(ss, vs).astype(BF16)

    _run_lagged(scores(), lag, finish)


def _na(q, k, v, kc, vc, bias, *, batch, seq, tq=256, nb=8, lag=1):
    past = kc.shape[0] // batch
    wins = _na_windows(seq // GRID_W, tq)
    tok = pl.BlockSpec((nb * seq, HEAD_DIM), lambda h, b: (b, h))
    cache = pl.BlockSpec((nb * past, HEAD_DIM), lambda h, b: (b, h))
    return pl.pallas_call(
        functools.partial(_na_kernel, seq=seq, tq=tq, wins=wins, lag=lag),
        grid=(B_HEADS, batch // nb),
        in_specs=[tok, tok, tok, cache, cache,
                  pl.BlockSpec((1, seq, seq), lambda h, b: (h, 0, 0))],
        out_specs=tok,
        out_shape=jax.ShapeDtypeStruct((batch * seq, B_WIDTH), BF16),
        compiler_params=_params(), name="na",
    )(q, k, v, kc, vc, bias)


def _out_kernel(*refs, latent, tm, seq, n_o, final, gate_t, side):
    x_ref, shift_ref, scale_ref, gate_ref, g_ref = refs[:5]
    o_refs = refs[5:5 + n_o]
    wg_ref, wo_ref = refs[5 + n_o:7 + n_o]
    n_fixed = 7 + n_o + (1 if final else 0)
    fn_ref = refs[n_fixed - 1] if final else None
    n_si = side.n_in if side else 0
    y_ref = refs[n_fixed + n_si]
    if side:
        side.body(refs[n_fixed:n_fixed + n_si], refs[n_fixed + n_si + 1:])
    row = (pl.program_id(0) * tm) // seq if latent else 0
    h = _mod_norm(x_ref, g_ref, shift_ref, scale_ref, row)
    gt = _dot_nt(h, wg_ref[...]) if gate_t else _dot(h, wg_ref[...])
    o = jnp.concatenate([r[...] for r in o_refs], axis=-1).astype(F32)
    t = (o * _silu(gt)).astype(BF16)
    y = x_ref[...] + gate_ref[pl.ds(row, 1), :] * _dot(t, wo_ref[...])
    if final:
        y = _rms(y, fn_ref[...])
    y_ref[...] = y


def _out(x, mod, g, os_, wg, wo, fn, *, latent, seq, name, tm=512, gate_t=False, side=None):
    m = x.shape[0]
    width = wo.shape[0]
    final = fn is not None
    in_specs = [pl.BlockSpec((tm, D_MODEL), lambda i: (i, 0))] + _mod_specs(latent, (0, 1, 2))
    in_specs += [_resident((1, D_MODEL))]
    in_specs += [pl.BlockSpec((tm, o.shape[1]), lambda i: (i, 0)) for o in os_]
    in_specs += [_resident(wg.shape), _resident((width, D_MODEL))]
    args = [x, mod, mod, mod, g, *os_, wg, wo]
    if final:
        in_specs.append(_resident((1, D_MODEL)))
        args.append(fn)
    out_specs = [pl.BlockSpec((tm, D_MODEL), lambda i: (i, 0))]
    out_shape = [jax.ShapeDtypeStruct((m, D_MODEL), F32)]
    if side:
        assert side.steps == m // tm
        in_specs += side.in_specs
        args += side.args
        out_specs += side.out_specs
        out_shape += side.out_shape
    res = pl.pallas_call(
        functools.partial(_out_kernel, latent=latent, tm=tm, seq=seq, n_o=len(os_), final=final,
                          gate_t=gate_t, side=side),
        grid=(m // tm,), in_specs=in_specs, out_specs=out_specs, out_shape=out_shape,
        compiler_params=_params(), name=name,
    )(*args)
    return res if side else res[0]


def _in1_body(x_ref, shift_ref, scale_ref, g_ref, w_ref, qan_ref, kvn_ref, wq_ref, rope_refs,
              q_ref, ckv_ref, kr_ref, row):
    h = _mod_norm(x_ref, g_ref, shift_ref, scale_ref, row)
    z = _dot_nt(h, w_ref[...])
    kr = z[:, Q_LORA + KV_LORA:]
    if rope_refs:
        cos, slo, shi = (r[...] for r in rope_refs)
        kr = _rope(kr, cos, slo, shi, QK_ROPE // 4)
    ckv_ref[...] = _rms(z[:, Q_LORA:Q_LORA + KV_LORA], kvn_ref[...]).astype(ckv_ref.dtype)
    kr_ref[...] = kr[:, :QK_ROPE].astype(kr_ref.dtype)
    q_lat = _rms(z[:, :Q_LORA], qan_ref[...]).astype(BF16)
    q = _dot(q_lat, wq_ref[...])
    for j in range(MLA_HEADS * MLA_QK // LANES):
        sl = slice(j * LANES, (j + 1) * LANES)
        qj = q[:, sl]
        if rope_refs and j % 2 == 1:
            qj = _rope(qj, cos, slo, shi, QK_ROPE // 4)
        q_ref[:, sl] = (qj * Q_SCALE_L1).astype(BF16)


def _in1_kernel(*refs, latent, tm, seq):
    n_in = 11 if latent else 8
    rope_refs = refs[8:11] if latent else None
    row = (pl.program_id(0) * tm) // seq if latent else 0
    _in1_body(*refs[:8], rope_refs, *refs[n_in:], row)


def _in1(x, mod, g, w, qan, kvn, wq, rope_tabs, *, latent, seq, tm=512):
    m = x.shape[0]
    row_spec = lambda n: pl.BlockSpec((tm, n), lambda i: (i, 0))
    in_specs = [row_spec(D_MODEL)] + _mod_specs(latent, (0, 1)) + [
        _resident((1, D_MODEL)), _resident(w.shape),
        _resident((1, Q_LORA)), _resident((1, KV_LORA)),
        _resident((Q_LORA, MLA_HEADS * MLA_QK))]
    args = [x, mod, mod, g, w, qan, kvn, wq]
    if latent:
        nblk = seq // tm
        in_specs += [pl.BlockSpec((tm, LANES), lambda i: (i % nblk, 0))] * 3
        args += list(rope_tabs)
    lat_dt = BF16 if latent else F32
    widths = [(MLA_HEADS * MLA_QK, BF16), (KV_LORA, lat_dt), (QK_ROPE, lat_dt)]
    return pl.pallas_call(
        functools.partial(_in1_kernel, latent=latent, tm=tm, seq=seq),
        grid=(m // tm,), in_specs=in_specs,
        out_specs=[row_spec(n) for n, _ in widths],
        out_shape=[jax.ShapeDtypeStruct((m, n), dt) for n, dt in widths],
        compiler_params=_params(), name="in1_lat" if latent else "in1_ctx",
    )(*args)


def _mla_body(q_ref, src, w_ref, o_ref, *, hb, nbs, seq, lag):
    n_src = len(src) // 2
    ckv = [src[2 * s][...].astype(BF16) for s in range(n_src)]
    krp = []
    for s in range(n_src):
        kr = src[2 * s + 1][...].astype(F32)
        pad = jnp.zeros((kr.shape[0], LANES - QK_ROPE), F32)
        krp.append(jnp.concatenate([kr, pad], axis=1).astype(BF16))
    kvw = QK_NOPE + V_DIM

    def scores():
        for hd in range(hb):
            kv = [_dot(c, w_ref[:, hd * kvw:(hd + 1) * kvw]).astype(BF16) for c in ckv]
            for b in range(nbs):
                rq = slice(b * seq, (b + 1) * seq)
                part = lambda a: a[b * (a.shape[0] // nbs):(b + 1) * (a.shape[0] // nbs)]
                kvb = [part(kv_s) for kv_s in kv]
                kcat = [jnp.concatenate([kv_s[:, :QK_NOPE], part(kr_s)], axis=1)
                        for kv_s, kr_s in zip(kvb, krp)]
                q = q_ref[rq, hd * MLA_QK:(hd + 1) * MLA_QK]
                yield (rq, hd, [_dot_nt(q, kc) for kc in kcat],
                       [_with_ones(kv_s[:, QK_NOPE:]) for kv_s in kvb])

    def finish(unit):
        rq, hd, ss, vs = unit
        o_ref[rq, hd * V_DIM:(hd + 1) * V_DIM] = _softmax_pv(ss, vs).astype(BF16)

    _run_lagged(scores(), lag, finish)


def _mla_kernel(*refs, n_src, hb, nbs, seq, lag):
    _mla_body(refs[0], refs[1:1 + 2 * n_src], refs[1 + 2 * n_src], refs[2 + 2 * n_src],
              hb=hb, nbs=nbs, seq=seq, lag=lag)


def _mla(q, srcs, w_kv, *, batch, seq, hb, name, lag=1):
    kvw = QK_NOPE + V_DIM
    in_specs = [pl.BlockSpec((seq, hb * MLA_QK), lambda b, j: (b, j))]
    args = [q]
    for ckv, kr, t in srcs:
        for a in (ckv, kr):
            wd = a.shape[-1]
            if a.ndim == 3:
                in_specs.append(pl.BlockSpec((None, t, wd), lambda b, j: (b, 0, 0)))
            else:
                in_specs.append(pl.BlockSpec((t, wd), lambda b, j: (b, 0)))
            args.append(a)
    in_specs.append(pl.BlockSpec((KV_LORA, hb * kvw), lambda b, j: (0, j)))
    args.append(w_kv)
    return pl.pallas_call(
        functools.partial(_mla_kernel, n_src=len(srcs), hb=hb, nbs=1, seq=seq, lag=lag),
        grid=(batch, MLA_HEADS // hb), in_specs=in_specs,
        out_specs=pl.BlockSpec((seq, hb * V_DIM), lambda b, j: (b, j)),
        out_shape=jax.ShapeDtypeStruct((batch * seq, L1_WIDTH), BF16),
        compiler_params=_params(), name=name,
    )(*args)


def _l1_ctx_kernel(*refs, nbs, seq, lag, side):
    fixed = refs[:9]
    n_si = side.n_in
    o_ref, ckv_ref, kr_ref = refs[9 + n_si:12 + n_si]
    q_scr = refs[-1]
    side.body(refs[9:9 + n_si], refs[12 + n_si:-1])
    _in1_body(*fixed[:8], None, q_scr, ckv_ref, kr_ref, 0)
    _mla_body(q_scr, [ckv_ref, kr_ref], fixed[8], o_ref, hb=MLA_HEADS, nbs=nbs, seq=seq, lag=lag)


def _l1_ctx(x, mod, g, w, qan, kvn, wq, w_kv, side, *, seq, nbs=2, lag=4):
    m = x.shape[0]
    tm = nbs * seq
    row_spec = lambda n: pl.BlockSpec((tm, n), lambda i: (i, 0))
    in_specs = [row_spec(D_MODEL)] + _mod_specs(False, (0, 1)) + [
        _resident((1, D_MODEL)), _resident(w.shape),
        _resident((1, Q_LORA)), _resident((1, KV_LORA)),
        _resident(wq.shape), _resident(w_kv.shape)]
    assert side.steps == m // tm
    widths = [(L1_WIDTH, BF16), (KV_LORA, F32), (QK_ROPE, F32)]
    return pl.pallas_call(
        functools.partial(_l1_ctx_kernel, nbs=nbs, seq=seq, lag=lag, side=side),
        grid=(m // tm,), in_specs=in_specs + side.in_specs,
        out_specs=[row_spec(n) for n, _ in widths] + side.out_specs,
        out_shape=[jax.ShapeDtypeStruct((m, n), dt) for n, dt in widths] + side.out_shape,
        scratch_shapes=[pltpu.VMEM((tm, MLA_HEADS * MLA_QK), BF16)],
        compiler_params=_params(), name="l1_ctx",
    )(x, mod, mod, g, w, qan, kvn, wq, w_kv, *side.args)


def _rope_tables(seq, d):
    a = d // 2
    t = np.arange(seq)
    row = (t // GRID_W).astype(np.float64)
    col = (t % GRID_W).astype(np.float64)
    inv = 1.0 / (ROPE_THETA ** (np.arange(a // 2, dtype=np.float64) * 2.0 / a))
    ang = np.concatenate([row[:, None] * inv] * 2 + [col[:, None] * inv] * 2, axis=-1)
    ang = np.tile(ang, (1, LANES // d))
    lower = np.tile((np.arange(d) % a) < a // 2, LANES // d)[None, :]
    cos, sin = np.cos(ang), np.sin(ang)
    tabs = (cos, np.where(lower, -sin, 0.0), np.where(lower, 0.0, sin))
    return tuple(jnp.asarray(x, F32) for x in tabs)


def kernel(x_prompt, x_sample, cache_l0_a_k, cache_l0_a_v, cache_l0_b_k, cache_l0_b_v, cache_l1_ckv, cache_l1_krope, c, c_ctx, l0_norm, l0_ada_w, l0_ada_b, l0_w_in, l0_q_norm, l0_k_norm, l0_rpb, l0_w_out, l1_norm, l1_ada_w, l1_ada_b, l1_w_in, l1_q_a_norm, l1_w_q_b, l1_kv_a_norm, l1_w_kv_b, l1_w_out, final_norm):
    batch, seq_c, _ = x_prompt.shape
    dec_batch, seq_l, _ = x_sample.shape
    past = cache_l0_a_k.shape[1]

    w0_qkv, bias = _prep0(l0_w_in, l0_rpb, seq_l)

    row2 = lambda v: v.reshape(1, -1)
    cs = jnp.zeros((ADA_ROWS, D_MODEL), F32).at[:dec_batch].set(c).at[8].set(c_ctx)
    mod0 = _adaln(cs, l0_ada_w, l0_ada_b)

    rope_a = _rope_tables(seq_l, HEAD_DIM)
    rope_c = _rope_tables(seq_l, QK_ROPE)

    tm = 512
    xc = x_prompt.reshape(batch * seq_c, D_MODEL)
    xl = x_sample.reshape(dec_batch * seq_l, D_MODEL)
    qa, ka, va, qb, kb, vb, ka_f, va_f, kb_f, vb_f = _in0(
        xc, mod0, row2(l0_norm), w0_qkv, row2(l0_q_norm), row2(l0_k_norm), None,
        latent=False, seq=seq_c, tm=tm)
    side = _l0_side(l0_w_in, l0_w_out, cs, l1_ada_w, l1_ada_b, dec_batch * seq_l // tm)
    qa_l, ka_l, va_l, qb_l, kb_l, vb_l, w0_gate, w0_out, mod1 = _in0(
        xl, mod0, row2(l0_norm), w0_qkv, row2(l0_q_norm), row2(l0_k_norm), rope_a,
        latent=True, seq=seq_l, tm=tm, side=side)

    oa, ob = _ctx_attn(qa, ka, va, qb, kb, vb, batch=batch, seq=seq_c)
    side = _l1_weight_side(l1_w_in.T, l1_w_q_b, l1_w_kv_b, l1_w_out, batch * seq_c // tm)
    xc, w1_all, w1_gate, wq, w_kv, w1_out = _out(
        xc, mod0, row2(l0_norm), [oa, ob], w0_gate, w0_out, None,
        latent=False, seq=seq_c, name="out0_ctx", tm=tm, side=side)
    l1_nbs = tm // seq_c
    side = _cache_side([cache_l0_a_k, cache_l0_a_v, cache_l0_b_k, cache_l0_b_v], batch // l1_nbs)
    oc, ckv_f, kr_f, ck_a, cv_a, ck_b, cv_b = _l1_ctx(
        xc, mod1, row2(l1_norm), w1_all, row2(l1_q_a_norm), row2(l1_kv_a_norm), wq, w_kv, side,
        seq=seq_c, nbs=l1_nbs)
    y_prompt = _out(xc, mod1, row2(l1_norm), [oc], w1_gate, w1_out, row2(final_norm),
                    latent=False, seq=seq_c, name="out1_ctx", gate_t=True)

    oa = _attn(qa_l, [(ck_a, cv_a, past), (ka_l, va_l, seq_l)],
               batch=dec_batch, seq=seq_l, heads=A_Q_HEADS, group=A_Q_HEADS // A_KV_HEADS,
               name="attn_a_lat")
    ob = _na(qb_l, kb_l, vb_l, ck_b, cv_b, bias, batch=dec_batch, seq=seq_l)
    xl = _out(xl, mod0, row2(l0_norm), [oa, ob], w0_gate, w0_out, None,
              latent=True, seq=seq_l, name="out0_lat")
    q1, ckv, kr = _in1(xl, mod1, row2(l1_norm), w1_all, row2(l1_q_a_norm),
                       row2(l1_kv_a_norm), wq, rope_c, latent=True, seq=seq_l, tm=1024)
    oc = _mla(q1, [(cache_l1_ckv, cache_l1_krope, past), (ckv, kr, seq_l)], w_kv,
              batch=dec_batch, seq=seq_l, hb=8, lag=1, name="mla_lat")
    y_sample = _out(xl, mod1, row2(l1_norm), [oc], w1_gate, w1_out, row2(final_norm),
                    latent=True, seq=seq_l, name="out1_lat", gate_t=True)

    return (y_prompt.reshape(batch, seq_c, D_MODEL),
            y_sample.reshape(dec_batch, seq_l, D_MODEL),
            ka_f.reshape(batch, seq_c, A_KV_HEADS, HEAD_DIM),
            va_f.reshape(batch, seq_c, A_KV_HEADS, HEAD_DIM),
            kb_f.reshape(batch, seq_c, B_HEADS, HEAD_DIM),
            vb_f.reshape(batch, seq_c, B_HEADS, HEAD_DIM),
            ckv_f.reshape(batch, seq_c, KV_LORA),
            kr_f.reshape(batch, seq_c, QK_ROPE))
```

```python
import functools
import itertools
import math
from typing import Callable, NamedTuple

import jax
import jax.numpy as jnp
import numpy as np
from jax import lax
from jax.experimental import pallas as pl
from jax.experimental.pallas import tpu as pltpu

LANES = 128
VMEM_LIMIT = 56 * 1024 * 1024
ADA_ROWS = 16
F32 = jnp.float32
BF16 = jnp.bfloat16

D_MODEL = 2048
GRID_W = 64
HEAD_DIM = 128
A_Q_HEADS = 8
A_KV_HEADS = 2
B_HEADS = 8
A_WIDTH = A_Q_HEADS * HEAD_DIM
A_KV_WIDTH = A_KV_HEADS * HEAD_DIM
B_WIDTH = B_HEADS * HEAD_DIM
L0_WIDTH = A_WIDTH + B_WIDTH
L0_QKV = A_WIDTH + 2 * A_KV_WIDTH + 3 * B_WIDTH
NA_WIN_H = 8
NA_WIN_W = 16
MLA_HEADS = 16
Q_LORA = 512
KV_LORA = 512
QK_NOPE = 128
QK_ROPE = 64
V_DIM = 128
L1_WIDTH = MLA_HEADS * V_DIM
L1_LAT = Q_LORA + KV_LORA + QK_ROPE
L1_LAT_PAD = L1_LAT + (LANES - QK_ROPE)
MLA_QK = 2 * LANES
ROPE_THETA = 10000.0
EPS = 1e-6
NEG_INF = -1e30
LOG2E = 1.4426950408889634
Q_SCALE_L0 = HEAD_DIM ** -0.5 * LOG2E
Q_SCALE_L1 = (QK_NOPE + QK_ROPE) ** -0.5 * LOG2E


def _params(vmem=VMEM_LIMIT):
    return pltpu.CompilerParams(vmem_limit_bytes=vmem)


def _resident(shape):
    nd = len(shape)
    return pl.BlockSpec(shape, lambda *_: (0,) * nd, pipeline_mode=pl.Buffered(1))


def _dot(a, b):
    return jnp.dot(a, b, preferred_element_type=F32)


def _dot_nt(a, b):
    return lax.dot_general(a, b, (((1,), (1,)), ((), ())), preferred_element_type=F32)


def _rms(x, g):
    return x * lax.rsqrt(jnp.mean(x * x, axis=-1, keepdims=True) + EPS) * g


def _silu(x):
    return x * jax.nn.sigmoid(x)


def _rope(x, cos, sin_lo, sin_hi, half):
    return (x * cos + pltpu.roll(x, half, 1) * sin_hi
            + pltpu.roll(x, LANES - half, 1) * sin_lo)


def _mod_norm(x_ref, g_ref, shift_ref, scale_ref, row):
    x = x_ref[...]
    h = _rms(x, g_ref[...]) * (1.0 + scale_ref[pl.ds(row, 1), :]) + shift_ref[pl.ds(row, 1), :]
    return h.astype(BF16)


def _with_ones(v):
    v = v.astype(BF16)
    return jnp.concatenate([v, jnp.ones_like(v)], axis=1)


def _softmax_pv(ss, vs):
    mx = ss[0].max(axis=-1, keepdims=True)
    for s in ss[1:]:
        mx = jnp.maximum(mx, s.max(axis=-1, keepdims=True))
    acc = _dot(jnp.exp2(ss[0] - mx).astype(BF16), vs[0])
    for s, v in zip(ss[1:], vs[1:]):
        acc = acc + _dot(jnp.exp2(s - mx).astype(BF16), v)
    return acc[:, :LANES] / acc[:, LANES:]


def _run_lagged(units, lag, finish):
    pending = []
    for unit in units:
        pending.append(unit)
        while len(pending) > lag:
            finish(pending.pop(0))
    for unit in pending:
        finish(unit)


WEIGHT_ROW_BLOCKS = 8


class _SideWork(NamedTuple):
    steps: int
    args: list
    in_specs: list
    out_specs: list
    out_shape: list
    body: Callable

    @property
    def n_in(self):
        return len(self.args)


def _l1_weight_side(w1_t, wq, wkv, w1o, steps):
    g_rows = L1_WIDTH // steps
    g_blk = math.gcd(L1_LAT, g_rows)
    g_n = g_rows // g_blk
    lat_rows = 128
    lat_n = L1_LAT_PAD // lat_rows
    row_blk = lambda a: pl.BlockSpec((a.shape[0] // steps, a.shape[1]), lambda i: (i, 0))
    lat_idx = lambda i: (jnp.minimum(i, lat_n - 1), 0)
    in_specs = [pl.BlockSpec((lat_rows, D_MODEL), lat_idx)]
    in_specs += [pl.BlockSpec((g_blk, D_MODEL), lambda i, k=k: (L1_LAT // g_blk + g_n * i + k, 0))
                 for k in range(g_n)]
    in_specs += [row_blk(wq), row_blk(wkv), row_blk(w1o)]
    args = [w1_t] * (1 + g_n) + [wq, wkv, w1o]
    out_dims = [(L1_LAT_PAD, D_MODEL), (L1_WIDTH, D_MODEL), (Q_LORA, MLA_HEADS * MLA_QK),
                wkv.shape, w1o.shape]
    out_specs = [pl.BlockSpec((lat_rows, D_MODEL), lat_idx)]
    out_specs += [pl.BlockSpec((r // steps, c), lambda i: (i, 0)) for r, c in out_dims[1:]]

    def body(ins, outs):
        lat_in, g_in = ins[0], ins[1:1 + g_n]
        wq_in, wkv_in, w1o_in = ins[1 + g_n:]
        lat_o, g_o, wq_o, wkv_o, w1o_o = outs

        @pl.when(pl.program_id(0) < lat_n)
        def _():
            lat_o[...] = lat_in[...].astype(BF16)

        for k, g_ref in enumerate(g_in):
            g_o[k * g_blk:(k + 1) * g_blk, :] = g_ref[...].astype(BF16)
        qw = QK_NOPE + QK_ROPE
        pad = jnp.zeros((wq_in.shape[0], MLA_QK - qw), F32)
        parts = []
        for hd in range(MLA_HEADS):
            parts += [wq_in[:, hd * qw:(hd + 1) * qw], pad]
        wq_o[...] = jnp.concatenate(parts, axis=1).astype(BF16)
        wkv_o[...] = wkv_in[...].astype(BF16)
        w1o_o[...] = w1o_in[...].astype(BF16)

    return _SideWork(steps, args, in_specs, out_specs,
                     [jax.ShapeDtypeStruct(d, BF16) for d in out_dims], body)


def _adaln_kernel(c_ref, w_ref, b_ref, o_ref):
    c = c_ref[...]
    o_ref[...] = _dot(_silu(c).astype(BF16), w_ref[...].astype(BF16)) + b_ref[...]


def _adaln(cs, w, b):
    tn = 1024
    n = w.shape[1]
    return pl.pallas_call(
        _adaln_kernel,
        grid=(n // tn,),
        in_specs=[pl.BlockSpec((ADA_ROWS, D_MODEL), lambda j: (0, 0)),
                  pl.BlockSpec((D_MODEL, tn), lambda j: (0, j)),
                  pl.BlockSpec((1, tn), lambda j: (0, j))],
        out_specs=pl.BlockSpec((ADA_ROWS, tn), lambda j: (0, j)),
        out_shape=jax.ShapeDtypeStruct((ADA_ROWS, n), F32),
        compiler_params=_params(),
        name="adaln",
    )(cs, w, b.reshape(1, n))


def _mod_specs(latent, cols):
    rb = 0 if latent else 1
    return [pl.BlockSpec((8, D_MODEL), lambda i, c=c: (rb, c)) for c in cols]


def _store_heads(o_ref, y):
    o_ref[...] = y.reshape(o_ref.shape)


def _in0_kernel(*refs, latent, tm, seq, side):
    n_in = 10 if latent else 7
    n_si = side.n_in if side else 0
    n_out = 6 if latent else 10
    outs = refs[n_in + n_si:n_in + n_si + n_out]
    if side:
        side.body(refs[n_in:n_in + n_si], refs[n_in + n_si + n_out:])
    if latent:
        (x_ref, shift_ref, scale_ref, g_ref, w_ref, qn_ref, kn_ref,
         cos_ref, slo_ref, shi_ref) = refs[:n_in]
        qa_ref, ka_ref, va_ref, qb_ref, kb_ref, vb_ref = outs
    else:
        x_ref, shift_ref, scale_ref, g_ref, w_ref, qn_ref, kn_ref = refs[:n_in]
        (qa_ref, ka_ref, va_ref, qb_ref, kb_ref, vb_ref,
         kaf_ref, vaf_ref, kbf_ref, vbf_ref) = outs
    row = (pl.program_id(0) * tm) // seq if latent else 0
    h = _mod_norm(x_ref, g_ref, shift_ref, scale_ref, row)

    def head_norm(y, g_ref_):
        y = _rms(y, g_ref_[...])
        if latent:
            y = _rope(y, cos_ref[...], slo_ref[...], shi_ref[...], HEAD_DIM // 4)
        return y

    c_ka, c_va, c_qb = A_WIDTH, A_WIDTH + A_KV_WIDTH, A_WIDTH + 2 * A_KV_WIDTH
    c_kb, c_vb = c_qb + B_WIDTH, c_qb + 2 * B_WIDTH
    y = _dot(h, w_ref[:, c_kb:c_kb + B_WIDTH])
    kb_ref[...] = y.astype(BF16)
    if not latent:
        _store_heads(kbf_ref, y)
    y = _dot(h, w_ref[:, c_vb:c_vb + B_WIDTH])
    vb_ref[...] = y.astype(BF16)
    if not latent:
        _store_heads(vbf_ref, y)
    y = _dot(h, w_ref[:, c_ka:c_ka + A_KV_WIDTH])
    for hd in range(A_KV_HEADS):
        sl = slice(hd * HEAD_DIM, (hd + 1) * HEAD_DIM)
        k = head_norm(y[:, sl], kn_ref)
        ka_ref[:, sl] = k.astype(BF16)
        if not latent:
            kaf_ref[:, hd, :] = k
    y = _dot(h, w_ref[:, c_va:c_va + A_KV_WIDTH])
    va_ref[...] = y.astype(BF16)
    if not latent:
        _store_heads(vaf_ref, y)
    y = _dot(h, w_ref[:, :A_WIDTH])
    for hd in range(A_Q_HEADS):
        sl = slice(hd * HEAD_DIM, (hd + 1) * HEAD_DIM)
        qa_ref[:, sl] = (head_norm(y[:, sl], qn_ref) * Q_SCALE_L0).astype(BF16)
    y = _dot(h, w_ref[:, c_qb:c_qb + B_WIDTH])
    qb_ref[...] = (y * Q_SCALE_L0).astype(BF16)


def _in0(x, mod, g, w, qn, kn, rope_tabs, *, latent, seq, tm=512, side=None):
    m = x.shape[0]
    row_spec = lambda n: pl.BlockSpec((tm, n), lambda i: (i, 0))
    in_specs = [row_spec(D_MODEL)] + _mod_specs(latent, (0, 1)) + [
        _resident((1, D_MODEL)), _resident((D_MODEL, L0_QKV)),
        _resident((1, HEAD_DIM)), _resident((1, HEAD_DIM))]
    args = [x, mod, mod, g, w, qn, kn]
    if latent:
        nblk = seq // tm
        in_specs += [pl.BlockSpec((tm, LANES), lambda i: (i % nblk, 0))] * 3
        args += list(rope_tabs)
    widths = [A_WIDTH, A_KV_WIDTH, A_KV_WIDTH, B_WIDTH, B_WIDTH, B_WIDTH]
    out_specs = [row_spec(n) for n in widths]
    out_shape = [jax.ShapeDtypeStruct((m, n), BF16) for n in widths]
    if not latent:
        for nh in (A_KV_HEADS, A_KV_HEADS, B_HEADS, B_HEADS):
            out_specs.append(pl.BlockSpec((tm, nh, HEAD_DIM), lambda i: (i, 0, 0)))
            out_shape.append(jax.ShapeDtypeStruct((m, nh, HEAD_DIM), F32))
    if side:
        assert side.steps == m // tm
        in_specs += side.in_specs
        args += side.args
        out_specs += side.out_specs
        out_shape += side.out_shape
    return pl.pallas_call(
        functools.partial(_in0_kernel, latent=latent, tm=tm, seq=seq, side=side),
        grid=(m // tm,), in_specs=in_specs, out_specs=out_specs, out_shape=out_shape,
        compiler_params=_params(), name="in0_lat" if latent else "in0_ctx",
    )(*args)


def _attn_units(q_ref, kv_refs, o_ref, *, heads, group, sub, nbs=1):
    seq = q_ref.shape[0] // nbs
    for b in range(nbs):
        for j in range(heads // group):
            sl_kv = slice(j * HEAD_DIM, (j + 1) * HEAD_DIM)
            rows = lambda r: slice(b * (r.shape[0] // nbs), (b + 1) * (r.shape[0] // nbs))
            ks = [k_ref[rows(k_ref), sl_kv].astype(BF16) for k_ref, _ in kv_refs]
            vs = [_with_ones(v_ref[rows(v_ref), sl_kv]) for _, v_ref in kv_refs]
            sls = [slice((j * group + gi) * HEAD_DIM, (j * group + gi + 1) * HEAD_DIM)
                   for gi in range(group)]
            for r0 in range(b * seq, (b + 1) * seq, sub):
                q = jnp.concatenate([q_ref[r0:r0 + sub, sl] for sl in sls], axis=0)
                yield o_ref, r0, sub, sls, [_dot_nt(q, k) for k in ks], vs


def _attn_finish(unit):
    o_ref, r0, sub, sls, ss, vs = unit
    o = _softmax_pv(ss, vs).astype(BF16)
    for gi, sl in enumerate(sls):
        o_ref[r0:r0 + sub, sl] = o[gi * sub:(gi + 1) * sub]


def _attn_kernel(*refs, n_src, heads, group, sub, lag):
    q_ref = refs[0]
    kv_refs = [(refs[1 + 2 * s], refs[2 + 2 * s]) for s in range(n_src)]
    o_ref = refs[1 + 2 * n_src]
    _run_lagged(_attn_units(q_ref, kv_refs, o_ref, heads=heads, group=group, sub=sub),
                lag, _attn_finish)


def _attn(q, srcs, *, batch, seq, heads, group, name, lag=1):
    kvw = (heads // group) * HEAD_DIM
    in_specs = [pl.BlockSpec((seq, heads * HEAD_DIM), lambda b: (b, 0))]
    args = [q]
    for k, v, t in srcs:
        in_specs += [pl.BlockSpec((t, kvw), lambda b: (b, 0))] * 2
        args += [k, v]
    return pl.pallas_call(
        functools.partial(_attn_kernel, n_src=len(srcs), heads=heads, group=group,
                          sub=min(seq, 256), lag=lag),
        grid=(batch,), in_specs=in_specs,
        out_specs=pl.BlockSpec((seq, heads * HEAD_DIM), lambda b: (b, 0)),
        out_shape=jax.ShapeDtypeStruct((batch * seq, heads * HEAD_DIM), BF16),
        compiler_params=_params(), name=name,
    )(*args)


def _cache_side(caches, steps):
    dec_batch, past = caches[0].shape[:2]
    spb = steps // dec_batch
    rows = past // spb
    in_specs = [pl.BlockSpec((None, rows) + c.shape[2:],
                             lambda i, *_: (i // spb, i % spb, 0, 0)) for c in caches]
    widths = [c.shape[2] * HEAD_DIM for c in caches]
    out_specs = [pl.BlockSpec((rows, w), lambda i, *_: (i, 0)) for w in widths]
    out_shape = [jax.ShapeDtypeStruct((dec_batch * past, w), BF16) for w in widths]

    def body(ins, outs):
        for c_ref, o_ref in zip(ins, outs):
            o_ref[...] = c_ref[...].reshape(o_ref.shape).astype(BF16)

    return _SideWork(steps, list(caches), in_specs, out_specs, out_shape, body)


def _ctx_attn_kernel(qa_ref, ka_ref, va_ref, qb_ref, kb_ref, vb_ref, oa_ref, ob_ref, *, nbs, lag):
    sub = min(qa_ref.shape[0] // nbs, 256)
    units = itertools.chain(
        _attn_units(qa_ref, [(ka_ref, va_ref)], oa_ref, heads=A_Q_HEADS,
                    group=A_Q_HEADS // A_KV_HEADS, sub=sub, nbs=nbs),
        _attn_units(qb_ref, [(kb_ref, vb_ref)], ob_ref, heads=B_HEADS, group=1, sub=sub, nbs=nbs))
    _run_lagged(units, lag, _attn_finish)


def _ctx_attn(qa, ka, va, qb, kb, vb, *, batch, seq, nbs=4, lag=4):
    tok = lambda n: pl.BlockSpec((nbs * seq, n), lambda i: (i, 0))
    return pl.pallas_call(
        functools.partial(_ctx_attn_kernel, nbs=nbs, lag=lag),
        grid=(batch // nbs,),
        in_specs=[tok(A_WIDTH), tok(A_KV_WIDTH), tok(A_KV_WIDTH), tok(B_WIDTH), tok(B_WIDTH),
                  tok(B_WIDTH)],
        out_specs=[tok(A_WIDTH), tok(B_WIDTH)],
        out_shape=[jax.ShapeDtypeStruct((batch * seq, A_WIDTH), BF16),
                   jax.ShapeDtypeStruct((batch * seq, B_WIDTH), BF16)],
        compiler_params=_params(), name="ctx_attn",
    )(qa, ka, va, qb, kb, vb)


def _na_geometry(rows):
    wh = min(NA_WIN_H, rows)
    row_start = np.clip(np.arange(rows) - wh // 2, 0, rows - wh)
    return wh, row_start


def _na_bias_fill(rpb_ref, hd, o_ref, q_rows, rows):
    wh, row_start = _na_geometry(rows)
    nx = 2 * NA_WIN_W - 1
    qc = lax.broadcasted_iota(jnp.int32, (GRID_W, GRID_W), 0)
    kc = lax.broadcasted_iota(jnp.int32, (GRID_W, GRID_W), 1)
    col_start = jnp.clip(qc - NA_WIN_W // 2, 0, GRID_W - NA_WIN_W)
    col_ok = (kc >= col_start) & (kc < col_start + NA_WIN_W)
    dx = kc - qc + NA_WIN_W - 1
    base = hd * ((2 * NA_WIN_H - 1) * nx)
    tiles = {}
    for dy in range(-(wh - 1), wh):
        t = jnp.zeros((GRID_W, GRID_W), F32)
        for j in range(nx):
            t = jnp.where(dx == j, rpb_ref[base + (dy + NA_WIN_H - 1) * nx + j], t)
        tiles[dy] = jnp.where(col_ok, t * LOG2E, NEG_INF)
    o_ref[...] = jnp.full(o_ref.shape, NEG_INF, F32)
    for li, qr in enumerate(q_rows):
        for kr in range(int(row_start[qr]), int(row_start[qr]) + wh):
            o_ref[0, li * GRID_W:(li + 1) * GRID_W, kr * GRID_W:(kr + 1) * GRID_W] = tiles[kr - qr]


def _prep0_kernel(w_ref, rpb_ref, o_ref, bias_ref, *, rows, sph):
    o_ref[...] = w_ref[...].astype(BF16)
    i = pl.program_id(0)
    rps = rows // sph
    for part in range(sph):
        @pl.when(i % sph == part)
        def _():
            _na_bias_fill(rpb_ref, i // sph, bias_ref, range(part * rps, (part + 1) * rps), rows)


def _prep0(w0, rpb, seq):
    nb = WEIGHT_ROW_BLOCKS
    sph = nb // B_HEADS
    rows = seq // GRID_W
    wr = w0.shape[0] // nb
    return pl.pallas_call(
        functools.partial(_prep0_kernel, rows=rows, sph=sph),
        grid=(nb,),
        in_specs=[pl.BlockSpec((wr, L0_QKV), lambda i: (i, 0)),
                  pl.BlockSpec(memory_space=pltpu.SMEM)],
        out_specs=[pl.BlockSpec((wr, L0_QKV), lambda i: (i, 0)),
                   pl.BlockSpec((1, seq // sph, seq), lambda i: (i // sph, i % sph, 0))],
        out_shape=[jax.ShapeDtypeStruct((w0.shape[0], L0_QKV), BF16),
                   jax.ShapeDtypeStruct((B_HEADS, seq, seq), F32)],
        compiler_params=_params(), name="prep0",
    )(w0, rpb.reshape(-1))


def _l0_side(w0, w0o, cs, ada_w, ada_b, steps):
    cw = 512
    assert L0_QKV % cw == 0 and L0_WIDTH % cw == 0
    n_c = L0_WIDTH // cw
    r_blk = D_MODEL // steps
    n_ada = ada_w.shape[1]
    tn = n_ada // steps
    o_blk = pl.BlockSpec((w0o.shape[0] // steps, w0o.shape[1]), lambda i: (i, 0))
    in_specs = [pl.BlockSpec((r_blk, cw), lambda i, k=k: (i, L0_QKV // cw + k)) for k in range(n_c)]
    in_specs += [o_blk, _resident((ADA_ROWS, D_MODEL)),
                 pl.BlockSpec((D_MODEL, tn), lambda i: (0, i)),
                 pl.BlockSpec((1, tn), lambda i: (0, i))]
    out_specs = [pl.BlockSpec((r_blk, L0_WIDTH), lambda i: (i, 0)), o_blk,
                 pl.BlockSpec((ADA_ROWS, tn), lambda i: (0, i))]
    out_shape = [jax.ShapeDtypeStruct((D_MODEL, L0_WIDTH), BF16),
                 jax.ShapeDtypeStruct(w0o.shape, BF16),
                 jax.ShapeDtypeStruct((ADA_ROWS, n_ada), F32)]

    def body(ins, outs):
        outs[0][...] = jnp.concatenate([r[...] for r in ins[:n_c]], axis=1).astype(BF16)
        outs[1][...] = ins[n_c][...].astype(BF16)
        c_ref, aw_ref, ab_ref = ins[n_c + 1:]
        outs[2][...] = (_dot(_silu(c_ref[...]).astype(BF16), aw_ref[...].astype(BF16))
                        + ab_ref[...])

    args = [w0] * n_c + [w0o, cs, ada_w, ada_b.reshape(1, n_ada)]
    return _SideWork(steps, args, in_specs, out_specs, out_shape, body)


def _na_windows(rows, tq):
    wh, row_start = _na_geometry(rows)
    rq = tq // GRID_W
    wins = []
    for qb in range(rows // rq):
        lo = int(row_start[qb * rq]) * GRID_W
        hi = (int(row_start[qb * rq + rq - 1]) + wh) * GRID_W
        wins.append((lo // LANES * LANES, -(-hi // LANES) * LANES))
    return wins


def _na_kernel(q_ref, k_ref, v_ref, kc_ref, vc_ref, bias_ref, o_ref, *, seq, tq, wins, lag):
    nb = q_ref.shape[0] // seq
    past = kc_ref.shape[0] // nb

    def scores():
        for b in range(nb):
            kc = kc_ref[b * past:(b + 1) * past, :]
            vc = _with_ones(vc_ref[b * past:(b + 1) * past, :])
            v = _with_ones(v_ref[b * seq:(b + 1) * seq, :])
            for qb, (lo, hi) in enumerate(wins):
                rs = slice(qb * tq, (qb + 1) * tq)
                rows = slice(b * seq + qb * tq, b * seq + (qb + 1) * tq)
                q = q_ref[rows, :]
                s_loc = _dot_nt(q, k_ref[b * seq + lo:b * seq + hi, :]) + bias_ref[0, rs, lo:hi]
                yield rows, [s_loc, _dot_nt(q, kc)], [v[lo:hi], vc]

    def finish(unit):
        rows, ss, vs = unit
        o_ref[rows, :] = _softmax_pv(ss, vs).astype(BF16)

    _run_lagged(scores(), lag, finish)


def _na(q, k, v, kc, vc, bias, *, batch, seq, tq=256, nb=8, lag=1):
    past = kc.shape[0] // batch
    wins = _na_windows(seq // GRID_W, tq)
    tok = pl.BlockSpec((nb * seq, HEAD_DIM), lambda h, b: (b, h))
    cache = pl.BlockSpec((nb * past, HEAD_DIM), lambda h, b: (b, h))
    return pl.pallas_call(
        functools.partial(_na_kernel, seq=seq, tq=tq, wins=wins, lag=lag),
        grid=(B_HEADS, batch // nb),
        in_specs=[tok, tok, tok, cache, cache,
                  pl.BlockSpec((1, seq, seq), lambda h, b: (h, 0, 0))],
        out_specs=tok,
        out_shape=jax.ShapeDtypeStruct((batch * seq, B_WIDTH), BF16),
        compiler_params=_params(), name="na",
    )(q, k, v, kc, vc, bias)


def _out_kernel(*refs, latent, tm, seq, n_o, final, gate_t, side):
    x_ref, shift_ref, scale_ref, gate_ref, g_ref = refs[:5]
    o_refs = refs[5:5 + n_o]
    wg_ref, wo_ref = refs[5 + n_o:7 + n_o]
    n_fixed = 7 + n_o + (1 if final else 0)
    fn_ref = refs[n_fixed - 1] if final else None
    n_si = side.n_in if side else 0
    y_ref = refs[n_fixed + n_si]
    if side:
        side.body(refs[n_fixed:n_fixed + n_si], refs[n_fixed + n_si + 1:])
    row = (pl.program_id(0) * tm) // seq if latent else 0
    h = _mod_norm(x_ref, g_ref, shift_ref, scale_ref, row)
    gt = _dot_nt(h, wg_ref[...]) if gate_t else _dot(h, wg_ref[...])
    o = jnp.concatenate([r[...] for r in o_refs], axis=-1).astype(F32)
    t = (o * _silu(gt)).astype(BF16)
    y = x_ref[...] + gate_ref[pl.ds(row, 1), :] * _dot(t, wo_ref[...])
    if final:
        y = _rms(y, fn_ref[...])
    y_ref[...] = y


def _out(x, mod, g, os_, wg, wo, fn, *, latent, seq, name, tm=512, gate_t=False, side=None):
    m = x.shape[0]
    width = wo.shape[0]
    final = fn is not None
    in_specs = [pl.BlockSpec((tm, D_MODEL), lambda i: (i, 0))] + _mod_specs(latent, (0, 1, 2))
    in_specs += [_resident((1, D_MODEL))]
    in_specs += [pl.BlockSpec((tm, o.shape[1]), lambda i: (i, 0)) for o in os_]
    in_specs += [_resident(wg.shape), _resident((width, D_MODEL))]
    args = [x, mod, mod, mod, g, *os_, wg, wo]
    if final:
        in_specs.append(_resident((1, D_MODEL)))
        args.append(fn)
    out_specs = [pl.BlockSpec((tm, D_MODEL), lambda i: (i, 0))]
    out_shape = [jax.ShapeDtypeStruct((m, D_MODEL), F32)]
    if side:
        assert side.steps == m // tm
        in_specs += side.in_specs
        args += side.args
        out_specs += side.out_specs
        out_shape += side.out_shape
    res = pl.pallas_call(
        functools.partial(_out_kernel, latent=latent, tm=tm, seq=seq, n_o=len(os_), final=final,
                          gate_t=gate_t, side=side),
        grid=(m // tm,), in_specs=in_specs, out_specs=out_specs, out_shape=out_shape,
        compiler_params=_params(), name=name,
    )(*args)
    return res if side else res[0]


def _in1_body(x_ref, shift_ref, scale_ref, g_ref, w_ref, qan_ref, kvn_ref, wq_ref, rope_refs,
              q_ref, ckv_ref, kr_ref, row):
    h = _mod_norm(x_ref, g_ref, shift_ref, scale_ref, row)
    z = _dot_nt(h, w_ref[...])
    kr = z[:, Q_LORA + KV_LORA:]
    if rope_refs:
        cos, slo, shi = (r[...] for r in rope_refs)
        kr = _rope(kr, cos, slo, shi, QK_ROPE // 4)
    ckv_ref[...] = _rms(z[:, Q_LORA:Q_LORA + KV_LORA], kvn_ref[...]).astype(ckv_ref.dtype)
    kr_ref[...] = kr[:, :QK_ROPE].astype(kr_ref.dtype)
    q_lat = _rms(z[:, :Q_LORA], qan_ref[...]).astype(BF16)
    q = _dot(q_lat, wq_ref[...])
    for j in range(MLA_HEADS * MLA_QK // LANES):
        sl = slice(j * LANES, (j + 1) * LANES)
        qj = q[:, sl]
        if rope_refs and j % 2 == 1:
            qj = _rope(qj, cos, slo, shi, QK_ROPE // 4)
        q_ref[:, sl] = (qj * Q_SCALE_L1).astype(BF16)


def _in1_kernel(*refs, latent, tm, seq):
    n_in = 11 if latent else 8
    rope_refs = refs[8:11] if latent else None
    row = (pl.program_id(0) * tm) // seq if latent else 0
    _in1_body(*refs[:8], rope_refs, *refs[n_in:], row)


def _in1(x, mod, g, w, qan, kvn, wq, rope_tabs, *, latent, seq, tm=512):
    m = x.shape[0]
    row_spec = lambda n: pl.BlockSpec((tm, n), lambda i: (i, 0))
    in_specs = [row_spec(D_MODEL)] + _mod_specs(latent, (0, 1)) + [
        _resident((1, D_MODEL)), _resident(w.shape),
        _resident((1, Q_LORA)), _resident((1, KV_LORA)),
        _resident((Q_LORA, MLA_HEADS * MLA_QK))]
    args = [x, mod, mod, g, w, qan, kvn, wq]
    if latent:
        nblk = seq // tm
        in_specs += [pl.BlockSpec((tm, LANES), lambda i: (i % nblk, 0))] * 3
        args += list(rope_tabs)
    lat_dt = BF16 if latent else F32
    widths = [(MLA_HEADS * MLA_QK, BF16), (KV_LORA, lat_dt), (QK_ROPE, lat_dt)]
    return pl.pallas_call(
        functools.partial(_in1_kernel, latent=latent, tm=tm, seq=seq),
        grid=(m // tm,), in_specs=in_specs,
        out_specs=[row_spec(n) for n, _ in widths],
        out_shape=[jax.ShapeDtypeStruct((m, n), dt) for n, dt in widths],
        compiler_params=_params(), name="in1_lat" if latent else "in1_ctx",
    )(*args)


def _mla_body(q_ref, src, w_ref, o_ref, *, hb, nbs, seq, lag, sub=None):
    sub = sub or seq
    n_src = len(src) // 2
    ckv = [src[2 * s][...].astype(BF16) for s in range(n_src)]
    krp = []
    for s in range(n_src):
        kr = src[2 * s + 1][...].astype(F32)
        pad = jnp.zeros((kr.shape[0], LANES - QK_ROPE), F32)
        krp.append(jnp.concatenate([kr, pad], axis=1).astype(BF16))
    kvw = QK_NOPE + V_DIM

    def scores():
        for hd in range(hb):
            kv = [_dot(c, w_ref[:, hd * kvw:(hd + 1) * kvw]).astype(BF16) for c in ckv]
            for b in range(nbs):
                part = lambda a: a[b * (a.shape[0] // nbs):(b + 1) * (a.shape[0] // nbs)]
                kvb = [part(kv_s) for kv_s in kv]
                kcat = [jnp.concatenate([kv_s[:, :QK_NOPE], part(kr_s)], axis=1)
                        for kv_s, kr_s in zip(kvb, krp)]
                vs = [_with_ones(kv_s[:, QK_NOPE:]) for kv_s in kvb]
                for r0 in range(b * seq, (b + 1) * seq, sub):
                    rq = slice(r0, r0 + sub)
                    q = q_ref[rq, hd * MLA_QK:(hd + 1) * MLA_QK]
                    yield rq, hd, [_dot_nt(q, kc) for kc in kcat], vs

    def finish(unit):
        rq, hd, ss, vs = unit
        o_ref[rq, hd * V_DIM:(hd + 1) * V_DIM] = _softmax_pv(ss, vs).astype(BF16)

    _run_lagged(scores(), lag, finish)


def _mla_kernel(*refs, n_src, hb, nbs, seq, lag, sub):
    _mla_body(refs[0], refs[1:1 + 2 * n_src], refs[1 + 2 * n_src], refs[2 + 2 * n_src],
              hb=hb, nbs=nbs, seq=seq, lag=lag, sub=sub)


def _mla(q, srcs, w_kv, *, batch, seq, hb, name, lag=1, sub=512):
    kvw = QK_NOPE + V_DIM
    in_specs = [pl.BlockSpec((seq, hb * MLA_QK), lambda b, j: (b, j))]
    args = [q]
    for ckv, kr, t in srcs:
        for a in (ckv, kr):
            wd = a.shape[-1]
            if a.ndim == 3:
                in_specs.append(pl.BlockSpec((None, t, wd), lambda b, j: (b, 0, 0)))
            else:
                in_specs.append(pl.BlockSpec((t, wd), lambda b, j: (b, 0)))
            args.append(a)
    in_specs.append(pl.BlockSpec((KV_LORA, hb * kvw), lambda b, j: (0, j)))
    args.append(w_kv)
    return pl.pallas_call(
        functools.partial(_mla_kernel, n_src=len(srcs), hb=hb, nbs=1, seq=seq, lag=lag, sub=sub),
        grid=(batch, MLA_HEADS // hb), in_specs=in_specs,
        out_specs=pl.BlockSpec((seq, hb * V_DIM), lambda b, j: (b, j)),
        out_shape=jax.ShapeDtypeStruct((batch * seq, L1_WIDTH), BF16),
        compiler_params=_params(), name=name,
    )(*args)


def _l1_ctx_kernel(*refs, nbs, seq, lag, side):
    fixed = refs[:9]
    n_si = side.n_in
    o_ref, ckv_ref, kr_ref = refs[9 + n_si:12 + n_si]
    q_scr = refs[-1]
    side.body(refs[9:9 + n_si], refs[12 + n_si:-1])
    _in1_body(*fixed[:8], None, q_scr, ckv_ref, kr_ref, 0)
    _mla_body(q_scr, [ckv_ref, kr_ref], fixed[8], o_ref, hb=MLA_HEADS, nbs=nbs, seq=seq, lag=lag)


def _l1_ctx(x, mod, g, w, qan, kvn, wq, w_kv, side, *, seq, nbs=2, lag=4):
    m = x.shape[0]
    tm = nbs * seq
    row_spec = lambda n: pl.BlockSpec((tm, n), lambda i: (i, 0))
    in_specs = [row_spec(D_MODEL)] + _mod_specs(False, (0, 1)) + [
        _resident((1, D_MODEL)), _resident(w.shape),
        _resident((1, Q_LORA)), _resident((1, KV_LORA)),
        _resident(wq.shape), _resident(w_kv.shape)]
    assert side.steps == m // tm
    widths = [(L1_WIDTH, BF16), (KV_LORA, F32), (QK_ROPE, F32)]
    return pl.pallas_call(
        functools.partial(_l1_ctx_kernel, nbs=nbs, seq=seq, lag=lag, side=side),
        grid=(m // tm,), in_specs=in_specs + side.in_specs,
        out_specs=[row_spec(n) for n, _ in widths] + side.out_specs,
        out_shape=[jax.ShapeDtypeStruct((m, n), dt) for n, dt in widths] + side.out_shape,
        scratch_shapes=[pltpu.VMEM((tm, MLA_HEADS * MLA_QK), BF16)],
        compiler_params=_params(), name="l1_ctx",
    )(x, mod, mod, g, w, qan, kvn, wq, w_kv, *side.args)


def _rope_tables(seq, d):
    a = d // 2
    t = np.arange(seq)
    row = (t // GRID_W).astype(np.float64)
    col = (t % GRID_W).astype(np.float64)
    inv = 1.0 / (ROPE_THETA ** (np.arange(a // 2, dtype=np.float64) * 2.0 / a))
    ang = np.concatenate([row[:, None] * inv] * 2 + [col[:, None] * inv] * 2, axis=-1)
    ang = np.tile(ang, (1, LANES // d))
    lower = np.tile((np.arange(d) % a) < a // 2, LANES // d)[None, :]
    cos, sin = np.cos(ang), np.sin(ang)
    tabs = (cos, np.where(lower, -sin, 0.0), np.where(lower, 0.0, sin))
    return tuple(jnp.asarray(x, F32) for x in tabs)


def kernel(x_prompt, x_sample, cache_l0_a_k, cache_l0_a_v, cache_l0_b_k, cache_l0_b_v, cache_l1_ckv, cache_l1_krope, c, c_ctx, l0_norm, l0_ada_w, l0_ada_b, l0_w_in, l0_q_norm, l0_k_norm, l0_rpb, l0_w_out, l1_norm, l1_ada_w, l1_ada_b, l1_w_in, l1_q_a_norm, l1_w_q_b, l1_kv_a_norm, l1_w_kv_b, l1_w_out, final_norm):
    batch, seq_c, _ = x_prompt.shape
    dec_batch, seq_l, _ = x_sample.shape
    past = cache_l0_a_k.shape[1]

    w0_qkv, bias = _prep0(l0_w_in, l0_rpb, seq_l)

    row2 = lambda v: v.reshape(1, -1)
    cs = jnp.zeros((ADA_ROWS, D_MODEL), F32).at[:dec_batch].set(c).at[8].set(c_ctx)
    mod0 = _adaln(cs, l0_ada_w, l0_ada_b)

    rope_a = _rope_tables(seq_l, HEAD_DIM)
    rope_c = _rope_tables(seq_l, QK_ROPE)

    tm = 512
    xc = x_prompt.reshape(batch * seq_c, D_MODEL)
    xl = x_sample.reshape(dec_batch * seq_l, D_MODEL)
    qa, ka, va, qb, kb, vb, ka_f, va_f, kb_f, vb_f = _in0(
        xc, mod0, row2(l0_norm), w0_qkv, row2(l0_q_norm), row2(l0_k_norm), None,
        latent=False, seq=seq_c, tm=tm)
    side = _l0_side(l0_w_in, l0_w_out, cs, l1_ada_w, l1_ada_b, dec_batch * seq_l // tm)
    qa_l, ka_l, va_l, qb_l, kb_l, vb_l, w0_gate, w0_out, mod1 = _in0(
        xl, mod0, row2(l0_norm), w0_qkv, row2(l0_q_norm), row2(l0_k_norm), rope_a,
        latent=True, seq=seq_l, tm=tm, side=side)

    oa, ob = _ctx_attn(qa, ka, va, qb, kb, vb, batch=batch, seq=seq_c)
    side = _l1_weight_side(l1_w_in.T, l1_w_q_b, l1_w_kv_b, l1_w_out, batch * seq_c // tm)
    xc, w1_all, w1_gate, wq, w_kv, w1_out = _out(
        xc, mod0, row2(l0_norm), [oa, ob], w0_gate, w0_out, None,
        latent=False, seq=seq_c, name="out0_ctx", tm=tm, side=side)
    l1_nbs = tm // seq_c
    side = _cache_side([cache_l0_a_k, cache_l0_a_v, cache_l0_b_k, cache_l0_b_v], batch // l1_nbs)
    oc, ckv_f, kr_f, ck_a, cv_a, ck_b, cv_b = _l1_ctx(
        xc, mod1, row2(l1_norm), w1_all, row2(l1_q_a_norm), row2(l1_kv_a_norm), wq, w_kv, side,
        seq=seq_c, nbs=l1_nbs)
    y_prompt = _out(xc, mod1, row2(l1_norm), [oc], w1_gate, w1_out, row2(final_norm),
                    latent=False, seq=seq_c, name="out1_ctx", gate_t=True)

    oa = _attn(qa_l, [(ck_a, cv_a, past), (ka_l, va_l, seq_l)],
               batch=dec_batch, seq=seq_l, heads=A_Q_HEADS, group=A_Q_HEADS // A_KV_HEADS,
               name="attn_a_lat")
    ob = _na(qb_l, kb_l, vb_l, ck_b, cv_b, bias, batch=dec_batch, seq=seq_l)
    xl = _out(xl, mod0, row2(l0_norm), [oa, ob], w0_gate, w0_out, None,
              latent=True, seq=seq_l, name="out0_lat")
    q1, ckv, kr = _in1(xl, mod1, row2(l1_norm), w1_all, row2(l1_q_a_norm),
                       row2(l1_kv_a_norm), wq, rope_c, latent=True, seq=seq_l, tm=1024)
    oc = _mla(q1, [(cache_l1_ckv, cache_l1_krope, past), (ckv, kr, seq_l)], w_kv,
              batch=dec_batch, seq=seq_l, hb=8, lag=1, name="mla_lat")
    y_sample = _out(xl, mod1, row2(l1_norm), [oc], w1_gate, w1_out, row2(final_norm),
                    latent=True, seq=seq_l, name="out1_lat", gate_t=True)

    return (y_prompt.reshape(batch, seq_c, D_MODEL),
            y_sample.reshape(dec_batch, seq_l, D_MODEL),
            ka_f.reshape(batch, seq_c, A_KV_HEADS, HEAD_DIM),
            va_f.reshape(batch, seq_c, A_KV_HEADS, HEAD_DIM),
            kb_f.reshape(batch, seq_c, B_HEADS, HEAD_DIM),
            vb_f.reshape(batch, seq_c, B_HEADS, HEAD_DIM),
            ckv_f.reshape(batch, seq_c, KV_LORA),
            kr_f.reshape(batch, seq_c, QK_ROPE))
```
